```python
import jax, jax.numpy as jnp
from jax import lax
import numpy as np

D_MODEL = 1024
BATCH = 8
SEQ = 2048
DEPTH = 1

D_MIX = D_MODEL
D_MLSTM = D_MIX // 2
D_ATTN = D_MIX - D_MLSTM
D_IN_PROJ = 2 * D_MLSTM + 3 * D_ATTN
MLSTM_HEADS = 4
MLSTM_HEAD_DIM = D_MLSTM // MLSTM_HEADS
MLSTM_QKV_BLOCK = 4
MLSTM_CONV = 5
MLSTM_CHUNK = 64
ATTN_HEADS = 8
ATTN_HEAD_DIM = D_ATTN // ATTN_HEADS
ROPE_DIM = ATTN_HEAD_DIM // 4
ROPE_THETA = 500000.0
DILATED_PATTERNS = ((128, 1), (512, 4), (2048, 16))
BAND_BLOCK = 64
N_EXPERTS = 16
EC_CAPACITY = 2
D_EXPERT = 2816
NORM_EPS = 1e-6
NEG_INF = -1e30

kernel_name = "hybrid_mlstm_dilated_attn_ec_moe"


def rms_norm(x, g):
    xf = x.astype(jnp.float32)
    y = xf * lax.rsqrt(jnp.mean(xf * xf, axis=-1, keepdims=True) + NORM_EPS)
    return (y * g.astype(jnp.float32)).astype(x.dtype)


def rope_partial(t, positions):
    half = ROPE_DIM // 2
    inv_freq = ROPE_THETA ** (-2.0 * jnp.arange(half, dtype=jnp.float32) / ROPE_DIM)
    ang = positions.astype(jnp.float32)[:, None] * inv_freq[None, :]
    cos, sin = jnp.cos(ang), jnp.sin(ang)
    t1, t2 = t[..., :half], t[..., half:ROPE_DIM]
    return jnp.concatenate([t1 * cos - t2 * sin, t2 * cos + t1 * sin, t[..., ROPE_DIM:]], axis=-1)


def mlstm_chunkwise(q, k, v, i_pre, f_pre):
    B, H, S, DH = q.shape
    L = MLSTM_CHUNK
    NC = S // L
    qc = q.reshape(B, H, NC, L, DH)
    kc = k.reshape(B, H, NC, L, DH)
    vc = v.reshape(B, H, NC, L, DH)
    ig = i_pre.reshape(B, H, NC, L)
    b = jnp.cumsum(jax.nn.log_sigmoid(f_pre).reshape(B, H, NC, L), axis=-1)
    g = b[..., -1]
    a = g[..., None] - b + ig

    def step(carry, inp):
        C, n, m = carry
        k_c, v_c, a_c, g_c = inp
        m_new = jnp.maximum(g_c + m, a_c.max(-1))
        decay = jnp.exp(g_c + m - m_new)
        w = jnp.exp(a_c - m_new[..., None])
        C_new = decay[..., None, None] * C + jnp.einsum('bhs,bhsd,bhse->bhde', w, v_c, k_c)
        n_new = decay[..., None] * n + jnp.einsum('bhs,bhse->bhe', w, k_c)
        return (C_new, n_new, m_new), (C, n, m)

    init = (jnp.zeros((B, H, DH, DH), jnp.float32), jnp.zeros((B, H, DH), jnp.float32),
            jnp.zeros((B, H), jnp.float32))
    xs = (jnp.moveaxis(kc, 2, 0), jnp.moveaxis(vc, 2, 0), jnp.moveaxis(a, 2, 0), jnp.moveaxis(g, 2, 0))
    _, (C_prev, n_prev, m_prev) = lax.scan(step, init, xs)
    C_prev = jnp.moveaxis(C_prev, 0, 2)
    n_prev = jnp.moveaxis(n_prev, 0, 2)
    m_prev = jnp.moveaxis(m_prev, 0, 2)

    lower = jnp.tril(jnp.ones((L, L), dtype=bool))
    log_D = jnp.where(lower, b[..., :, None] - b[..., None, :] + ig[..., None, :], NEG_INF)
    inter_log = b + m_prev[..., None]
    m_t = jnp.maximum(inter_log, log_D.max(-1))
    W = jnp.exp(log_D - m_t[..., None]) * jnp.einsum('bhctd,bhcsd->bhcts', qc, kc)
    inter_scale = jnp.exp(inter_log - m_t)
    num = (jnp.einsum('bhcts,bhcsd->bhctd', W, vc)
           + inter_scale[..., None] * jnp.einsum('bhcde,bhcte->bhctd', C_prev, qc))
    den = W.sum(-1) + inter_scale * jnp.einsum('bhcte,bhce->bhct', qc, n_prev)
    h = num / jnp.maximum(jnp.abs(den), jnp.exp(-m_t))[..., None]
    return h.reshape(B, H, S, DH)


def mlstm_mixer(x_m, z, conv_w, conv_b, wq, wk, wv, w_if_f, b_if_f, w_if_b, b_if_b, norm_g, skip):
    B, S, _ = x_m.shape
    H, DH = MLSTM_HEADS, MLSTM_HEAD_DIM
    xf = x_m.astype(jnp.float32)
    x_c = lax.conv_general_dilated(
        xf, conv_w.astype(jnp.float32)[:, None, :], window_strides=(1,),
        padding=[(MLSTM_CONV // 2, MLSTM_CONV // 2)], dimension_numbers=('NWC', 'WIO', 'NWC'),
        feature_group_count=D_MLSTM) + conv_b.astype(jnp.float32)
    x_c = jax.nn.silu(x_c)

    def blockdiag(t, w):
        tb = t.reshape(B, S, D_MLSTM // MLSTM_QKV_BLOCK, MLSTM_QKV_BLOCK)
        return jnp.einsum('bsgi,gio->bsgo', tb, w.astype(jnp.float32)).reshape(B, S, D_MLSTM)

    q = blockdiag(x_c, wq)
    k = blockdiag(x_c, wk) * (MLSTM_HEAD_DIM ** -0.5)
    v = blockdiag(xf, wv)
    qkv = jnp.concatenate([q, k, v], axis=-1)

    def heads(t):
        return t.reshape(B, S, H, DH).transpose(0, 2, 1, 3)

    qh, kh, vh = heads(q), heads(k), heads(v)

    def run(w_if, b_if, flip):
        gates = qkv @ w_if.astype(jnp.float32) + b_if.astype(jnp.float32)
        args = (qh, kh, vh, gates[..., :H].transpose(0, 2, 1), gates[..., H:].transpose(0, 2, 1))
        if flip:
            args = tuple(jnp.flip(t, axis=2) for t in args)
        h = mlstm_chunkwise(*args)
        return jnp.flip(h, axis=2) if flip else h

    h = run(w_if_f, b_if_f, False) + run(w_if_b, b_if_b, True)
    h = rms_norm(h.transpose(0, 2, 1, 3), norm_g.reshape(H, DH)).reshape(B, S, D_MLSTM)
    return (h + skip.astype(jnp.float32) * x_c) * jax.nn.silu(z.astype(jnp.float32))


def banded_window_stats(q, k, v, half):
    lead = q.shape[:-2]
    L, D = q.shape[-2], q.shape[-1]
    blk = BAND_BLOCK
    nb = -(-L // blk)
    pad = nb * blk - L
    padw = [(0, 0)] * len(lead)
    qb = jnp.pad(q, padw + [(0, pad), (0, 0)]).reshape(*lead, nb, blk, D)

    def windows(t):
        tp = jnp.pad(t, padw + [(blk, blk + pad), (0, 0)]).reshape(*lead, nb + 2, blk, D)
        return jnp.concatenate([tp[..., :-2, :, :], tp[..., 1:-1, :, :], tp[..., 2:, :, :]], axis=-2)

    kw, vw = windows(k), windows(v)
    qpos = jnp.arange(nb)[:, None] * blk + jnp.arange(blk)[None, :]
    kpos = jnp.arange(nb)[:, None] * blk - blk + jnp.arange(3 * blk)[None, :]
    mask = ((jnp.abs(qpos[:, :, None] - kpos[:, None, :]) <= half)
            & (kpos[:, None, :] >= 0) & (kpos[:, None, :] < L))
    scores = jnp.where(mask, jnp.einsum('...nqd,...nkd->...nqk', qb, kw), NEG_INF)
    m = scores.max(-1)
    p = jnp.exp(scores - m[..., None])
    s = p.sum(-1)
    num = jnp.einsum('...nqk,...nkd->...nqd', p, vw)
    return (m.reshape(*lead, nb * blk)[..., :L], s.reshape(*lead, nb * blk)[..., :L],
            num.reshape(*lead, nb * blk, D)[..., :L, :])


def dilated_branch(q, k, v, dil, half):
    B, H, S, D = q.shape

    def to_res(t):
        return t.reshape(B, H, S // dil, dil, D).swapaxes(2, 3)

    m, s, num = banded_window_stats(to_res(q), to_res(k), to_res(v), half)
    return (m.swapaxes(2, 3).reshape(B, H, S), s.swapaxes(2, 3).reshape(B, H, S),
            num.swapaxes(2, 3).reshape(B, H, S, D))


def dilated_attention_mixer(qa, ka, va):
    B, S, _ = qa.shape

    def heads(t):
        return t.astype(jnp.float32).reshape(B, S, ATTN_HEADS, ATTN_HEAD_DIM).transpose(0, 2, 1, 3)

    pos = jnp.arange(S)
    q = rope_partial(heads(qa), pos) * (ATTN_HEAD_DIM ** -0.5)
    k = rope_partial(heads(ka), pos)
    v = heads(va)
    stats = [dilated_branch(q, k, v, dil, win // (2 * dil)) for win, dil in DILATED_PATTERNS]
    m_all = jnp.stack([st[0] for st in stats])
    w = jnp.exp(m_all - m_all.max(0))
    den = jnp.sum(w * jnp.stack([st[1] for st in stats]), axis=0)
    num = jnp.sum(w[..., None] * jnp.stack([st[2] for st in stats]), axis=0)
    out = num / den[..., None]
    return out.transpose(0, 2, 1, 3).reshape(B, S, D_ATTN)


def expert_choice_ffn(h, w_router, w1, w3, w2):
    B, S, D = h.shape
    cap = EC_CAPACITY * S // N_EXPERTS
    logits = jnp.einsum('bsd,de->bse', h.astype(jnp.float32), w_router.astype(jnp.float32))
    aff = jax.nn.softmax(logits, axis=-1)
    gate, idx = lax.top_k(aff.transpose(0, 2, 1), cap)
    xs = jax.vmap(lambda hb, ib: hb[ib])(h, idx)
    up = jnp.einsum('becd,edf->becf', xs, w1)
    gt = jnp.einsum('becd,edf->becf', xs, w3)
    y = jnp.einsum('becf,efd->becd', jax.nn.silu(up) * gt, w2) * gate[..., None].astype(h.dtype)
    return jax.vmap(lambda yb, ib: jax.ops.segment_sum(
        yb.reshape(-1, D), ib.reshape(-1), num_segments=S))(y, idx)


def setup_inputs(seed: int = 0) -> dict:
    key = jax.random.key(seed)
    ks = jax.random.split(key, 24)

    def nrm(k, shape, scale):
        return jax.random.normal(k, shape, jnp.float32) * scale

    H = MLSTM_HEADS
    nblk = D_MLSTM // MLSTM_QKV_BLOCK
    f_bias = jnp.linspace(3.0, 6.0, H, dtype=jnp.float32)[None, :]
    return {
        "x": nrm(ks[0], (BATCH, SEQ, D_MODEL), 1.0),
        "norm1_g": 1.0 + nrm(ks[1], (DEPTH, D_MODEL), 0.02),
        "w_in": nrm(ks[2], (DEPTH, D_MODEL, D_IN_PROJ), D_MODEL ** -0.5),
        "conv_w": nrm(ks[3], (DEPTH, MLSTM_CONV, D_MLSTM), MLSTM_CONV ** -0.5),
        "conv_b": nrm(ks[4], (DEPTH, D_MLSTM), 0.01),
        "wq_m": nrm(ks[5], (DEPTH, nblk, MLSTM_QKV_BLOCK, MLSTM_QKV_BLOCK), MLSTM_QKV_BLOCK ** -0.5),
        "wk_m": nrm(ks[6], (DEPTH, nblk, MLSTM_QKV_BLOCK, MLSTM_QKV_BLOCK), MLSTM_QKV_BLOCK ** -0.5),
        "wv_m": nrm(ks[7], (DEPTH, nblk, MLSTM_QKV_BLOCK, MLSTM_QKV_BLOCK), MLSTM_QKV_BLOCK ** -0.5),
        "w_if_fwd": nrm(ks[8], (DEPTH, 3 * D_MLSTM, 2 * H), (3 * D_MLSTM) ** -0.5),
        "b_if_fwd": jnp.concatenate([nrm(ks[9], (DEPTH, H), 0.1),
                                     f_bias + nrm(ks[10], (DEPTH, H), 0.1)], axis=-1),
        "w_if_bwd": nrm(ks[11], (DEPTH, 3 * D_MLSTM, 2 * H), (3 * D_MLSTM) ** -0.5),
        "b_if_bwd": jnp.concatenate([nrm(ks[12], (DEPTH, H), 0.1),
                                     f_bias + nrm(ks[13], (DEPTH, H), 0.1)], axis=-1),
        "mlstm_norm_g": 1.0 + nrm(ks[14], (DEPTH, D_MLSTM), 0.02),
        "mlstm_skip": 1.0 + nrm(ks[15], (DEPTH, D_MLSTM), 0.02),
        "attn_norm_g": 1.0 + nrm(ks[16], (DEPTH, D_ATTN), 0.02),
        "w_out": nrm(ks[17], (DEPTH, D_MIX, D_MODEL), D_MIX ** -0.5),
        "norm2_g": 1.0 + nrm(ks[18], (DEPTH, D_MODEL), 0.02),
        "w_router": nrm(ks[19], (DEPTH, D_MODEL, N_EXPERTS), D_MODEL ** -0.5),
        "w1": nrm(ks[20], (DEPTH, N_EXPERTS, D_MODEL, D_EXPERT), D_MODEL ** -0.5),
        "w3": nrm(ks[21], (DEPTH, N_EXPERTS, D_MODEL, D_EXPERT), D_MODEL ** -0.5),
        "w2": nrm(ks[22], (DEPTH, N_EXPERTS, D_EXPERT, D_MODEL), D_EXPERT ** -0.5),
        "norm_f_g": 1.0 + nrm(ks[23], (D_MODEL,), 0.02),
    }


def reference(x, norm1_g, w_in, conv_w, conv_b, wq_m, wk_m, wv_m, w_if_fwd, b_if_fwd,
              w_if_bwd, b_if_bwd, mlstm_norm_g, mlstm_skip, attn_norm_g, w_out, norm2_g,
              w_router, w1, w3, w2, norm_f_g):
    splits = [D_MLSTM, 2 * D_MLSTM, 2 * D_MLSTM + D_ATTN, 2 * D_MLSTM + 2 * D_ATTN]
    for l in range(DEPTH):
        h = rms_norm(x, norm1_g[l])
        proj = h @ w_in[l]
        x_m, z, qa, ka, va = jnp.split(proj, splits, axis=-1)
        y_m = mlstm_mixer(x_m, z, conv_w[l], conv_b[l], wq_m[l], wk_m[l], wv_m[l],
                          w_if_fwd[l], b_if_fwd[l], w_if_bwd[l], b_if_bwd[l],
                          mlstm_norm_g[l], mlstm_skip[l])
        y_a = rms_norm(dilated_attention_mixer(qa, ka, va), attn_norm_g[l])
        mixed = jnp.concatenate([y_m, y_a], axis=-1).astype(x.dtype)
        x = x + mixed @ w_out[l]
        h = rms_norm(x, norm2_g[l])
        x = x + expert_choice_ffn(h, w_router[l], w1[l], w3[l], w2[l])
    return rms_norm(x, norm_f_g)
```

```python
import functools

import jax
import jax.numpy as jnp
from jax import lax
from jax.experimental import pallas as pl
from jax.experimental.pallas import tpu as pltpu

F32 = jnp.float32
BF16 = jnp.bfloat16

D_MODEL = 1024
D_MLSTM = 512
D_ATTN = 512
D_IN_PROJ = 2 * D_MLSTM + 3 * D_ATTN
MLSTM_HEADS = 4
MLSTM_HEAD_DIM = 128
MLSTM_QKV_BLOCK = 4
MLSTM_CONV = 5
ATTN_HEADS = 8
ATTN_HEAD_DIM = 64
ROPE_DIM = 16
ROPE_THETA = 500000.0
DILATED_PATTERNS = ((128, 1), (512, 4), (2048, 16))
N_EXPERTS = 16
EC_CAPACITY = 2
D_EXPERT = 2816
NORM_EPS = 1e-6
NEG_INF = -1e30

LANES = 128
MLSTM_CHUNK = 128
ROW_TILE = 512
ATTN_Q_TILE = 128
FFN_F_TILE = 256
FFN_ROW_TILE = 512
THRESHOLD_MANTISSA_STEPS = 40
VMEM_LIMIT = 56 * 1024 * 1024


def _params(sem):
    return pltpu.CompilerParams(dimension_semantics=sem, vmem_limit_bytes=VMEM_LIMIT)


def _rms(x, g):
    return x * lax.rsqrt(jnp.mean(x * x, axis=-1, keepdims=True) + NORM_EPS) * g


def _silu(x):
    return x * (1.0 / (1.0 + jnp.exp(-x)))


def _dot(a, b):
    return jnp.dot(a, b, preferred_element_type=F32)


def _dot_nt(a, b):
    return lax.dot_general(a, b, (((1,), (1,)), ((), ())), preferred_element_type=F32)


def _in_proj_kernel(x_ref, g_ref, w_ref, cos_ref, sa_ref, sb_ref,
                    xm_ref, z_ref, q_ref, k_ref, v_ref):
    h = _rms(x_ref[...], g_ref[...])
    p = _dot(h.astype(BF16), w_ref[...])
    xm_ref[...] = p[:, :D_MLSTM]
    z_ref[...] = p[:, D_MLSTM:2 * D_MLSTM]
    cos, sa, sb = cos_ref[...], sa_ref[...], sb_ref[...]
    half = ROPE_DIM // 2

    def rope(t):
        outs = []
        for j in range(D_ATTN // LANES):
            tj = t[:, j * LANES:(j + 1) * LANES]
            up = pltpu.roll(tj, LANES - half, axis=1)
            dn = pltpu.roll(tj, half, axis=1)
            outs.append(tj * cos + up * sa + dn * sb)
        return jnp.concatenate(outs, axis=1)

    o = 2 * D_MLSTM
    q_ref[...] = (rope(p[:, o:o + D_ATTN]) * (ATTN_HEAD_DIM ** -0.5)).astype(BF16)
    k_ref[...] = rope(p[:, o + D_ATTN:o + 2 * D_ATTN]).astype(BF16)
    v_ref[...] = p[:, o + 2 * D_ATTN:].astype(BF16)


def _rope_tables(seq):
    half = ROPE_DIM // 2
    inv_freq = ROPE_THETA ** (-2.0 * jnp.arange(half, dtype=F32) / ROPE_DIM)
    ang = jnp.arange(seq).astype(F32)[:, None] * inv_freq[None, :]
    cos, sin = jnp.cos(ang), jnp.sin(ang)
    pad = jnp.zeros((seq, ATTN_HEAD_DIM - ROPE_DIM), F32)
    cos_h = jnp.concatenate([cos, cos, pad + 1.0], axis=1)
    sa_h = jnp.concatenate([-sin, jnp.zeros_like(sin), pad], axis=1)
    sb_h = jnp.concatenate([jnp.zeros_like(sin), sin, pad], axis=1)
    rep = LANES // ATTN_HEAD_DIM
    return tuple(jnp.tile(t, (1, rep)) for t in (cos_h, sa_h, sb_h))


def _in_proj(x2d, g, w_bf, seq):
    n = x2d.shape[0]
    tiles_per_seq = seq // ROW_TILE
    cos, sa, sb = _rope_tables(seq)
    row = lambda i: (i, 0)
    fixed = lambda i: (0, 0)
    pos = lambda i: (i % tiles_per_seq, 0)
    return pl.pallas_call(
        _in_proj_kernel,
        grid=(n // ROW_TILE,),
        in_specs=[
            pl.BlockSpec((ROW_TILE, D_MODEL), row),
            pl.BlockSpec((1, D_MODEL), fixed),
            pl.BlockSpec((D_MODEL, D_IN_PROJ), fixed),
            pl.BlockSpec((ROW_TILE, LANES), pos),
            pl.BlockSpec((ROW_TILE, LANES), pos),
            pl.BlockSpec((ROW_TILE, LANES), pos),
        ],
        out_specs=[
            pl.BlockSpec((ROW_TILE, D_MLSTM), row),
            pl.BlockSpec((ROW_TILE, D_MLSTM), row),
            pl.BlockSpec((ROW_TILE, D_ATTN), row),
            pl.BlockSpec((ROW_TILE, D_ATTN), row),
            pl.BlockSpec((ROW_TILE, D_ATTN), row),
        ],
        out_shape=[
            jax.ShapeDtypeStruct((n, D_MLSTM), F32),
            jax.ShapeDtypeStruct((n, D_MLSTM), F32),
            jax.ShapeDtypeStruct((n, D_ATTN), BF16),
            jax.ShapeDtypeStruct((n, D_ATTN), BF16),
            jax.ShapeDtypeStruct((n, D_ATTN), BF16),
        ],
        compiler_params=_params(("parallel",)),
        name="in_proj",
    )(x2d, g, w_bf, cos, sa, sb)


def _mlstm_pre_kernel(xm_ref, cw_ref, cb_ref, wq_ref, wk_ref, wv_ref, wif_ref, bif_ref,
                      xc_ref, q_ref, k_ref, v_ref, g_ref):
    x = xm_ref[...]
    seq = x.shape[0]
    t = lax.broadcasted_iota(jnp.int32, x.shape, 0)
    acc = jnp.zeros_like(x) + cb_ref[...]
    for j in range(MLSTM_CONV):
        d = j - MLSTM_CONV // 2
        if d == 0:
            tap = x
        else:
            tap = pltpu.roll(x, (-d) % seq, axis=0)
            tap = jnp.where((t + d >= 0) & (t + d < seq), tap, 0.0)
        acc = acc + tap * cw_ref[j:j + 1, :]
    xc = _silu(acc)
    xc_ref[...] = xc
    xcb = xc.astype(BF16)
    q = _dot(xcb, wq_ref[...])
    k = _dot(xcb, wk_ref[...]) * (MLSTM_HEAD_DIM ** -0.5)
    v = _dot(x.astype(BF16), wv_ref[...])
    q_ref[...] = q.astype(BF16)
    k_ref[...] = k.astype(BF16)
    v_ref[...] = v.astype(BF16)
    qkv = jnp.concatenate([q, k, v], axis=1).astype(BF16)
    g_ref[...] = _dot_nt(wif_ref[...], qkv) + bif_ref[...]


def _block_diag(w):
    nblk = w.shape[0]
    eye = jnp.eye(nblk, dtype=w.dtype)
    dense = eye[:, None, :, None] * w[:, :, None, :]
    return dense.reshape(nblk * MLSTM_QKV_BLOCK, nblk * MLSTM_QKV_BLOCK)


def _gate_rows(w_f, b_f, w_b, b_b):
    h = MLSTM_HEADS
    cols = []
    bias = []
    zero_w = jnp.zeros((w_f.shape[0],), F32)
    for hd in range(h):
        cols += [w_f[:, hd], w_f[:, h + hd], w_b[:, hd], w_b[:, h + hd]] + [zero_w] * 4
        bias += [b_f[hd], b_f[h + hd], b_b[hd], b_b[h + hd]] + [jnp.zeros((), F32)] * 4
    return jnp.stack(cols, axis=0), jnp.stack(bias)[:, None]


def _mlstm_pre(xm, conv_w, conv_b, wq, wk, wv, wif_rows, bif_rows):
    b, seq, _ = xm.shape
    nrow = wif_rows.shape[0]
    per_b = lambda i: (i, 0, 0)
    fixed = lambda i: (0, 0)
    return pl.pallas_call(
        _mlstm_pre_kernel,
        grid=(b,),
        in_specs=[
            pl.BlockSpec((None, seq, D_MLSTM), per_b),
            pl.BlockSpec((MLSTM_CONV, D_MLSTM), fixed),
            pl.BlockSpec((1, D_MLSTM), fixed),
            pl.BlockSpec((D_MLSTM, D_MLSTM), fixed),
            pl.BlockSpec((D_MLSTM, D_MLSTM), fixed),
            pl.BlockSpec((D_MLSTM, D_MLSTM), fixed),
            pl.BlockSpec((nrow, 3 * D_MLSTM), fixed),
            pl.BlockSpec((nrow, 1), fixed),
        ],
        out_specs=[
            pl.BlockSpec((None, seq, D_MLSTM), per_b),
            pl.BlockSpec((None, seq, D_MLSTM), per_b),
            pl.BlockSpec((None, seq, D_MLSTM), per_b),
            pl.BlockSpec((None, seq, D_MLSTM), per_b),
            pl.BlockSpec((None, nrow, seq), per_b),
        ],
        out_shape=[
            jax.ShapeDtypeStruct((b, seq, D_MLSTM), F32),
            jax.ShapeDtypeStruct((b, seq, D_MLSTM), BF16),
            jax.ShapeDtypeStruct((b, seq, D_MLSTM), BF16),
            jax.ShapeDtypeStruct((b, seq, D_MLSTM), BF16),
            jax.ShapeDtypeStruct((b, nrow, seq), F32),
        ],
        compiler_params=_params(("parallel",)),
        name="mlstm_pre",
    )(xm, conv_w, conv_b, wq, wk, wv, wif_rows, bif_rows)


def _log_sigmoid(x):
    return jnp.minimum(x, 0.0) - jnp.log1p(jnp.exp(-jnp.abs(x)))


def _mlstm_kernel(q_ref, k_ref, v_ref, g_ref, xc_ref, z_ref, ng_ref, sk_ref, o_ref,
                  dc_ref, st_ref, cs_ref, ms_ref):
    L = MLSTM_CHUNK
    dh = MLSTM_HEAD_DIM
    nc = g_ref.shape[0]
    lane = lax.broadcasted_iota(jnp.int32, (1, L), 1)
    sub8 = lax.broadcasted_iota(jnp.int32, (8, L), 0)
    row_i =lax.broadcasted_iota(jnp.int32, (L, L), 0)
    col_i = lax.broadcasted_iota(jnp.int32, (L, L), 1)
    ones_col = jnp.where(lax.broadcasted_iota(jnp.int32, (L, dh), 1) == 0, 1.0, 0.0).astype(BF16)

    def gates(c):
        g = g_ref[c]
        lf = _log_sigmoid(g)
        pre, suf = lf, lf
        d = 1
        while d < L:
            pre = pre + jnp.where(lane >= d, pltpu.roll(pre, d, axis=1), 0.0)
            suf = suf + jnp.where(lane < L - d, pltpu.roll(suf, L - d, axis=1), 0.0)
            d *= 2
        return g[0:1], pre[1:2], pre[1:2, L - 1:L], g[2:3], suf[3:4], suf[3:4, 0:1]

    def v_aug(c):
        return jnp.concatenate([v_ref[pl.ds(c * L, L), :], ones_col], axis=1)

    def phase_a(c, carry):
        i_f, b_f, g_f, i_b, b_b, g_b = gates(c)
        kt = k_ref[pl.ds(c * L, L), :].astype(F32).T
        a_f = g_f - b_f + i_f
        a_b = g_b - b_b + i_b
        ml_f = jnp.max(a_f, axis=1, keepdims=True)
        ml_b = jnp.max(a_b, axis=1, keepdims=True)
        kw = jnp.concatenate([kt * jnp.exp(a_f - ml_f), kt * jnp.exp(a_b - ml_b)], axis=0)
        dc_ref[c] = _dot(kw.astype(BF16), v_aug(c))
        st = jnp.zeros((8, L), F32)
        for r, s in enumerate((ml_f, g_f, ml_b, g_b)):
            st = jnp.where(sub8 == r, s, st)
        st_ref[c] = st
        return carry

    lax.fori_loop(0, nc, phase_a, 0)

    def scan_dir(off_rows, off_cols, st_row, ms_row, reverse):
        def body(i, carry):
            c = (nc - 1 - i) if reverse else i
            state, m = carry
            cs_ref[c, :, off_cols:off_cols + 2 * dh] = state.astype(BF16)
            ms_ref[c, ms_row:ms_row + 1, :] = m
            st = st_ref[c]
            ml, g = st[st_row:st_row + 1], st[st_row + 1:st_row + 2]
            m_new = jnp.maximum(g + m, ml)
            alpha = jnp.exp(g + m - m_new)
            beta = jnp.exp(ml - m_new)
            alpha2 = jnp.concatenate([alpha, alpha], axis=1)
            beta2 = jnp.concatenate([beta, beta], axis=1)
            state = alpha2 * state + beta2 * dc_ref[c, off_rows:off_rows + dh, :]
            return state, m_new

        init = (jnp.zeros((dh, 2 * dh), F32), jnp.zeros((1, L), F32))
        lax.fori_loop(0, nc, body, init)

    scan_dir(0, 0, 0, 0, False)
    scan_dir(dh, 2 * dh, 2, 1, True)

    ng = ng_ref[...]
    sk = sk_ref[...]

    def to_col(r):
        return jnp.sum(jnp.where(row_i == col_i, jnp.broadcast_to(r, (L, L)), 0.0),
                       axis=1, keepdims=True)

    def direction(s_qk, qc, vaug, i_r, b_r, m_prev, keep):
        b_c = to_col(b_r)
        log_d = jnp.where(keep, b_c - b_r + i_r, NEG_INF)
        inter = b_c + m_prev
        m_t = jnp.maximum(inter, jnp.max(log_d, axis=1, keepdims=True))
        w = jnp.exp(log_d - m_t) * s_qk
        scale = jnp.exp(inter - m_t)
        intra = _dot(w.astype(BF16), vaug)
        tot = intra + scale * qc
        den = tot[:, dh:dh + 1]
        return tot[:, :dh] / jnp.maximum(jnp.abs(den), jnp.exp(-m_t))

    def phase_c(c, carry):
        i_f, b_f, _, i_b, b_b, _ = gates(c)
        rows = pl.ds(c * L, L)
        q = q_ref[rows, :]
        s_qk = _dot_nt(q, k_ref[rows, :])
        qc = _dot(q, cs_ref[c])
        vaug = v_aug(c)
        ms = ms_ref[c]
        h = (direction(s_qk, qc[:, :2 * dh], vaug, i_f, b_f, ms[0:1, 0:1], col_i <= row_i)
             + direction(s_qk, qc[:, 2 * dh:], vaug, i_b, b_b, ms[1:2, 0:1], col_i >= row_i))
        hn = _rms(h, ng)
        o_ref[rows, :] = (hn + sk * xc_ref[rows, :]) * _silu(z_ref[rows, :])
        return carry

    lax.fori_loop(0, nc, phase_c, 0)


def _mlstm(q, k, v, g_chunks, xc, z, norm_g, skip):
    b, seq, _ = q.shape
    nc = seq // MLSTM_CHUNK
    dh = MLSTM_HEAD_DIM
    head = lambda i, j: (i, 0, j)
    vec = lambda i, j: (0, j)
    blk = pl.BlockSpec((None, seq, dh), head)
    return pl.pallas_call(
        _mlstm_kernel,
        grid=(b, MLSTM_HEADS),
        in_specs=[
            blk, blk, blk,
            pl.BlockSpec((None, None, nc, 8, MLSTM_CHUNK), lambda i, j: (i, j, 0, 0, 0)),
            blk, blk,
            pl.BlockSpec((1, dh), vec),
            pl.BlockSpec((1, dh), vec),
        ],
        out_specs=blk,
        out_shape=jax.ShapeDtypeStruct((b, seq, D_MLSTM), F32),
        scratch_shapes=[
            pltpu.VMEM((nc, 2 * dh, 2 * dh), F32),
            pltpu.VMEM((nc, 8, MLSTM_CHUNK), F32),
            pltpu.VMEM((nc, dh, 4 * dh), BF16),
            pltpu.VMEM((nc, 8, MLSTM_CHUNK), F32),
        ],
        compiler_params=_params(("parallel", "parallel")),
        name="mlstm",
    )(q, k, v, g_chunks, xc, z, norm_g, skip)


def _dilated_multiplicity(seq):
    d = jnp.arange(seq)[:, None] - jnp.arange(seq)[None, :]
    c = jnp.zeros((seq, seq), F32)
    for win, dil in DILATED_PATTERNS:
        half = win // (2 * dil)
        c = c + ((d % dil == 0) & (jnp.abs(d) <= half * dil)).astype(F32)
    return c.astype(BF16)


def _dil_attn_kernel(q_ref, k_ref, v_ref, c_ref, o_ref):
    q = q_ref[...]
    k = k_ref[...]
    v = v_ref[...]
    c = c_ref[...].astype(F32)
    first = lax.broadcasted_iota(jnp.int32, (1, LANES), 1) < ATTN_HEAD_DIM
    zero = jnp.zeros_like(q)
    outs = []
    for sel in (first, jnp.logical_not(first)):
        s = _dot_nt(jnp.where(sel, q, zero), k)
        s = jnp.where(c > 0.0, s, NEG_INF)
        m = jnp.max(s, axis=1, keepdims=True)
        p = c * jnp.exp(s - m)
        den = jnp.sum(p, axis=1, keepdims=True)
        outs.append(_dot(p.astype(BF16), v) / den)
    o_ref[...] = jnp.where(first, outs[0], outs[1])


def _dil_attn(q, k, v, cmask):
    b, seq, _ = q.shape
    pairs = D_ATTN // LANES
    return pl.pallas_call(
        _dil_attn_kernel,
        grid=(b, pairs, seq // ATTN_Q_TILE),
        in_specs=[
            pl.BlockSpec((None, ATTN_Q_TILE, LANES), lambda i, j, t: (i, t, j)),
            pl.BlockSpec((None, seq, LANES), lambda i, j, t: (i, 0, j)),
            pl.BlockSpec((None, seq, LANES), lambda i, j, t: (i, 0, j)),
            pl.BlockSpec((ATTN_Q_TILE, seq), lambda i, j, t: (t, 0)),
        ],
        out_specs=pl.BlockSpec((None, ATTN_Q_TILE, LANES), lambda i, j, t: (i, t, j)),
        out_shape=jax.ShapeDtypeStruct((b, seq, D_ATTN), F32),
        compiler_params=_params(("parallel", "parallel", "parallel")),
        name="dil_attn",
    )(q, k, v, cmask)


def _out_proj_kernel(ym_ref, ya_ref, x_ref, ag_ref, w_ref, n2_ref, wr_ref,
                     x2_ref, h2_ref, lg_ref):
    ya = _rms(ya_ref[...], ag_ref[...])
    mixed = jnp.concatenate([ym_ref[...], ya], axis=1).astype(BF16)
    x2 = x_ref[...] + _dot(mixed, w_ref[...])
    x2_ref[...] = x2
    h2 = _rms(x2, n2_ref[...])
    h2_ref[...] = h2.astype(BF16)
    lg_ref[...] = jnp.dot(h2, wr_ref[...], preferred_element_type=F32,
                          precision=lax.Precision.HIGHEST)


def _out_proj(ym, ya, x2d, attn_g, w_bf, n2g, wr_pad):
    n = x2d.shape[0]
    row = lambda i: (i, 0)
    fixed = lambda i: (0, 0)
    return pl.pallas_call(
        _out_proj_kernel,
        grid=(n // ROW_TILE,),
        in_specs=[
            pl.BlockSpec((ROW_TILE, D_MLSTM), row),
            pl.BlockSpec((ROW_TILE, D_ATTN), row),
            pl.BlockSpec((ROW_TILE, D_MODEL), row),
            pl.BlockSpec((1, D_ATTN), fixed),
            pl.BlockSpec((D_MODEL, D_MODEL), fixed),
            pl.BlockSpec((1, D_MODEL), fixed),
            pl.BlockSpec((D_MODEL, LANES), fixed),
        ],
        out_specs=[
            pl.BlockSpec((ROW_TILE, D_MODEL), row),
            pl.BlockSpec((ROW_TILE, D_MODEL), row),
            pl.BlockSpec((ROW_TILE, LANES), row),
        ],
        out_shape=[
            jax.ShapeDtypeStruct((n, D_MODEL), F32),
            jax.ShapeDtypeStruct((n, D_MODEL), BF16),
            jax.ShapeDtypeStruct((n, LANES), F32),
        ],
        compiler_params=_params(("parallel",)),
        name="out_proj",
    )(ym, ya, x2d, attn_g, w_bf, n2g, wr_pad)


def _route_kernel(lg_ref, tri_ref, eye_ref, slot_ref, slot_t_ref, aff_ref, *, cap):
    lg = lg_ref[...]
    valid = lax.broadcasted_iota(jnp.int32, (1, LANES), 1) < N_EXPERTS
    lg = jnp.where(valid, lg, NEG_INF)
    e = jnp.exp(lg - jnp.max(lg, axis=1, keepdims=True))
    aff = e / jnp.sum(e, axis=1, keepdims=True)
    aff_ref[...] = aff
    def enough(cand):
        return jnp.sum(jnp.where(aff >= cand, 1.0, 0.0), axis=0, keepdims=True) >= cap

    tiny = jnp.full((1, LANES), 2.0 ** -126, F32)
    normal = enough(tiny)
    p = tiny
    for bit in range(6, -1, -1):
        cand = p * (2.0 ** (2 ** bit))
        p = jnp.where(enough(cand), cand, p)
    lo = jnp.where(normal, p, 0.0)
    hi = jnp.where(normal, p * 2.0, tiny)
    step = jnp.where(normal, p * 0.5, 0.0)
    for _ in range(THRESHOLD_MANTISSA_STEPS):
        cand = lo + step
        ok = enough(cand)
        lo = jnp.where(ok, cand, lo)
        hi = jnp.where(ok, hi, cand)
        step = step * 0.5
    gt = jnp.where(aff >= hi, 1.0, 0.0)
    eq = jnp.where(aff >= lo, 1.0, 0.0) - gt
    need = cap - jnp.sum(gt, axis=0, keepdims=True)
    tri = tri_ref[...]
    eq_before = _dot(tri, eq.astype(BF16))
    sel = gt + eq * jnp.where(eq_before < need, 1.0, 0.0)
    pos = _dot(tri, sel.astype(BF16))
    slot = jnp.where(valid & (sel > 0.0), pos, -1.0)
    slot_ref[...] = slot
    slot_t_ref[...] = _dot_nt(eye_ref[...], slot.astype(BF16))


def _route(logits, cap):
    b, seq, _ = logits.shape
    tri = (jnp.arange(seq)[None, :] < jnp.arange(seq)[:, None]).astype(BF16)
    eye = jnp.eye(LANES, dtype=BF16)
    per_b = lambda i: (i, 0, 0)
    fixed = lambda i: (0, 0)
    return pl.pallas_call(
        functools.partial(_route_kernel, cap=cap),
        grid=(b,),
        in_specs=[
            pl.BlockSpec((None, seq, LANES), per_b),
            pl.BlockSpec((seq, seq), fixed),
            pl.BlockSpec((LANES, LANES), fixed),
        ],
        out_specs=[
            pl.BlockSpec((None, seq, LANES), per_b),
            pl.BlockSpec((None, LANES, seq), per_b),
            pl.BlockSpec((None, seq, LANES), per_b),
        ],
        out_shape=[
            jax.ShapeDtypeStruct((b, seq, LANES), F32),
            jax.ShapeDtypeStruct((b, LANES, seq), F32),
            jax.ShapeDtypeStruct((b, seq, LANES), F32),
        ],
        compiler_params=_params(("parallel",)),
        name="route",
    )(logits, tri, eye)


def _moe_gather_kernel(slot_ref, h_ref, xs_ref):
    srow = slot_ref[...]
    cap, seq = xs_ref.shape[0], srow.shape[1]
    ci = lax.broadcasted_iota(jnp.int32, (cap, seq), 0).astype(F32)
    onehot = jnp.where(srow == ci, 1.0, 0.0).astype(BF16)
    xs_ref[...] = _dot(onehot, h_ref[...]).astype(BF16)


def _moe_gather(slot_t, h2, cap):
    b, seq, _ = h2.shape
    return pl.pallas_call(
        _moe_gather_kernel,
        grid=(b, N_EXPERTS),
        in_specs=[
            pl.BlockSpec((None, None, 1, seq), lambda i, e: (i, e, 0, 0)),
            pl.BlockSpec((None, seq, D_MODEL), lambda i, e: (i, 0, 0)),
        ],
        out_specs=pl.BlockSpec((None, None, cap, D_MODEL), lambda i, e: (i, e, 0, 0)),
        out_shape=jax.ShapeDtypeStruct((b, N_EXPERTS, cap, D_MODEL), BF16),
        compiler_params=_params(("parallel", "parallel")),
        name="moe_gather",
    )(slot_t, h2)


def _moe_ffn_kernel(xs_ref, w1_ref, w3_ref, w2_ref, y_ref, acc_ref, w1b_ref, w3b_ref, w2b_ref):
    f = pl.program_id(1)
    nb, cap, _ = xs_ref.shape
    w1b_ref[...] = w1_ref[...].astype(BF16)
    w3b_ref[...] = w3_ref[...].astype(BF16)
    w2b_ref[...] = w2_ref[...].astype(BF16)
    per = FFN_ROW_TILE // cap
    for r in range(nb // per):
        x = xs_ref[r * per:(r + 1) * per].reshape(FFN_ROW_TILE, D_MODEL)
        up = _dot(x, w1b_ref[...])
        gt = _dot(x, w3b_ref[...])
        act = (_silu(up) * gt).astype(BF16)
        part = _dot(act, w2b_ref[...])
        rows = slice(r * FFN_ROW_TILE, (r + 1) * FFN_ROW_TILE)

        @pl.when(f == 0)
        def _():
            acc_ref[rows, :] = part

        @pl.when(f > 0)
        def _():
            acc_ref[rows, :] += part

    @pl.when(f == pl.num_programs(1) - 1)
    def _():
        y_ref[...] = acc_ref[...].astype(BF16).reshape(nb, cap, D_MODEL)


def _moe_ffn(xs, w1, w3, w2):
    b, ne, cap, _ = xs.shape
    nf = D_EXPERT // FFN_F_TILE
    tok = pl.BlockSpec((b, None, cap, D_MODEL), lambda e, f: (0, e, 0, 0))
    return pl.pallas_call(
        _moe_ffn_kernel,
        grid=(ne, nf),
        in_specs=[
            tok,
            pl.BlockSpec((None, D_MODEL, FFN_F_TILE), lambda e, f: (e, 0, f)),
            pl.BlockSpec((None, D_MODEL, FFN_F_TILE), lambda e, f: (e, 0, f)),
            pl.BlockSpec((None, FFN_F_TILE, D_MODEL), lambda e, f: (e, f, 0)),
        ],
        out_specs=tok,
        out_shape=jax.ShapeDtypeStruct(xs.shape, BF16),
        scratch_shapes=[
            pltpu.VMEM((b * cap, D_MODEL), F32),
            pltpu.VMEM((D_MODEL, FFN_F_TILE), BF16),
            pltpu.VMEM((D_MODEL, FFN_F_TILE), BF16),
            pltpu.VMEM((FFN_F_TILE, D_MODEL), BF16),
        ],
        compiler_params=_params(("parallel", "arbitrary")),
        name="moe_ffn",
    )(xs, w1, w3, w2)


def _moe_scatter_kernel(slot_ref, aff_ref, y_ref, x2_ref, g_ref, o_ref):
    slot = slot_ref[...]
    aff = aff_ref[...]
    rows, cap = slot.shape[0], y_ref.shape[1]
    ci = lax.broadcasted_iota(jnp.int32, (rows, cap), 1).astype(F32)
    acc = x2_ref[...]
    for e in range(N_EXPERTS):
        onehot = jnp.where(slot[:, e:e + 1] == ci, 1.0, 0.0).astype(BF16)
        acc = acc + aff[:, e:e + 1] * _dot(onehot, y_ref[e])
    o_ref[...] = _rms(acc, g_ref[...])


def _moe_scatter(slot, aff, y, x2, norm_g):
    b, seq, _ = x2.shape
    cap = y.shape[2]
    tile = lambda i, r: (i, r, 0)
    return pl.pallas_call(
        _moe_scatter_kernel,
        grid=(b, seq // ROW_TILE),
        in_specs=[
            pl.BlockSpec((None, ROW_TILE, LANES), tile),
            pl.BlockSpec((None, ROW_TILE, LANES), tile),
            pl.BlockSpec((None, N_EXPERTS, cap, D_MODEL), lambda i, r: (i, 0, 0, 0)),
            pl.BlockSpec((None, ROW_TILE, D_MODEL), tile),
            pl.BlockSpec((1, D_MODEL), lambda i, r: (0, 0)),
        ],
        out_specs=pl.BlockSpec((None, ROW_TILE, D_MODEL), tile),
        out_shape=jax.ShapeDtypeStruct((b, seq, D_MODEL), F32),
        compiler_params=_params(("parallel", "parallel")),
        name="moe_scatter",
    )(slot, aff, y, x2, norm_g)


def kernel(x, norm1_g, w_in, conv_w, conv_b, wq_m, wk_m, wv_m, w_if_fwd, b_if_fwd,
           w_if_bwd, b_if_bwd, mlstm_norm_g, mlstm_skip, attn_norm_g, w_out, norm2_g,
           w_router, w1, w3, w2, norm_f_g):
    b, seq, _ = x.shape
    assert w_in.shape[0] == 1, "single-layer problem"
    assert seq % ROW_TILE == 0 and seq % MLSTM_CHUNK == 0 and seq % ATTN_Q_TILE == 0
    cap = EC_CAPACITY * seq // N_EXPERTS
    assert FFN_ROW_TILE % cap == 0 and (b * cap) % FFN_ROW_TILE == 0
    nc = seq // MLSTM_CHUNK
    l = 0
    x2d = x.reshape(b * seq, D_MODEL)
    xm, z, qa, ka, va = _in_proj(x2d, norm1_g[l][None, :], w_in[l].astype(BF16), seq)
    shp = lambda t: t.reshape(b, seq, t.shape[-1])
    wif_rows, bif_rows = _gate_rows(w_if_fwd[l], b_if_fwd[l], w_if_bwd[l], b_if_bwd[l])
    xc, qm, km, vm, gates = _mlstm_pre(
        shp(xm), conv_w[l], conv_b[l][None, :],
        _block_diag(wq_m[l]).astype(BF16), _block_diag(wk_m[l]).astype(BF16),
        _block_diag(wv_m[l]).astype(BF16), wif_rows.astype(BF16), bif_rows)
    g_chunks = gates.reshape(b, MLSTM_HEADS, 8, nc, MLSTM_CHUNK).transpose(0, 1, 3, 2, 4)
    ym = _mlstm(qm, km, vm, g_chunks, xc, shp(z), mlstm_norm_g[l][None, :],
                mlstm_skip[l][None, :])
    ya = _dil_attn(shp(qa), shp(ka), shp(va), _dilated_multiplicity(seq))
    wr_pad = jnp.pad(w_router[l], ((0, 0), (0, LANES - N_EXPERTS)))
    x2, h2, logits = _out_proj(
        ym.reshape(b * seq, D_MLSTM), ya.reshape(b * seq, D_ATTN), x2d,
        attn_norm_g[l][None, :], w_out[l].astype(BF16), norm2_g[l][None, :], wr_pad)
    slot, slot_t, aff = _route(logits.reshape(b, seq, LANES), cap)
    xs = _moe_gather(slot_t.reshape(b, LANES, 1, seq), h2.reshape(b, seq, D_MODEL), cap)
    y = _moe_ffn(xs, w1[l], w3[l], w2[l])
    return _moe_scatter(slot, aff, y, x2.reshape(b, seq, D_MODEL), norm_f_g[None, :])
```

```python
import functools

import jax
import jax.numpy as jnp
from jax import lax
from jax.experimental import pallas as pl
from jax.experimental.pallas import tpu as pltpu

F32 = jnp.float32
BF16 = jnp.bfloat16

D_MODEL = 1024
D_MLSTM = 512
D_ATTN = 512
D_IN_PROJ = 2 * D_MLSTM + 3 * D_ATTN
MLSTM_HEADS = 4
MLSTM_HEAD_DIM = 128
MLSTM_QKV_BLOCK = 4
MLSTM_CONV = 5
ATTN_HEADS = 8
ATTN_HEAD_DIM = 64
ROPE_DIM = 16
ROPE_THETA = 500000.0
DILATED_PATTERNS = ((128, 1), (512, 4), (2048, 16))
N_EXPERTS = 16
EC_CAPACITY = 2
D_EXPERT = 2816
NORM_EPS = 1e-6
NEG_INF = -1e30

LANES = 128
MLSTM_CHUNK = 128
ROW_TILE = 512
ATTN_Q_TILE = 512
ROUTE_BLOCK = 256
FFN_F_TILE = 256
FFN_ROW_TILE = 512
THRESHOLD_MANTISSA_STEPS = 40
VMEM_LIMIT = 56 * 1024 * 1024


def _params(sem):
    return pltpu.CompilerParams(dimension_semantics=sem, vmem_limit_bytes=VMEM_LIMIT)


def _rms(x, g):
    return x * lax.rsqrt(jnp.mean(x * x, axis=-1, keepdims=True) + NORM_EPS) * g


def _silu(x):
    return x * (1.0 / (1.0 + jnp.exp(-x)))


def _dot(a, b):
    return jnp.dot(a, b, preferred_element_type=F32)


def _dot_nt(a, b):
    return lax.dot_general(a, b, (((1,), (1,)), ((), ())), preferred_element_type=F32)


def _in_proj_kernel(x_ref, g_ref, w_ref, cos_ref, sa_ref, sb_ref,
                    xm_ref, z_ref, q_ref, k_ref, v_ref):
    h = _rms(x_ref[...], g_ref[...])
    p = _dot(h.astype(BF16), w_ref[...])
    xm_ref[...] = p[:, :D_MLSTM]
    z_ref[...] = p[:, D_MLSTM:2 * D_MLSTM]
    cos, sa, sb = cos_ref[...], sa_ref[...], sb_ref[...]
    half = ROPE_DIM // 2

    def rope(t):
        outs = []
        for j in range(D_ATTN // LANES):
            tj = t[:, j * LANES:(j + 1) * LANES]
            up = pltpu.roll(tj, LANES - half, axis=1)
            dn = pltpu.roll(tj, half, axis=1)
            outs.append(tj * cos + up * sa + dn * sb)
        return jnp.concatenate(outs, axis=1)

    o = 2 * D_MLSTM
    q_ref[...] = (rope(p[:, o:o + D_ATTN]) * (ATTN_HEAD_DIM ** -0.5)).astype(BF16)
    k_ref[...] = rope(p[:, o + D_ATTN:o + 2 * D_ATTN]).astype(BF16)
    v_ref[...] = p[:, o + 2 * D_ATTN:].astype(BF16)


def _rope_tables(seq):
    half = ROPE_DIM // 2
    inv_freq = ROPE_THETA ** (-2.0 * jnp.arange(half, dtype=F32) / ROPE_DIM)
    ang = jnp.arange(seq).astype(F32)[:, None] * inv_freq[None, :]
    cos, sin = jnp.cos(ang), jnp.sin(ang)
    pad = jnp.zeros((seq, ATTN_HEAD_DIM - ROPE_DIM), F32)
    cos_h = jnp.concatenate([cos, cos, pad + 1.0], axis=1)
    sa_h = jnp.concatenate([-sin, jnp.zeros_like(sin), pad], axis=1)
    sb_h = jnp.concatenate([jnp.zeros_like(sin), sin, pad], axis=1)
    rep = LANES // ATTN_HEAD_DIM
    return tuple(jnp.tile(t, (1, rep)) for t in (cos_h, sa_h, sb_h))


def _in_proj(x2d, g, w_bf, seq):
    n = x2d.shape[0]
    tiles_per_seq = seq // ROW_TILE
    cos, sa, sb = _rope_tables(seq)
    row = lambda i: (i, 0)
    fixed = lambda i: (0, 0)
    pos = lambda i: (i % tiles_per_seq, 0)
    return pl.pallas_call(
        _in_proj_kernel,
        grid=(n // ROW_TILE,),
        in_specs=[
            pl.BlockSpec((ROW_TILE, D_MODEL), row),
            pl.BlockSpec((1, D_MODEL), fixed),
            pl.BlockSpec((D_MODEL, D_IN_PROJ), fixed),
            pl.BlockSpec((ROW_TILE, LANES), pos),
            pl.BlockSpec((ROW_TILE, LANES), pos),
            pl.BlockSpec((ROW_TILE, LANES), pos),
        ],
        out_specs=[
            pl.BlockSpec((ROW_TILE, D_MLSTM), row),
            pl.BlockSpec((ROW_TILE, D_MLSTM), row),
            pl.BlockSpec((ROW_TILE, D_ATTN), row),
            pl.BlockSpec((ROW_TILE, D_ATTN), row),
            pl.BlockSpec((ROW_TILE, D_ATTN), row),
        ],
        out_shape=[
            jax.ShapeDtypeStruct((n, D_MLSTM), F32),
            jax.ShapeDtypeStruct((n, D_MLSTM), F32),
            jax.ShapeDtypeStruct((n, D_ATTN), BF16),
            jax.ShapeDtypeStruct((n, D_ATTN), BF16),
            jax.ShapeDtypeStruct((n, D_ATTN), BF16),
        ],
        compiler_params=_params(("parallel",)),
        name="in_proj",
    )(x2d, g, w_bf, cos, sa, sb)


def _mlstm_pre_kernel(xm_ref, cw_ref, cb_ref, wq_ref, wkt_ref, wv_ref, wif_ref, bif_ref,
                      xc_ref, q_ref, kt_ref, v_ref, g_ref):
    x = xm_ref[...]
    seq = x.shape[0]
    t = lax.broadcasted_iota(jnp.int32, x.shape, 0)
    acc = jnp.zeros_like(x) + cb_ref[...]
    for j in range(MLSTM_CONV):
        d = j - MLSTM_CONV // 2
        if d == 0:
            tap = x
        else:
            tap = pltpu.roll(x, (-d) % seq, axis=0)
            tap = jnp.where((t + d >= 0) & (t + d < seq), tap, 0.0)
        acc = acc + tap * cw_ref[j:j + 1, :]
    xc = _silu(acc)
    xc_ref[...] = xc
    xcb = xc.astype(BF16)
    q = _dot(xcb, wq_ref[...]).astype(BF16)
    kt = (_dot_nt(wkt_ref[...], xcb) * (MLSTM_HEAD_DIM ** -0.5)).astype(BF16)
    v = _dot(x.astype(BF16), wv_ref[...]).astype(BF16)
    q_ref[...] = q
    v_ref[...] = v
    L = MLSTM_CHUNK
    for c in range(seq // L):
        kt_ref[c] = kt[:, c * L:(c + 1) * L]
    wif = wif_ref[...]
    g_ref[...] = (_dot_nt(wif[:, :D_MLSTM], q) + _dot(wif[:, D_MLSTM:2 * D_MLSTM], kt)
                  + _dot_nt(wif[:, 2 * D_MLSTM:], v) + bif_ref[...])


def _block_diag(w):
    nblk = w.shape[0]
    n = nblk * MLSTM_QKV_BLOCK
    tiled = jnp.tile(w.reshape(n, MLSTM_QKV_BLOCK), (1, nblk))
    blk = jnp.arange(n) // MLSTM_QKV_BLOCK
    return jnp.where(blk[:, None] == blk[None, :], tiled, 0.0)


def _gate_rows(w_f, b_f, w_b, b_b):
    h = MLSTM_HEADS
    cols = []
    bias = []
    zero_w = jnp.zeros((w_f.shape[0],), F32)
    for off in (h, 0):
        for hd in range(h):
            cols += [w_f[:, off + hd], w_b[:, off + hd]] + [zero_w] * 6
            bias += [b_f[off + hd], b_b[off + hd]] + [jnp.zeros((), F32)] * 6
    return jnp.stack(cols, axis=0), jnp.stack(bias)[:, None]


def _mlstm_pre(xm, conv_w, conv_b, wq, wkt, wv, wif_rows, bif_rows):
    b, seq, _ = xm.shape
    nrow = wif_rows.shape[0]
    nc = seq // MLSTM_CHUNK
    per_b = lambda i: (i, 0, 0)
    fixed = lambda i: (0, 0)
    return pl.pallas_call(
        _mlstm_pre_kernel,
        grid=(b,),
        in_specs=[
            pl.BlockSpec((None, seq, D_MLSTM), per_b),
            pl.BlockSpec((MLSTM_CONV, D_MLSTM), fixed),
            pl.BlockSpec((1, D_MLSTM), fixed),
            pl.BlockSpec((D_MLSTM, D_MLSTM), fixed),
            pl.BlockSpec((D_MLSTM, D_MLSTM), fixed),
            pl.BlockSpec((D_MLSTM, D_MLSTM), fixed),
            pl.BlockSpec((nrow, 3 * D_MLSTM), fixed),
            pl.BlockSpec((nrow, 1), fixed),
        ],
        out_specs=[
            pl.BlockSpec((None, seq, D_MLSTM), per_b),
            pl.BlockSpec((None, seq, D_MLSTM), per_b),
            pl.BlockSpec((None, nc, D_MLSTM, MLSTM_CHUNK), lambda i: (i, 0, 0, 0)),
            pl.BlockSpec((None, seq, D_MLSTM), per_b),
            pl.BlockSpec((None, nrow, seq), per_b),
        ],
        out_shape=[
            jax.ShapeDtypeStruct((b, seq, D_MLSTM), F32),
            jax.ShapeDtypeStruct((b, seq, D_MLSTM), BF16),
            jax.ShapeDtypeStruct((b, nc, D_MLSTM, MLSTM_CHUNK), BF16),
            jax.ShapeDtypeStruct((b, seq, D_MLSTM), BF16),
            jax.ShapeDtypeStruct((b, nrow, seq), F32),
        ],
        compiler_params=_params(("parallel",)),
        name="mlstm_pre",
    )(xm, conv_w, conv_b, wq, wkt, wv, wif_rows, bif_rows)


def _log_sigmoid(x):
    return jnp.minimum(x, 0.0) - jnp.log1p(jnp.exp(-jnp.abs(x)))


def _mlstm_kernel(q_ref, kt_ref, v_ref, g_ref, xc_ref, z_ref, ng_ref, sk_ref, o_ref,
                  cum_ref, w_ref, ml_ref, tot_ref, dc_ref, cs_ref, ms_ref):
    L = MLSTM_CHUNK
    dh = MLSTM_HEAD_DIM
    nc = kt_ref.shape[0]
    rows_all = g_ref.shape[1]
    lane = lax.broadcasted_iota(jnp.int32, (rows_all, L), 1)
    fwd_row = lax.broadcasted_iota(jnp.int32, (rows_all, L), 0) % 8 == 0
    row_i = lax.broadcasted_iota(jnp.int32, (L, L), 0)
    col_i = lax.broadcasted_iota(jnp.int32, (L, L), 1)
    ones_col = jnp.where(lax.broadcasted_iota(jnp.int32, (L, dh), 1) == 0, 1.0, 0.0).astype(BF16)

    lf = _log_sigmoid(g_ref[0])
    pre, suf = lf, lf
    d = 1
    while d < L:
        pre = pre + jnp.where(lane >= d, pltpu.roll(pre, d, axis=1), 0.0)
        suf = suf + jnp.where(lane < L - d, pltpu.roll(suf, L - d, axis=1), 0.0)
        d *= 2
    cum = jnp.where(fwd_row, pre, suf)
    tot = jnp.where(fwd_row, cum[:, L - 1:L], cum[:, 0:1])
    a = tot - cum + g_ref[1]
    ml = jnp.max(a, axis=1, keepdims=True)
    cum_ref[...] = cum
    w_ref[...] = jnp.exp(a - ml)
    ml_ref[...] = jnp.broadcast_to(ml, (rows_all, L))
    tot_ref[...] = tot

    def chunk_rows(ref, c):
        return ref[pl.ds(pl.multiple_of(c * 8, 8), 8), :]

    def v_aug(c):
        return jnp.concatenate([v_ref[pl.ds(c * L, L), :], ones_col], axis=1)

    def phase_a(c, carry):
        kt = kt_ref[c].astype(F32)
        w = chunk_rows(w_ref, c)
        kw = jnp.concatenate([kt * w[0:1], kt * w[1:2]], axis=0)
        dc_ref[c] = _dot(kw.astype(BF16), v_aug(c))
        return carry

    lax.fori_loop(0, nc, phase_a, 0, unroll=2)

    def scan_dir(direction_row, reverse):
        off_rows = direction_row * dh
        off_cols = direction_row * 2 * dh

        def body(i, carry):
            c = (nc - 1 - i) if reverse else i
            state, m = carry
            cs_ref[c, :, off_cols:off_cols + 2 * dh] = state.astype(BF16)
            ms_ref[c, direction_row:direction_row + 1, :] = m
            ml_c = chunk_rows(ml_ref, c)[direction_row:direction_row + 1]
            g_c = chunk_rows(tot_ref, c)[direction_row:direction_row + 1]
            m_new = jnp.maximum(g_c + m, ml_c)
            alpha = jnp.exp(g_c + m - m_new)
            beta = jnp.exp(ml_c - m_new)
            alpha2 = jnp.concatenate([alpha, alpha], axis=1)
            beta2 = jnp.concatenate([beta, beta], axis=1)
            state = alpha2 * state + beta2 * dc_ref[c, off_rows:off_rows + dh, :]
            return state, m_new

        init = (jnp.zeros((dh, 2 * dh), F32), jnp.zeros((1, L), F32))
        lax.fori_loop(0, nc, body, init)

    scan_dir(0, False)
    scan_dir(1, True)

    ng = ng_ref[...]
    sk = sk_ref[...]

    def to_col(r):
        return jnp.sum(jnp.where(row_i == col_i, jnp.broadcast_to(r, (L, L)), 0.0),
                       axis=1, keepdims=True)

    def direction(s_qk, qc, vaug, i_r, b_r, m_prev, keep):
        b_c = to_col(b_r)
        log_d = jnp.where(keep, b_c - b_r + i_r, NEG_INF)
        inter = b_c + m_prev
        m_t = jnp.maximum(inter, jnp.max(log_d, axis=1, keepdims=True))
        w = jnp.exp(log_d - m_t) * s_qk
        scale = jnp.exp(inter - m_t)
        intra = _dot(w.astype(BF16), vaug)
        tot_c = intra + scale * qc
        den = tot_c[:, dh:dh + 1]
        return tot_c[:, :dh] / jnp.maximum(jnp.abs(den), jnp.exp(-m_t))

    def phase_c(c, carry):
        rows = pl.ds(c * L, L)
        q = q_ref[rows, :]
        s_qk = _dot(q, kt_ref[c])
        qc = _dot(q, cs_ref[c])
        vaug = v_aug(c)
        ms = ms_ref[c]
        b = chunk_rows(cum_ref, c)
        gi = chunk_rows(g_ref.at[1], c)
        h = (direction(s_qk, qc[:, :2 * dh], vaug, gi[0:1], b[0:1], ms[0:1, 0:1], col_i <= row_i)
             + direction(s_qk, qc[:, 2 * dh:], vaug, gi[1:2], b[1:2], ms[1:2, 0:1], col_i >= row_i))
        hn = _rms(h, ng)
        o_ref[rows, :] = (hn + sk * xc_ref[rows, :]) * _silu(z_ref[rows, :])
        return carry

    lax.fori_loop(0, nc, phase_c, 0, unroll=2)


def _mlstm(q, kt, v, gates, xc, z, norm_g, skip):
    b, seq, _ = q.shape
    nc = seq // MLSTM_CHUNK
    dh = MLSTM_HEAD_DIM
    head = lambda i, j: (i, 0, j)
    vec = lambda i, j: (0, j)
    blk = pl.BlockSpec((None, seq, dh), head)
    gate_rows = pltpu.VMEM((nc * 8, MLSTM_CHUNK), F32)
    return pl.pallas_call(
        _mlstm_kernel,
        grid=(b, MLSTM_HEADS),
        in_specs=[
            blk,
            pl.BlockSpec((None, nc, dh, MLSTM_CHUNK), lambda i, j: (i, 0, j, 0)),
            blk,
            pl.BlockSpec((None, None, 2, nc * 8, MLSTM_CHUNK), lambda i, j: (i, j, 0, 0, 0)),
            blk, blk,
            pl.BlockSpec((1, dh), vec),
            pl.BlockSpec((1, dh), vec),
        ],
        out_specs=blk,
        out_shape=jax.ShapeDtypeStruct((b, seq, D_MLSTM), F32),
        scratch_shapes=[
            gate_rows, gate_rows, gate_rows, gate_rows,
            pltpu.VMEM((nc, 2 * dh, 2 * dh), F32),
            pltpu.VMEM((nc, dh, 4 * dh), BF16),
            pltpu.VMEM((nc, 8, MLSTM_CHUNK), F32),
        ],
        compiler_params=_params(("parallel", "parallel")),
        name="mlstm",
    )(q, kt, v, gates, xc, z, norm_g, skip)


def _dilated_log_multiplicity(seq):
    d = jnp.arange(seq)[:, None] - jnp.arange(seq)[None, :]
    c = jnp.zeros((seq, seq), F32)
    for win, dil in DILATED_PATTERNS:
        half = win // (2 * dil)
        c = c + ((d % dil == 0) & (jnp.abs(d) <= half * dil)).astype(F32)
    return jnp.where(c > 0.0, jnp.log(jnp.maximum(c, 1.0)), NEG_INF)


def _dil_attn_kernel(q_ref, k_ref, v_ref, bias_ref, o_ref):
    q = q_ref[...]
    k = k_ref[...]
    v = v_ref[...]
    first = lax.broadcasted_iota(jnp.int32, (1, LANES), 1) < ATTN_HEAD_DIM
    zero = jnp.zeros_like(q)
    outs = []
    for sel in (first, jnp.logical_not(first)):
        s = _dot_nt(jnp.where(sel, q, zero), k) + bias_ref[...]
        m = jnp.max(s, axis=1, keepdims=True)
        p = jnp.exp(s - m)
        den = jnp.sum(p, axis=1, keepdims=True)
        outs.append(_dot(p.astype(BF16), v) / den)
    o_ref[...] = jnp.where(first, outs[0], outs[1])


def _dil_attn(q, k, v, bias):
    b, seq, _ = q.shape
    pairs = D_ATTN // LANES
    return pl.pallas_call(
        _dil_attn_kernel,
        grid=(seq // ATTN_Q_TILE, b, pairs),
        in_specs=[
            pl.BlockSpec((None, ATTN_Q_TILE, LANES), lambda t, i, j: (i, t, j)),
            pl.BlockSpec((None, seq, LANES), lambda t, i, j: (i, 0, j)),
            pl.BlockSpec((None, seq, LANES), lambda t, i, j: (i, 0, j)),
            pl.BlockSpec((ATTN_Q_TILE, seq), lambda t, i, j: (t, 0)),
        ],
        out_specs=pl.BlockSpec((None, ATTN_Q_TILE, LANES), lambda t, i, j: (i, t, j)),
        out_shape=jax.ShapeDtypeStruct((b, seq, D_ATTN), F32),
        compiler_params=_params(("parallel", "parallel", "parallel")),
        name="dil_attn",
    )(q, k, v, bias)


def _out_proj_kernel(ym_ref, ya_ref, x_ref, ag_ref, w_ref, n2_ref, wr_ref,
                     x2_ref, h2_ref, lg_ref):
    ya = _rms(ya_ref[...], ag_ref[...])
    mixed = jnp.concatenate([ym_ref[...], ya], axis=1).astype(BF16)
    x2 = x_ref[...] + _dot(mixed, w_ref[...])
    x2_ref[...] = x2
    h2 = _rms(x2, n2_ref[...])
    h2_ref[...] = h2.astype(BF16)
    lg_ref[...] = jnp.dot(h2, wr_ref[...], preferred_element_type=F32,
                          precision=lax.Precision.HIGHEST)


def _out_proj(ym, ya, x2d, attn_g, w_bf, n2g, wr_pad):
    n = x2d.shape[0]
    row = lambda i: (i, 0)
    fixed = lambda i: (0, 0)
    return pl.pallas_call(
        _out_proj_kernel,
        grid=(n // ROW_TILE,),
        in_specs=[
            pl.BlockSpec((ROW_TILE, D_MLSTM), row),
            pl.BlockSpec((ROW_TILE, D_ATTN), row),
            pl.BlockSpec((ROW_TILE, D_MODEL), row),
            pl.BlockSpec((1, D_ATTN), fixed),
            pl.BlockSpec((D_MODEL, D_MODEL), fixed),
            pl.BlockSpec((1, D_MODEL), fixed),
            pl.BlockSpec((D_MODEL, LANES), fixed),
        ],
        out_specs=[
            pl.BlockSpec((ROW_TILE, D_MODEL), row),
            pl.BlockSpec((ROW_TILE, D_MODEL), row),
            pl.BlockSpec((ROW_TILE, LANES), row),
        ],
        out_shape=[
            jax.ShapeDtypeStruct((n, D_MODEL), F32),
            jax.ShapeDtypeStruct((n, D_MODEL), BF16),
            jax.ShapeDtypeStruct((n, LANES), F32),
        ],
        compiler_params=_params(("parallel",)),
        name="out_proj",
    )(ym, ya, x2d, attn_g, w_bf, n2g, wr_pad)


def _route_kernel(lg_ref, tri_ref, eye_ref, slot_ref, slot_t_ref, aff_ref, *, cap):
    lg = lg_ref[...]
    valid = lax.broadcasted_iota(jnp.int32, (1, LANES), 1) < N_EXPERTS
    lg = jnp.where(valid, lg, NEG_INF)
    e = jnp.exp(lg - jnp.max(lg, axis=1, keepdims=True))
    aff = e / jnp.sum(e, axis=1, keepdims=True)
    aff_ref[...] = aff
    def enough(cand):
        return jnp.sum(jnp.where(aff >= cand, 1.0, 0.0), axis=0, keepdims=True) >= cap

    tiny = jnp.full((1, LANES), 2.0 ** -126, F32)
    normal = enough(tiny)
    p = tiny
    for bit in range(6, -1, -1):
        cand = p * (2.0 ** (2 ** bit))
        p = jnp.where(enough(cand), cand, p)
    lo = jnp.where(normal, p, 0.0)
    hi = jnp.where(normal, p * 2.0, tiny)
    step = jnp.where(normal, p * 0.5, 0.0)
    for _ in range(THRESHOLD_MANTISSA_STEPS):
        cand = lo + step
        ok = enough(cand)
        lo = jnp.where(ok, cand, lo)
        hi = jnp.where(ok, hi, cand)
        step = step * 0.5
    gt = jnp.where(aff >= hi, 1.0, 0.0)
    eq = jnp.where(aff >= lo, 1.0, 0.0) - gt
    need = cap - jnp.sum(gt, axis=0, keepdims=True)
    tri = tri_ref[...]

    def count_before(x):
        blk = tri.shape[0]
        run = jnp.zeros((1, LANES), F32)
        outs = []
        for j in range(x.shape[0] // blk):
            xb = x[j * blk:(j + 1) * blk]
            outs.append(_dot(tri, xb.astype(BF16)) + run)
            run = run + jnp.sum(xb, axis=0, keepdims=True)
        return jnp.concatenate(outs, axis=0)

    sel = gt + eq * jnp.where(count_before(eq) < need, 1.0, 0.0)
    pos = count_before(sel)
    slot = jnp.where(valid & (sel > 0.0), pos, -1.0)
    slot_ref[...] = slot
    slot_t_ref[...] = _dot_nt(eye_ref[...], slot.astype(BF16))


def _route(logits, cap):
    b, seq, _ = logits.shape
    tri = (jnp.arange(ROUTE_BLOCK)[None, :] < jnp.arange(ROUTE_BLOCK)[:, None]).astype(BF16)
    eye = jnp.eye(LANES, dtype=BF16)
    assert seq % ROUTE_BLOCK == 0
    per_b = lambda i: (i, 0, 0)
    fixed = lambda i: (0, 0)
    return pl.pallas_call(
        functools.partial(_route_kernel, cap=cap),
        grid=(b,),
        in_specs=[
            pl.BlockSpec((None, seq, LANES), per_b),
            pl.BlockSpec((ROUTE_BLOCK, ROUTE_BLOCK), fixed),
            pl.BlockSpec((LANES, LANES), fixed),
        ],
        out_specs=[
            pl.BlockSpec((None, seq, LANES), per_b),
            pl.BlockSpec((None, LANES, seq), per_b),
            pl.BlockSpec((None, seq, LANES), per_b),
        ],
        out_shape=[
            jax.ShapeDtypeStruct((b, seq, LANES), F32),
            jax.ShapeDtypeStruct((b, LANES, seq), F32),
            jax.ShapeDtypeStruct((b, seq, LANES), F32),
        ],
        compiler_params=_params(("parallel",)),
        name="route",
    )(logits, tri, eye)


def _moe_gather_kernel(slot_ref, h_ref, xs_ref):
    srow = slot_ref[...]
    cap, seq = xs_ref.shape[0], srow.shape[1]
    ci = lax.broadcasted_iota(jnp.int32, (cap, seq), 0).astype(F32)
    onehot = jnp.where(srow == ci, 1.0, 0.0).astype(BF16)
    xs_ref[...] = _dot(onehot, h_ref[...]).astype(BF16)


def _moe_gather(slot_t, h2, cap):
    b, seq, _ = h2.shape
    return pl.pallas_call(
        _moe_gather_kernel,
        grid=(b, N_EXPERTS),
        in_specs=[
            pl.BlockSpec((None, None, 1, seq), lambda i, e: (i, e, 0, 0)),
            pl.BlockSpec((None, seq, D_MODEL), lambda i, e: (i, 0, 0)),
        ],
        out_specs=pl.BlockSpec((None, None, cap, D_MODEL), lambda i, e: (i, e, 0, 0)),
        out_shape=jax.ShapeDtypeStruct((b, N_EXPERTS, cap, D_MODEL), BF16),
        compiler_params=_params(("parallel", "parallel")),
        name="moe_gather",
    )(slot_t, h2)


def _moe_ffn_kernel(xs_ref, w1_ref, w3_ref, w2_ref, y_ref, acc_ref, w1b_ref, w3b_ref, w2b_ref):
    f = pl.program_id(1)
    nb, cap, _ = xs_ref.shape
    w1b_ref[...] = w1_ref[...].astype(BF16)
    w3b_ref[...] = w3_ref[...].astype(BF16)
    w2b_ref[...] = w2_ref[...].astype(BF16)
    per = FFN_ROW_TILE // cap
    for r in range(nb // per):
        x = xs_ref[r * per:(r + 1) * per].reshape(FFN_ROW_TILE, D_MODEL)
        up = _dot(x, w1b_ref[...])
        gt = _dot(x, w3b_ref[...])
        act = (_silu(up) * gt).astype(BF16)
        part = _dot(act, w2b_ref[...])
        rows = slice(r * FFN_ROW_TILE, (r + 1) * FFN_ROW_TILE)

        @pl.when(f == 0)
        def _():
            acc_ref[rows, :] = part

        @pl.when(f > 0)
        def _():
            acc_ref[rows, :] += part

    @pl.when(f == pl.num_programs(1) - 1)
    def _():
        y_ref[...] = acc_ref[...].astype(BF16).reshape(nb, cap, D_MODEL)


def _moe_ffn(xs, w1, w3, w2):
    b, ne, cap, _ = xs.shape
    nf = D_EXPERT // FFN_F_TILE
    tok = pl.BlockSpec((b, None, cap, D_MODEL), lambda e, f: (0, e, 0, 0))
    return pl.pallas_call(
        _moe_ffn_kernel,
        grid=(ne, nf),
        in_specs=[
            tok,
            pl.BlockSpec((None, D_MODEL, FFN_F_TILE), lambda e, f: (e, 0, f)),
            pl.BlockSpec((None, D_MODEL, FFN_F_TILE), lambda e, f: (e, 0, f)),
            pl.BlockSpec((None, FFN_F_TILE, D_MODEL), lambda e, f: (e, f, 0)),
        ],
        out_specs=tok,
        out_shape=jax.ShapeDtypeStruct(xs.shape, BF16),
        scratch_shapes=[
            pltpu.VMEM((b * cap, D_MODEL), F32),
            pltpu.VMEM((D_MODEL, FFN_F_TILE), BF16),
            pltpu.VMEM((D_MODEL, FFN_F_TILE), BF16),
            pltpu.VMEM((FFN_F_TILE, D_MODEL), BF16),
        ],
        compiler_params=_params(("parallel", "arbitrary")),
        name="moe_ffn",
    )(xs, w1, w3, w2)


def _moe_scatter_kernel(slot_ref, aff_ref, y_ref, x2_ref, g_ref, o_ref):
    slot = slot_ref[...]
    aff = aff_ref[...]
    rows, cap = slot.shape[0], y_ref.shape[1]
    ci = lax.broadcasted_iota(jnp.int32, (rows, cap), 1).astype(F32)
    acc = x2_ref[...]
    for e in range(N_EXPERTS):
        onehot = jnp.where(slot[:, e:e + 1] == ci, 1.0, 0.0).astype(BF16)
        acc = acc + aff[:, e:e + 1] * _dot(onehot, y_ref[e])
    o_ref[...] = _rms(acc, g_ref[...])


def _moe_scatter(slot, aff, y, x2, norm_g):
    b, seq, _ = x2.shape
    cap = y.shape[2]
    tile = lambda i, r: (i, r, 0)
    return pl.pallas_call(
        _moe_scatter_kernel,
        grid=(b, seq // ROW_TILE),
        in_specs=[
            pl.BlockSpec((None, ROW_TILE, LANES), tile),
            pl.BlockSpec((None, ROW_TILE, LANES), tile),
            pl.BlockSpec((None, N_EXPERTS, cap, D_MODEL), lambda i, r: (i, 0, 0, 0)),
            pl.BlockSpec((None, ROW_TILE, D_MODEL), tile),
            pl.BlockSpec((1, D_MODEL), lambda i, r: (0, 0)),
        ],
        out_specs=pl.BlockSpec((None, ROW_TILE, D_MODEL), tile),
        out_shape=jax.ShapeDtypeStruct((b, seq, D_MODEL), F32),
        compiler_params=_params(("parallel", "parallel")),
        name="moe_scatter",
    )(slot, aff, y, x2, norm_g)


def kernel(x, norm1_g, w_in, conv_w, conv_b, wq_m, wk_m, wv_m, w_if_fwd, b_if_fwd,
           w_if_bwd, b_if_bwd, mlstm_norm_g, mlstm_skip, attn_norm_g, w_out, norm2_g,
           w_router, w1, w3, w2, norm_f_g):
    b, seq, _ = x.shape
    assert w_in.shape[0] == 1, "single-layer problem"
    assert seq % ROW_TILE == 0 and seq % MLSTM_CHUNK == 0 and seq % ATTN_Q_TILE == 0
    cap = EC_CAPACITY * seq // N_EXPERTS
    assert FFN_ROW_TILE % cap == 0 and (b * cap) % FFN_ROW_TILE == 0
    nc = seq // MLSTM_CHUNK
    l = 0
    x2d = x.reshape(b * seq, D_MODEL)
    xm, z, qa, ka, va = _in_proj(x2d, norm1_g[l][None, :], w_in[l].astype(BF16), seq)
    shp = lambda t: t.reshape(b, seq, t.shape[-1])
    wif_rows, bif_rows = _gate_rows(w_if_fwd[l], b_if_fwd[l], w_if_bwd[l], b_if_bwd[l])
    xc, qm, ktm, vm, gates = _mlstm_pre(
        shp(xm), conv_w[l], conv_b[l][None, :],
        _block_diag(wq_m[l]).astype(BF16), _block_diag(wk_m[l]).T.astype(BF16),
        _block_diag(wv_m[l]).astype(BF16), wif_rows.astype(BF16), bif_rows)
    gates = gates.reshape(b, 2, MLSTM_HEADS, 8, nc, MLSTM_CHUNK).transpose(0, 2, 1, 4, 3, 5)
    gates = gates.reshape(b, MLSTM_HEADS, 2, nc * 8, MLSTM_CHUNK)
    ym = _mlstm(qm, ktm, vm, gates, xc, shp(z), mlstm_norm_g[l][None, :],
                mlstm_skip[l][None, :])
    ya = _dil_attn(shp(qa), shp(ka), shp(va), _dilated_log_multiplicity(seq))
    wr_pad = jnp.pad(w_router[l], ((0, 0), (0, LANES - N_EXPERTS)))
    x2, h2, logits = _out_proj(
        ym.reshape(b * seq, D_MLSTM), ya.reshape(b * seq, D_ATTN), x2d,
        attn_norm_g[l][None, :], w_out[l].astype(BF16), norm2_g[l][None, :], wr_pad)
    slot, slot_t, aff = _route(logits.reshape(b, seq, LANES), cap)
    xs = _moe_gather(slot_t.reshape(b, LANES, 1, seq), h2.reshape(b, seq, D_MODEL), cap)
    y = _moe_ffn(xs, w1[l], w3[l], w2[l])
    return _moe_scatter(slot, aff, y, x2.reshape(b, seq, D_MODEL), norm_f_g[None, :])
```

```python
import functools

import jax
import jax.numpy as jnp
from jax import lax
from jax.experimental import pallas as pl
from jax.experimental.pallas import tpu as pltpu

F32 = jnp.float32
BF16 = jnp.bfloat16

D_MODEL = 1024
D_MLSTM = 512
D_ATTN = 512
D_IN_PROJ = 2 * D_MLSTM + 3 * D_ATTN
MLSTM_HEADS = 4
MLSTM_HEAD_DIM = 128
MLSTM_QKV_BLOCK = 4
MLSTM_CONV = 5
ATTN_HEADS = 8
ATTN_HEAD_DIM = 64
ROPE_DIM = 16
ROPE_THETA = 500000.0
DILATED_PATTERNS = ((128, 1), (512, 4), (2048, 16))
N_EXPERTS = 16
EC_CAPACITY = 2
D_EXPERT = 2816
NORM_EPS = 1e-6
NEG_INF = -1e30

LANES = 128
MLSTM_CHUNK = 128
ROW_TILE = 512
ATTN_Q_TILE = 512
ROUTE_BLOCK = 256
FFN_F_TILE = 256
FFN_N_TILE = 256
FFN_ROW_TILE = 512
THRESHOLD_MANTISSA_STEPS = 40
VMEM_LIMIT = 56 * 1024 * 1024


def _params(sem):
    return pltpu.CompilerParams(dimension_semantics=sem, vmem_limit_bytes=VMEM_LIMIT)


def _rms(x, g):
    return x * lax.rsqrt(jnp.mean(x * x, axis=-1, keepdims=True) + NORM_EPS) * g


def _silu(x):
    return x * (1.0 / (1.0 + jnp.exp(-x)))


def _dot(a, b):
    return jnp.dot(a, b, preferred_element_type=F32)


def _dot_nt(a, b):
    return lax.dot_general(a, b, (((1,), (1,)), ((), ())), preferred_element_type=F32)


def _in_proj_kernel(x_ref, g_ref, w_ref, cos_ref, sa_ref, sb_ref,
                    xm_ref, z_ref, q_ref, k_ref, v_ref):
    h = _rms(x_ref[...], g_ref[...])
    p = _dot(h.astype(BF16), w_ref[...])
    xm_ref[...] = p[:, :D_MLSTM]
    z_ref[...] = p[:, D_MLSTM:2 * D_MLSTM]
    cos, sa, sb = cos_ref[...], sa_ref[...], sb_ref[...]
    half = ROPE_DIM // 2

    def rope(t):
        outs = []
        for j in range(D_ATTN // LANES):
            tj = t[:, j * LANES:(j + 1) * LANES]
            up = pltpu.roll(tj, LANES - half, axis=1)
            dn = pltpu.roll(tj, half, axis=1)
            outs.append(tj * cos + up * sa + dn * sb)
        return jnp.concatenate(outs, axis=1)

    o = 2 * D_MLSTM
    q_ref[...] = (rope(p[:, o:o + D_ATTN]) * (ATTN_HEAD_DIM ** -0.5)).astype(BF16)
    k_ref[...] = rope(p[:, o + D_ATTN:o + 2 * D_ATTN]).astype(BF16)
    v_ref[...] = p[:, o + 2 * D_ATTN:].astype(BF16)


def _rope_tables(seq):
    half = ROPE_DIM // 2
    inv_freq = ROPE_THETA ** (-2.0 * jnp.arange(half, dtype=F32) / ROPE_DIM)
    ang = jnp.arange(seq).astype(F32)[:, None] * inv_freq[None, :]
    cos, sin = jnp.cos(ang), jnp.sin(ang)
    pad = jnp.zeros((seq, ATTN_HEAD_DIM - ROPE_DIM), F32)
    cos_h = jnp.concatenate([cos, cos, pad + 1.0], axis=1)
    sa_h = jnp.concatenate([-sin, jnp.zeros_like(sin), pad], axis=1)
    sb_h = jnp.concatenate([jnp.zeros_like(sin), sin, pad], axis=1)
    rep = LANES // ATTN_HEAD_DIM
    return tuple(jnp.tile(t, (1, rep)) for t in (cos_h, sa_h, sb_h))


def _in_proj(x2d, g, w_bf, seq):
    n = x2d.shape[0]
    tiles_per_seq = seq // ROW_TILE
    cos, sa, sb = _rope_tables(seq)
    row = lambda i: (i, 0)
    fixed = lambda i: (0, 0)
    pos = lambda i: (i % tiles_per_seq, 0)
    return pl.pallas_call(
        _in_proj_kernel,
        grid=(n // ROW_TILE,),
        in_specs=[
            pl.BlockSpec((ROW_TILE, D_MODEL), row),
            pl.BlockSpec((1, D_MODEL), fixed),
            pl.BlockSpec((D_MODEL, D_IN_PROJ), fixed),
            pl.BlockSpec((ROW_TILE, LANES), pos),
            pl.BlockSpec((ROW_TILE, LANES), pos),
            pl.BlockSpec((ROW_TILE, LANES), pos),
        ],
        out_specs=[
            pl.BlockSpec((ROW_TILE, D_MLSTM), row),
            pl.BlockSpec((ROW_TILE, D_MLSTM), row),
            pl.BlockSpec((ROW_TILE, D_ATTN), row),
            pl.BlockSpec((ROW_TILE, D_ATTN), row),
            pl.BlockSpec((ROW_TILE, D_ATTN), row),
        ],
        out_shape=[
            jax.ShapeDtypeStruct((n, D_MLSTM), F32),
            jax.ShapeDtypeStruct((n, D_MLSTM), F32),
            jax.ShapeDtypeStruct((n, D_ATTN), BF16),
            jax.ShapeDtypeStruct((n, D_ATTN), BF16),
            jax.ShapeDtypeStruct((n, D_ATTN), BF16),
        ],
        compiler_params=_params(("parallel",)),
        name="in_proj",
    )(x2d, g, w_bf, cos, sa, sb)


def _mlstm_pre_kernel(xm_ref, cw_ref, cb_ref, wq_ref, wkt_ref, wv_ref, wif_ref, bif_ref,
                      xc_ref, q_ref, kt_ref, v_ref, g_ref):
    x = xm_ref[...]
    seq = x.shape[0]
    t = lax.broadcasted_iota(jnp.int32, x.shape, 0)
    acc = jnp.zeros_like(x) + cb_ref[...]
    for j in range(MLSTM_CONV):
        d = j - MLSTM_CONV // 2
        if d == 0:
            tap = x
        else:
            tap = pltpu.roll(x, (-d) % seq, axis=0)
            tap = jnp.where((t + d >= 0) & (t + d < seq), tap, 0.0)
        acc = acc + tap * cw_ref[j:j + 1, :]
    xc = _silu(acc)
    xc_ref[...] = xc
    xcb = xc.astype(BF16)
    q = _dot(xcb, wq_ref[...]).astype(BF16)
    kt = (_dot_nt(wkt_ref[...], xcb) * (MLSTM_HEAD_DIM ** -0.5)).astype(BF16)
    v = _dot(x.astype(BF16), wv_ref[...]).astype(BF16)
    q_ref[...] = q
    v_ref[...] = v
    L = MLSTM_CHUNK
    for c in range(seq // L):
        kt_ref[c] = kt[:, c * L:(c + 1) * L]
    wif = wif_ref[...]
    g_ref[...] = (_dot_nt(wif[:, :D_MLSTM], q) + _dot(wif[:, D_MLSTM:2 * D_MLSTM], kt)
                  + _dot_nt(wif[:, 2 * D_MLSTM:], v) + bif_ref[...])


def _block_diag(w):
    nblk = w.shape[0]
    n = nblk * MLSTM_QKV_BLOCK
    tiled = jnp.tile(w.reshape(n, MLSTM_QKV_BLOCK), (1, nblk))
    blk = jnp.arange(n) // MLSTM_QKV_BLOCK
    return jnp.where(blk[:, None] == blk[None, :], tiled, 0.0)


def _gate_rows(w_f, b_f, w_b, b_b):
    h = MLSTM_HEADS
    cols = []
    bias = []
    zero_w = jnp.zeros((w_f.shape[0],), F32)
    for off in (h, 0):
        for hd in range(h):
            cols += [w_f[:, off + hd], w_b[:, off + hd]] + [zero_w] * 6
            bias += [b_f[off + hd], b_b[off + hd]] + [jnp.zeros((), F32)] * 6
    return jnp.stack(cols, axis=0), jnp.stack(bias)[:, None]


def _mlstm_pre(xm, conv_w, conv_b, wq, wkt, wv, wif_rows, bif_rows):
    b, seq, _ = xm.shape
    nrow = wif_rows.shape[0]
    nc = seq // MLSTM_CHUNK
    per_b = lambda i: (i, 0, 0)
    fixed = lambda i: (0, 0)
    return pl.pallas_call(
        _mlstm_pre_kernel,
        grid=(b,),
        in_specs=[
            pl.BlockSpec((None, seq, D_MLSTM), per_b),
            pl.BlockSpec((MLSTM_CONV, D_MLSTM), fixed),
            pl.BlockSpec((1, D_MLSTM), fixed),
            pl.BlockSpec((D_MLSTM, D_MLSTM), fixed),
            pl.BlockSpec((D_MLSTM, D_MLSTM), fixed),
            pl.BlockSpec((D_MLSTM, D_MLSTM), fixed),
            pl.BlockSpec((nrow, 3 * D_MLSTM), fixed),
            pl.BlockSpec((nrow, 1), fixed),
        ],
        out_specs=[
            pl.BlockSpec((None, seq, D_MLSTM), per_b),
            pl.BlockSpec((None, seq, D_MLSTM), per_b),
            pl.BlockSpec((None, nc, D_MLSTM, MLSTM_CHUNK), lambda i: (i, 0, 0, 0)),
            pl.BlockSpec((None, seq, D_MLSTM), per_b),
            pl.BlockSpec((None, nrow, seq), per_b),
        ],
        out_shape=[
            jax.ShapeDtypeStruct((b, seq, D_MLSTM), F32),
            jax.ShapeDtypeStruct((b, seq, D_MLSTM), BF16),
            jax.ShapeDtypeStruct((b, nc, D_MLSTM, MLSTM_CHUNK), BF16),
            jax.ShapeDtypeStruct((b, seq, D_MLSTM), BF16),
            jax.ShapeDtypeStruct((b, nrow, seq), F32),
        ],
        compiler_params=_params(("parallel",)),
        name="mlstm_pre",
    )(xm, conv_w, conv_b, wq, wkt, wv, wif_rows, bif_rows)


def _log_sigmoid(x):
    return jnp.minimum(x, 0.0) - jnp.log1p(jnp.exp(-jnp.abs(x)))


def _mlstm_kernel(q_ref, kt_ref, v_ref, g_ref, xc_ref, z_ref, ng_ref, sk_ref, o_ref,
                  cum_ref, w_ref, ml_ref, tot_ref, dc_ref, cs_ref, ms_ref):
    L = MLSTM_CHUNK
    dh = MLSTM_HEAD_DIM
    nc = kt_ref.shape[0]
    rows_all = g_ref.shape[1]
    lane = lax.broadcasted_iota(jnp.int32, (rows_all, L), 1)
    fwd_row = lax.broadcasted_iota(jnp.int32, (rows_all, L), 0) % 8 == 0
    row_i = lax.broadcasted_iota(jnp.int32, (L, L), 0)
    col_i = lax.broadcasted_iota(jnp.int32, (L, L), 1)
    ones_col = jnp.where(lax.broadcasted_iota(jnp.int32, (L, dh), 1) == 0, 1.0, 0.0).astype(BF16)

    lf = _log_sigmoid(g_ref[0])
    pre, suf = lf, lf
    d = 1
    while d < L:
        pre = pre + jnp.where(lane >= d, pltpu.roll(pre, d, axis=1), 0.0)
        suf = suf + jnp.where(lane < L - d, pltpu.roll(suf, L - d, axis=1), 0.0)
        d *= 2
    cum = jnp.where(fwd_row, pre, suf)
    tot = jnp.where(fwd_row, cum[:, L - 1:L], cum[:, 0:1])
    a = tot - cum + g_ref[1]
    ml = jnp.max(a, axis=1, keepdims=True)
    cum_ref[...] = cum
    w_ref[...] = jnp.exp(a - ml)
    ml_ref[...] = jnp.broadcast_to(ml, (rows_all, L))
    tot_ref[...] = tot

    def chunk_rows(ref, c):
        return ref[pl.ds(pl.multiple_of(c * 8, 8), 8), :]

    def v_aug(c):
        return jnp.concatenate([v_ref[pl.ds(c * L, L), :], ones_col], axis=1)

    def phase_a(c, carry):
        kt = kt_ref[c].astype(F32)
        w = chunk_rows(w_ref, c)
        kw = jnp.concatenate([kt * w[0:1], kt * w[1:2]], axis=0)
        dc_ref[c] = _dot(kw.astype(BF16), v_aug(c))
        return carry

    lax.fori_loop(0, nc, phase_a, 0, unroll=2)

    def scan_dir(direction_row, reverse):
        off_rows = direction_row * dh
        off_cols = direction_row * 2 * dh

        def body(i, carry):
            c = (nc - 1 - i) if reverse else i
            state, m = carry
            cs_ref[c, :, off_cols:off_cols + 2 * dh] = state.astype(BF16)
            ms_ref[c, direction_row:direction_row + 1, :] = m
            ml_c = chunk_rows(ml_ref, c)[direction_row:direction_row + 1]
            g_c = chunk_rows(tot_ref, c)[direction_row:direction_row + 1]
            m_new = jnp.maximum(g_c + m, ml_c)
            alpha = jnp.exp(g_c + m - m_new)
            beta = jnp.exp(ml_c - m_new)
            alpha2 = jnp.concatenate([alpha, alpha], axis=1)
            beta2 = jnp.concatenate([beta, beta], axis=1)
            state = alpha2 * state + beta2 * dc_ref[c, off_rows:off_rows + dh, :]
            return state, m_new

        init = (jnp.zeros((dh, 2 * dh), F32), jnp.zeros((1, L), F32))
        lax.fori_loop(0, nc, body, init)

    scan_dir(0, False)
    scan_dir(1, True)

    ng = ng_ref[...]
    sk = sk_ref[...]

    def to_col(r):
        return jnp.sum(jnp.where(row_i == col_i, jnp.broadcast_to(r, (L, L)), 0.0),
                       axis=1, keepdims=True)

    def direction(s_qk, qc, vaug, i_r, b_r, m_prev, keep):
        b_c = to_col(b_r)
        log_d = jnp.where(keep, b_c - b_r + i_r, NEG_INF)
        inter = b_c + m_prev
        m_t = jnp.maximum(inter, jnp.max(log_d, axis=1, keepdims=True))
        w = jnp.exp(log_d - m_t) * s_qk
        scale = jnp.exp(inter - m_t)
        intra = _dot(w.astype(BF16), vaug)
        tot_c = intra + scale * qc
        den = tot_c[:, dh:dh + 1]
        return tot_c[:, :dh] / jnp.maximum(jnp.abs(den), jnp.exp(-m_t))

    def phase_c(c, carry):
        rows = pl.ds(c * L, L)
        q = q_ref[rows, :]
        s_qk = _dot(q, kt_ref[c])
        qc = _dot(q, cs_ref[c])
        vaug = v_aug(c)
        ms = ms_ref[c]
        b = chunk_rows(cum_ref, c)
        gi = chunk_rows(g_ref.at[1], c)
        h = (direction(s_qk, qc[:, :2 * dh], vaug, gi[0:1], b[0:1], ms[0:1, 0:1], col_i <= row_i)
             + direction(s_qk, qc[:, 2 * dh:], vaug, gi[1:2], b[1:2], ms[1:2, 0:1], col_i >= row_i))
        hn = _rms(h, ng)
        o_ref[rows, :] = (hn + sk * xc_ref[rows, :]) * _silu(z_ref[rows, :])
        return carry

    lax.fori_loop(0, nc, phase_c, 0, unroll=2)


def _mlstm(q, kt, v, gates, xc, z, norm_g, skip):
    b, seq, _ = q.shape
    nc = seq // MLSTM_CHUNK
    dh = MLSTM_HEAD_DIM
    head = lambda i, j: (i, 0, j)
    vec = lambda i, j: (0, j)
    blk = pl.BlockSpec((None, seq, dh), head)
    gate_rows = pltpu.VMEM((nc * 8, MLSTM_CHUNK), F32)
    return pl.pallas_call(
        _mlstm_kernel,
        grid=(b, MLSTM_HEADS),
        in_specs=[
            blk,
            pl.BlockSpec((None, nc, dh, MLSTM_CHUNK), lambda i, j: (i, 0, j, 0)),
            blk,
            pl.BlockSpec((None, None, 2, nc * 8, MLSTM_CHUNK), lambda i, j: (i, j, 0, 0, 0)),
            blk, blk,
            pl.BlockSpec((1, dh), vec),
            pl.BlockSpec((1, dh), vec),
        ],
        out_specs=blk,
        out_shape=jax.ShapeDtypeStruct((b, seq, D_MLSTM), F32),
        scratch_shapes=[
            gate_rows, gate_rows, gate_rows, gate_rows,
            pltpu.VMEM((nc, 2 * dh, 2 * dh), F32),
            pltpu.VMEM((nc, dh, 4 * dh), BF16),
            pltpu.VMEM((nc, 8, MLSTM_CHUNK), F32),
        ],
        compiler_params=_params(("parallel", "parallel")),
        name="mlstm",
    )(q, kt, v, gates, xc, z, norm_g, skip)


def _dilated_log_multiplicity(seq):
    d = jnp.arange(seq)[:, None] - jnp.arange(seq)[None, :]
    c = jnp.zeros((seq, seq), F32)
    for win, dil in DILATED_PATTERNS:
        half = win // (2 * dil)
        c = c + ((d % dil == 0) & (jnp.abs(d) <= half * dil)).astype(F32)
    return jnp.where(c > 0.0, jnp.log(jnp.maximum(c, 1.0)), NEG_INF)


def _dil_attn_kernel(q_ref, k_ref, v_ref, bias_ref, o_ref):
    q = q_ref[...]
    k = k_ref[...]
    v = v_ref[...]
    first = lax.broadcasted_iota(jnp.int32, (1, LANES), 1) < ATTN_HEAD_DIM
    zero = jnp.zeros_like(q)
    outs = []
    for sel in (first, jnp.logical_not(first)):
        s = _dot_nt(jnp.where(sel, q, zero), k) + bias_ref[...]
        m = jnp.max(s, axis=1, keepdims=True)
        p = jnp.exp(s - m)
        den = jnp.sum(p, axis=1, keepdims=True)
        outs.append(_dot(p.astype(BF16), v) / den)
    o_ref[...] = jnp.where(first, outs[0], outs[1])


def _dil_attn(q, k, v, bias):
    b, seq, _ = q.shape
    pairs = D_ATTN // LANES
    return pl.pallas_call(
        _dil_attn_kernel,
        grid=(seq // ATTN_Q_TILE, b, pairs),
        in_specs=[
            pl.BlockSpec((None, ATTN_Q_TILE, LANES), lambda t, i, j: (i, t, j)),
            pl.BlockSpec((None, seq, LANES), lambda t, i, j: (i, 0, j)),
            pl.BlockSpec((None, seq, LANES), lambda t, i, j: (i, 0, j)),
            pl.BlockSpec((ATTN_Q_TILE, seq), lambda t, i, j: (t, 0)),
        ],
        out_specs=pl.BlockSpec((None, ATTN_Q_TILE, LANES), lambda t, i, j: (i, t, j)),
        out_shape=jax.ShapeDtypeStruct((b, seq, D_ATTN), F32),
        compiler_params=_params(("parallel", "parallel", "parallel")),
        name="dil_attn",
    )(q, k, v, bias)


def _out_proj_kernel(ym_ref, ya_ref, x_ref, ag_ref, w_ref, n2_ref, wr_ref,
                     x2_ref, h2_ref, lg_ref):
    ya = _rms(ya_ref[...], ag_ref[...])
    mixed = jnp.concatenate([ym_ref[...], ya], axis=1).astype(BF16)
    x2 = x_ref[...] + _dot(mixed, w_ref[...])
    x2_ref[...] = x2
    h2 = _rms(x2, n2_ref[...])
    hi = h2.astype(BF16)
    h2_ref[...] = hi
    lo = (h2 - hi.astype(F32)).astype(BF16)
    wr = wr_ref[...]
    w_hi = wr.astype(BF16)
    w_lo = (wr - w_hi.astype(F32)).astype(BF16)
    lg_ref[...] = _dot(jnp.concatenate([hi, lo, hi], axis=1),
                       jnp.concatenate([w_hi, w_hi, w_lo], axis=0))


def _out_proj(ym, ya, x2d, attn_g, w_bf, n2g, wr_pad):
    n = x2d.shape[0]
    row = lambda i: (i, 0)
    fixed = lambda i: (0, 0)
    return pl.pallas_call(
        _out_proj_kernel,
        grid=(n // ROW_TILE,),
        in_specs=[
            pl.BlockSpec((ROW_TILE, D_MLSTM), row),
            pl.BlockSpec((ROW_TILE, D_ATTN), row),
            pl.BlockSpec((ROW_TILE, D_MODEL), row),
            pl.BlockSpec((1, D_ATTN), fixed),
            pl.BlockSpec((D_MODEL, D_MODEL), fixed),
            pl.BlockSpec((1, D_MODEL), fixed),
            pl.BlockSpec((D_MODEL, LANES), fixed),
        ],
        out_specs=[
            pl.BlockSpec((ROW_TILE, D_MODEL), row),
            pl.BlockSpec((ROW_TILE, D_MODEL), row),
            pl.BlockSpec((ROW_TILE, LANES), row),
        ],
        out_shape=[
            jax.ShapeDtypeStruct((n, D_MODEL), F32),
            jax.ShapeDtypeStruct((n, D_MODEL), BF16),
            jax.ShapeDtypeStruct((n, LANES), F32),
        ],
        compiler_params=_params(("parallel",)),
        name="out_proj",
    )(ym, ya, x2d, attn_g, w_bf, n2g, wr_pad)


def _route_kernel(lg_ref, tri_ref, eye_ref, slot_ref, slot_t_ref, aff_ref, *, cap):
    lg = lg_ref[...]
    valid = lax.broadcasted_iota(jnp.int32, (1, LANES), 1) < N_EXPERTS
    lg = jnp.where(valid, lg, NEG_INF)
    e = jnp.exp(lg - jnp.max(lg, axis=1, keepdims=True))
    aff = e / jnp.sum(e, axis=1, keepdims=True)
    aff_ref[...] = aff
    def enough(cand):
        return jnp.sum(jnp.where(aff >= cand, 1.0, 0.0), axis=0, keepdims=True) >= cap

    tiny = jnp.full((1, LANES), 2.0 ** -126, F32)
    normal = enough(tiny)
    p = tiny
    for bit in range(6, -1, -1):
        cand = p * (2.0 ** (2 ** bit))
        p = jnp.where(enough(cand), cand, p)
    lo = jnp.where(normal, p, 0.0)
    hi = jnp.where(normal, p * 2.0, tiny)
    step = jnp.where(normal, p * 0.5, 0.0)
    for _ in range(THRESHOLD_MANTISSA_STEPS):
        cand = lo + step
        ok = enough(cand)
        lo = jnp.where(ok, cand, lo)
        hi = jnp.where(ok, hi, cand)
        step = step * 0.5
    gt = jnp.where(aff >= hi, 1.0, 0.0)
    eq = jnp.where(aff >= lo, 1.0, 0.0) - gt
    need = cap - jnp.sum(gt, axis=0, keepdims=True)
    tri = tri_ref[...]

    def count_before(x):
        blk = tri.shape[0]
        run = jnp.zeros((1, LANES), F32)
        outs = []
        for j in range(x.shape[0] // blk):
            xb = x[j * blk:(j + 1) * blk]
            outs.append(_dot(tri, xb.astype(BF16)) + run)
            run = run + jnp.sum(xb, axis=0, keepdims=True)
        return jnp.concatenate(outs, axis=0)

    sel = gt + eq * jnp.where(count_before(eq) < need, 1.0, 0.0)
    pos = count_before(sel)
    slot = jnp.where(valid & (sel > 0.0), pos, -1.0)
    slot_ref[...] = slot
    slot_t_ref[...] = _dot_nt(eye_ref[...], slot.astype(BF16))


def _route(logits, cap):
    b, seq, _ = logits.shape
    tri = (jnp.arange(ROUTE_BLOCK)[None, :] < jnp.arange(ROUTE_BLOCK)[:, None]).astype(BF16)
    eye = jnp.eye(LANES, dtype=BF16)
    assert seq % ROUTE_BLOCK == 0
    per_b = lambda i: (i, 0, 0)
    fixed = lambda i: (0, 0)
    return pl.pallas_call(
        functools.partial(_route_kernel, cap=cap),
        grid=(b,),
        in_specs=[
            pl.BlockSpec((None, seq, LANES), per_b),
            pl.BlockSpec((ROUTE_BLOCK, ROUTE_BLOCK), fixed),
            pl.BlockSpec((LANES, LANES), fixed),
        ],
        out_specs=[
            pl.BlockSpec((None, seq, LANES), per_b),
            pl.BlockSpec((None, LANES, seq), per_b),
            pl.BlockSpec((None, seq, LANES), per_b),
        ],
        out_shape=[
            jax.ShapeDtypeStruct((b, seq, LANES), F32),
            jax.ShapeDtypeStruct((b, LANES, seq), F32),
            jax.ShapeDtypeStruct((b, seq, LANES), F32),
        ],
        compiler_params=_params(("parallel",)),
        name="route",
    )(logits, tri, eye)


def _moe_gather_kernel(slot_ref, h_ref, xs_ref):
    srow = slot_ref[...]
    cap, seq = xs_ref.shape[0], srow.shape[1]
    ci = lax.broadcasted_iota(jnp.int32, (cap, seq), 0).astype(F32)
    onehot = jnp.where(srow == ci, 1.0, 0.0).astype(BF16)
    xs_ref[...] = _dot(onehot, h_ref[...]).astype(BF16)


def _moe_gather(slot_t, h2, cap):
    b, seq, _ = h2.shape
    return pl.pallas_call(
        _moe_gather_kernel,
        grid=(b, N_EXPERTS),
        in_specs=[
            pl.BlockSpec((None, None, 1, seq), lambda i, e: (i, e, 0, 0)),
            pl.BlockSpec((None, seq, D_MODEL), lambda i, e: (i, 0, 0)),
        ],
        out_specs=pl.BlockSpec((None, None, cap, D_MODEL), lambda i, e: (i, e, 0, 0)),
        out_shape=jax.ShapeDtypeStruct((b, N_EXPERTS, cap, D_MODEL), BF16),
        compiler_params=_params(("parallel", "parallel")),
        name="moe_gather",
    )(slot_t, h2)


def _moe_ffn_kernel(xs_ref, w1_ref, w3_ref, w2_ref, y_ref, act_ref, w1b_ref, w3b_ref, w2b_ref):
    s = pl.program_id(1)
    nb, cap, _ = xs_ref.shape
    nf = act_ref.shape[0]
    per = FFN_ROW_TILE // cap
    row_tiles = nb // per

    @pl.when(s < nf)
    def _():
        w1b_ref[...] = w1_ref[...].astype(BF16)
        w3b_ref[...] = w3_ref[...].astype(BF16)
        for r in range(row_tiles):
            x = xs_ref[r * per:(r + 1) * per].reshape(FFN_ROW_TILE, D_MODEL)
            up = _dot(x, w1b_ref[...])
            gt = _dot(x, w3b_ref[...])
            act_ref[s, r * FFN_ROW_TILE:(r + 1) * FFN_ROW_TILE, :] = (_silu(up) * gt).astype(BF16)

    @pl.when(s >= nf)
    def _():
        w2b_ref[...] = w2_ref[...].astype(BF16)
        for r in range(row_tiles):
            rows = slice(r * FFN_ROW_TILE, (r + 1) * FFN_ROW_TILE)
            act = jnp.concatenate([act_ref[f, rows, :] for f in range(nf)], axis=1)
            y = _dot(act, w2b_ref[...])
            y_ref[r * per:(r + 1) * per] = y.astype(BF16).reshape(per, cap, y.shape[1])


def _moe_ffn(xs, w1, w3, w2):
    b, ne, cap, _ = xs.shape
    nf = D_EXPERT // FFN_F_TILE
    nn = D_MODEL // FFN_N_TILE
    hidden = lambda e, s: (e, 0, jnp.minimum(s, nf - 1))
    out_col = lambda e, s: jnp.maximum(s - nf, 0)
    return pl.pallas_call(
        _moe_ffn_kernel,
        grid=(ne, nf + nn),
        in_specs=[
            pl.BlockSpec((b, None, cap, D_MODEL), lambda e, s: (0, e, 0, 0)),
            pl.BlockSpec((None, D_MODEL, FFN_F_TILE), hidden),
            pl.BlockSpec((None, D_MODEL, FFN_F_TILE), hidden),
            pl.BlockSpec((None, D_EXPERT, FFN_N_TILE), lambda e, s: (e, 0, out_col(e, s))),
        ],
        out_specs=pl.BlockSpec((b, None, cap, FFN_N_TILE), lambda e, s: (0, e, 0, out_col(e, s))),
        out_shape=jax.ShapeDtypeStruct(xs.shape, BF16),
        scratch_shapes=[
            pltpu.VMEM((nf, b * cap, FFN_F_TILE), BF16),
            pltpu.VMEM((D_MODEL, FFN_F_TILE), BF16),
            pltpu.VMEM((D_MODEL, FFN_F_TILE), BF16),
            pltpu.VMEM((D_EXPERT, FFN_N_TILE), BF16),
        ],
        compiler_params=_params(("parallel", "arbitrary")),
        name="moe_ffn",
    )(xs, w1, w3, w2)


def _moe_scatter_kernel(slot_ref, aff_ref, y_ref, x2_ref, g_ref, o_ref):
    slot = slot_ref[...]
    aff = aff_ref[...]
    rows, cap = slot.shape[0], y_ref.shape[1]
    ci = lax.broadcasted_iota(jnp.int32, (rows, cap), 1).astype(F32)
    acc = x2_ref[...]
    for e in range(N_EXPERTS):
        onehot = jnp.where(slot[:, e:e + 1] == ci, 1.0, 0.0).astype(BF16)
        acc = acc + aff[:, e:e + 1] * _dot(onehot, y_ref[e])
    o_ref[...] = _rms(acc, g_ref[...])


def _moe_scatter(slot, aff, y, x2, norm_g):
    b, seq, _ = x2.shape
    cap = y.shape[2]
    tile = lambda i, r: (i, r, 0)
    return pl.pallas_call(
        _moe_scatter_kernel,
        grid=(b, seq // ROW_TILE),
        in_specs=[
            pl.BlockSpec((None, ROW_TILE, LANES), tile),
            pl.BlockSpec((None, ROW_TILE, LANES), tile),
            pl.BlockSpec((None, N_EXPERTS, cap, D_MODEL), lambda i, r: (i, 0, 0, 0)),
            pl.BlockSpec((None, ROW_TILE, D_MODEL), tile),
            pl.BlockSpec((1, D_MODEL), lambda i, r: (0, 0)),
        ],
        out_specs=pl.BlockSpec((None, ROW_TILE, D_MODEL), tile),
        out_shape=jax.ShapeDtypeStruct((b, seq, D_MODEL), F32),
        compiler_params=_params(("parallel", "parallel")),
        name="moe_scatter",
    )(slot, aff, y, x2, norm_g)


def kernel(x, norm1_g, w_in, conv_w, conv_b, wq_m, wk_m, wv_m, w_if_fwd, b_if_fwd,
           w_if_bwd, b_if_bwd, mlstm_norm_g, mlstm_skip, attn_norm_g, w_out, norm2_g,
           w_router, w1, w3, w2, norm_f_g):
    b, seq, _ = x.shape
    assert w_in.shape[0] == 1, "single-layer problem"
    assert seq % ROW_TILE == 0 and seq % MLSTM_CHUNK == 0 and seq % ATTN_Q_TILE == 0
    cap = EC_CAPACITY * seq // N_EXPERTS
    assert FFN_ROW_TILE % cap == 0 and (b * cap) % FFN_ROW_TILE == 0
    nc = seq // MLSTM_CHUNK
    l = 0
    x2d = x.reshape(b * seq, D_MODEL)
    xm, z, qa, ka, va = _in_proj(x2d, norm1_g[l][None, :], w_in[l].astype(BF16), seq)
    shp = lambda t: t.reshape(b, seq, t.shape[-1])
    wif_rows, bif_rows = _gate_rows(w_if_fwd[l], b_if_fwd[l], w_if_bwd[l], b_if_bwd[l])
    xc, qm, ktm, vm, gates = _mlstm_pre(
        shp(xm), conv_w[l], conv_b[l][None, :],
        _block_diag(wq_m[l]).astype(BF16), _block_diag(wk_m[l]).T.astype(BF16),
        _block_diag(wv_m[l]).astype(BF16), wif_rows.astype(BF16), bif_rows)
    gates = gates.reshape(b, 2, MLSTM_HEADS, 8, nc, MLSTM_CHUNK).transpose(0, 2, 1, 4, 3, 5)
    gates = gates.reshape(b, MLSTM_HEADS, 2, nc * 8, MLSTM_CHUNK)
    ym = _mlstm(qm, ktm, vm, gates, xc, shp(z), mlstm_norm_g[l][None, :],
                mlstm_skip[l][None, :])
    ya = _dil_attn(shp(qa), shp(ka), shp(va), _dilated_log_multiplicity(seq))
    wr_pad = jnp.pad(w_router[l], ((0, 0), (0, LANES - N_EXPERTS)))
    x2, h2, logits = _out_proj(
        ym.reshape(b * seq, D_MLSTM), ya.reshape(b * seq, D_ATTN), x2d,
        attn_norm_g[l][None, :], w_out[l].astype(BF16), norm2_g[l][None, :], wr_pad)
    slot, slot_t, aff = _route(logits.reshape(b, seq, LANES), cap)
    xs = _moe_gather(slot_t.reshape(b, LANES, 1, seq), h2.reshape(b, seq, D_MODEL), cap)
    y = _moe_ffn(xs, w1[l], w3[l], w2[l])
    return _moe_scatter(slot, aff, y, x2.reshape(b, seq, D_MODEL), norm_f_g[None, :])
```

```python
import functools

import jax
import jax.numpy as jnp
from jax import lax
from jax.experimental import pallas as pl
from jax.experimental.pallas import tpu as pltpu

F32 = jnp.float32
BF16 = jnp.bfloat16

D_MODEL = 1024
D_MLSTM = 512
D_ATTN = 512
D_IN_PROJ = 2 * D_MLSTM + 3 * D_ATTN
MLSTM_HEADS = 4
MLSTM_HEAD_DIM = 128
MLSTM_QKV_BLOCK = 4
MLSTM_CONV = 5
ATTN_HEADS = 8
ATTN_HEAD_DIM = 64
ROPE_DIM = 16
ROPE_THETA = 500000.0
DILATED_PATTERNS = ((128, 1), (512, 4), (2048, 16))
N_EXPERTS = 16
EC_CAPACITY = 2
D_EXPERT = 2816
NORM_EPS = 1e-6
NEG_INF = -1e30

LANES = 128
MLSTM_CHUNK = 128
ROW_TILE = 512
ATTN_Q_TILE = 128
ATTN_CLASSES_PER_STEP = 4
ROUTE_BLOCK = 256
FFN_F_TILE = 256
FFN_N_TILE = 256
FFN_ROW_TILE = 512
THRESHOLD_MANTISSA_STEPS = 40
VMEM_LIMIT = 56 * 1024 * 1024


def _params(sem):
    return pltpu.CompilerParams(dimension_semantics=sem, vmem_limit_bytes=VMEM_LIMIT)


def _rms(x, g):
    return x * lax.rsqrt(jnp.mean(x * x, axis=-1, keepdims=True) + NORM_EPS) * g


def _silu(x):
    return x * (1.0 / (1.0 + jnp.exp(-x)))


def _dot(a, b):
    return jnp.dot(a, b, preferred_element_type=F32)


def _dot_nt(a, b):
    return lax.dot_general(a, b, (((1,), (1,)), ((), ())), preferred_element_type=F32)


def _in_proj_kernel(x_ref, g_ref, w_ref, cos_ref, sa_ref, sb_ref,
                    xm_ref, z_ref, q_ref, k_ref, v_ref):
    h = _rms(x_ref[...], g_ref[...])
    p = _dot(h.astype(BF16), w_ref[...])
    xm_ref[...] = p[:, :D_MLSTM]
    z_ref[...] = p[:, D_MLSTM:2 * D_MLSTM]
    cos, sa, sb = cos_ref[...], sa_ref[...], sb_ref[...]
    half = ROPE_DIM // 2

    def rope(t):
        outs = []
        for j in range(D_ATTN // LANES):
            tj = t[:, j * LANES:(j + 1) * LANES]
            up = pltpu.roll(tj, LANES - half, axis=1)
            dn = pltpu.roll(tj, half, axis=1)
            outs.append(tj * cos + up * sa + dn * sb)
        return jnp.concatenate(outs, axis=1)

    o = 2 * D_MLSTM
    q_ref[...] = (rope(p[:, o:o + D_ATTN]) * (ATTN_HEAD_DIM ** -0.5)).astype(BF16)
    k_ref[...] = rope(p[:, o + D_ATTN:o + 2 * D_ATTN]).astype(BF16)
    v_ref[...] = p[:, o + 2 * D_ATTN:].astype(BF16)


def _rope_tables(seq):
    half = ROPE_DIM // 2
    inv_freq = ROPE_THETA ** (-2.0 * jnp.arange(half, dtype=F32) / ROPE_DIM)
    ang = jnp.arange(seq).astype(F32)[:, None] * inv_freq[None, :]
    cos, sin = jnp.cos(ang), jnp.sin(ang)
    pad = jnp.zeros((seq, ATTN_HEAD_DIM - ROPE_DIM), F32)
    cos_h = jnp.concatenate([cos, cos, pad + 1.0], axis=1)
    sa_h = jnp.concatenate([-sin, jnp.zeros_like(sin), pad], axis=1)
    sb_h = jnp.concatenate([jnp.zeros_like(sin), sin, pad], axis=1)
    rep = LANES // ATTN_HEAD_DIM
    return tuple(jnp.tile(t, (1, rep)) for t in (cos_h, sa_h, sb_h))


def _in_proj(x2d, g, w_bf, seq):
    n = x2d.shape[0]
    tiles_per_seq = seq // ROW_TILE
    cos, sa, sb = _rope_tables(seq)
    row = lambda i: (i, 0)
    fixed = lambda i: (0, 0)
    pos = lambda i: (i % tiles_per_seq, 0)
    return pl.pallas_call(
        _in_proj_kernel,
        grid=(n // ROW_TILE,),
        in_specs=[
            pl.BlockSpec((ROW_TILE, D_MODEL), row),
            pl.BlockSpec((1, D_MODEL), fixed),
            pl.BlockSpec((D_MODEL, D_IN_PROJ), fixed),
            pl.BlockSpec((ROW_TILE, LANES), pos),
            pl.BlockSpec((ROW_TILE, LANES), pos),
            pl.BlockSpec((ROW_TILE, LANES), pos),
        ],
        out_specs=[
            pl.BlockSpec((ROW_TILE, D_MLSTM), row),
            pl.BlockSpec((ROW_TILE, D_MLSTM), row),
            pl.BlockSpec((ROW_TILE, D_ATTN), row),
            pl.BlockSpec((ROW_TILE, D_ATTN), row),
            pl.BlockSpec((ROW_TILE, D_ATTN), row),
        ],
        out_shape=[
            jax.ShapeDtypeStruct((n, D_MLSTM), F32),
            jax.ShapeDtypeStruct((n, D_MLSTM), F32),
            jax.ShapeDtypeStruct((n, D_ATTN), BF16),
            jax.ShapeDtypeStruct((n, D_ATTN), BF16),
            jax.ShapeDtypeStruct((n, D_ATTN), BF16),
        ],
        compiler_params=_params(("parallel",)),
        name="in_proj",
    )(x2d, g, w_bf, cos, sa, sb)


def _mlstm_pre_kernel(xm_ref, cw_ref, cb_ref, wq_ref, wkt_ref, wv_ref, wif_ref, bif_ref,
                      xc_ref, q_ref, kt_ref, v_ref, g_ref):
    x = xm_ref[...]
    seq = x.shape[0]
    t = lax.broadcasted_iota(jnp.int32, x.shape, 0)
    acc = jnp.zeros_like(x) + cb_ref[...]
    for j in range(MLSTM_CONV):
        d = j - MLSTM_CONV // 2
        if d == 0:
            tap = x
        else:
            tap = pltpu.roll(x, (-d) % seq, axis=0)
            tap = jnp.where((t + d >= 0) & (t + d < seq), tap, 0.0)
        acc = acc + tap * cw_ref[j:j + 1, :]
    xc = _silu(acc)
    xc_ref[...] = xc
    xcb = xc.astype(BF16)
    q = _dot(xcb, wq_ref[...]).astype(BF16)
    kt = (_dot_nt(wkt_ref[...], xcb) * (MLSTM_HEAD_DIM ** -0.5)).astype(BF16)
    v = _dot(x.astype(BF16), wv_ref[...]).astype(BF16)
    q_ref[...] = q
    v_ref[...] = v
    L = MLSTM_CHUNK
    for c in range(seq // L):
        kt_ref[c] = kt[:, c * L:(c + 1) * L]
    wif = wif_ref[...]
    g_ref[...] = (_dot_nt(wif[:, :D_MLSTM], q) + _dot(wif[:, D_MLSTM:2 * D_MLSTM], kt)
                  + _dot_nt(wif[:, 2 * D_MLSTM:], v) + bif_ref[...])


def _block_diag(w):
    nblk = w.shape[0]
    n = nblk * MLSTM_QKV_BLOCK
    tiled = jnp.tile(w.reshape(n, MLSTM_QKV_BLOCK), (1, nblk))
    blk = jnp.arange(n) // MLSTM_QKV_BLOCK
    return jnp.where(blk[:, None] == blk[None, :], tiled, 0.0)


def _gate_rows(w_f, b_f, w_b, b_b):
    h = MLSTM_HEADS
    cols = []
    bias = []
    zero_w = jnp.zeros((w_f.shape[0],), F32)
    for off in (h, 0):
        for hd in range(h):
            cols += [w_f[:, off + hd], w_b[:, off + hd]] + [zero_w] * 6
            bias += [b_f[off + hd], b_b[off + hd]] + [jnp.zeros((), F32)] * 6
    return jnp.stack(cols, axis=0), jnp.stack(bias)[:, None]


def _mlstm_pre(xm, conv_w, conv_b, wq, wkt, wv, wif_rows, bif_rows):
    b, seq, _ = xm.shape
    nrow = wif_rows.shape[0]
    nc = seq // MLSTM_CHUNK
    per_b = lambda i: (i, 0, 0)
    fixed = lambda i: (0, 0)
    return pl.pallas_call(
        _mlstm_pre_kernel,
        grid=(b,),
        in_specs=[
            pl.BlockSpec((None, seq, D_MLSTM), per_b),
            pl.BlockSpec((MLSTM_CONV, D_MLSTM), fixed),
            pl.BlockSpec((1, D_MLSTM), fixed),
            pl.BlockSpec((D_MLSTM, D_MLSTM), fixed),
            pl.BlockSpec((D_MLSTM, D_MLSTM), fixed),
            pl.BlockSpec((D_MLSTM, D_MLSTM), fixed),
            pl.BlockSpec((nrow, 3 * D_MLSTM), fixed),
            pl.BlockSpec((nrow, 1), fixed),
        ],
        out_specs=[
            pl.BlockSpec((None, seq, D_MLSTM), per_b),
            pl.BlockSpec((None, seq, D_MLSTM), per_b),
            pl.BlockSpec((None, nc, D_MLSTM, MLSTM_CHUNK), lambda i: (i, 0, 0, 0)),
            pl.BlockSpec((None, seq, D_MLSTM), per_b),
            pl.BlockSpec((None, nrow, seq), per_b),
        ],
        out_shape=[
            jax.ShapeDtypeStruct((b, seq, D_MLSTM), F32),
            jax.ShapeDtypeStruct((b, seq, D_MLSTM), BF16),
            jax.ShapeDtypeStruct((b, nc, D_MLSTM, MLSTM_CHUNK), BF16),
            jax.ShapeDtypeStruct((b, seq, D_MLSTM), BF16),
            jax.ShapeDtypeStruct((b, nrow, seq), F32),
        ],
        compiler_params=_params(("parallel",)),
        name="mlstm_pre",
    )(xm, conv_w, conv_b, wq, wkt, wv, wif_rows, bif_rows)


def _log_sigmoid(x):
    return jnp.minimum(x, 0.0) - jnp.log1p(jnp.exp(-jnp.abs(x)))


def _mlstm_kernel(q_ref, kt_ref, v_ref, g_ref, xc_ref, z_ref, ng_ref, sk_ref, o_ref,
                  cum_ref, w_ref, ml_ref, tot_ref, dc_ref, cs_ref, ms_ref):
    L = MLSTM_CHUNK
    dh = MLSTM_HEAD_DIM
    nc = kt_ref.shape[0]
    rows_all = g_ref.shape[1]
    lane = lax.broadcasted_iota(jnp.int32, (rows_all, L), 1)
    fwd_row = lax.broadcasted_iota(jnp.int32, (rows_all, L), 0) % 8 == 0
    row_i = lax.broadcasted_iota(jnp.int32, (L, L), 0)
    col_i = lax.broadcasted_iota(jnp.int32, (L, L), 1)
    ones_col = jnp.where(lax.broadcasted_iota(jnp.int32, (L, dh), 1) == 0, 1.0, 0.0).astype(BF16)

    lf = _log_sigmoid(g_ref[0])
    pre, suf = lf, lf
    d = 1
    while d < L:
        pre = pre + jnp.where(lane >= d, pltpu.roll(pre, d, axis=1), 0.0)
        suf = suf + jnp.where(lane < L - d, pltpu.roll(suf, L - d, axis=1), 0.0)
        d *= 2
    cum = jnp.where(fwd_row, pre, suf)
    tot = jnp.where(fwd_row, cum[:, L - 1:L], cum[:, 0:1])
    a = tot - cum + g_ref[1]
    ml = jnp.max(a, axis=1, keepdims=True)
    cum_ref[...] = cum
    w_ref[...] = jnp.exp(a - ml)
    ml_ref[...] = jnp.broadcast_to(ml, (rows_all, L))
    tot_ref[...] = tot

    def chunk_rows(ref, c):
        return ref[pl.ds(pl.multiple_of(c * 8, 8), 8), :]

    def v_aug(c):
        return jnp.concatenate([v_ref[pl.ds(c * L, L), :], ones_col], axis=1)

    def phase_a(c, carry):
        kt = kt_ref[c].astype(F32)
        w = chunk_rows(w_ref, c)
        kw = jnp.concatenate([kt * w[0:1], kt * w[1:2]], axis=0)
        dc_ref[c] = _dot(kw.astype(BF16), v_aug(c))
        return carry

    lax.fori_loop(0, nc, phase_a, 0, unroll=2)

    def scan_dir(direction_row, reverse):
        off_rows = direction_row * dh
        off_cols = direction_row * 2 * dh

        def body(i, carry):
            c = (nc - 1 - i) if reverse else i
            state, m = carry
            cs_ref[c, :, off_cols:off_cols + 2 * dh] = state.astype(BF16)
            ms_ref[c, direction_row:direction_row + 1, :] = m
            ml_c = chunk_rows(ml_ref, c)[direction_row:direction_row + 1]
            g_c = chunk_rows(tot_ref, c)[direction_row:direction_row + 1]
            m_new = jnp.maximum(g_c + m, ml_c)
            alpha = jnp.exp(g_c + m - m_new)
            beta = jnp.exp(ml_c - m_new)
            alpha2 = jnp.concatenate([alpha, alpha], axis=1)
            beta2 = jnp.concatenate([beta, beta], axis=1)
            state = alpha2 * state + beta2 * dc_ref[c, off_rows:off_rows + dh, :]
            return state, m_new

        init = (jnp.zeros((dh, 2 * dh), F32), jnp.zeros((1, L), F32))
        lax.fori_loop(0, nc, body, init)

    scan_dir(0, False)
    scan_dir(1, True)

    ng = ng_ref[...]
    sk = sk_ref[...]

    def to_col(r):
        return jnp.sum(jnp.where(row_i == col_i, jnp.broadcast_to(r, (L, L)), 0.0),
                       axis=1, keepdims=True)

    def direction(s_qk, qc, vaug, i_r, b_r, m_prev, keep):
        b_c = to_col(b_r)
        log_d = jnp.where(keep, b_c - b_r + i_r, NEG_INF)
        inter = b_c + m_prev
        m_t = jnp.maximum(inter, jnp.max(log_d, axis=1, keepdims=True))
        w = jnp.exp(log_d - m_t) * s_qk
        scale = jnp.exp(inter - m_t)
        intra = _dot(w.astype(BF16), vaug)
        tot_c = intra + scale * qc
        den = tot_c[:, dh:dh + 1]
        return tot_c[:, :dh] / jnp.maximum(jnp.abs(den), jnp.exp(-m_t))

    def phase_c(c, carry):
        rows = pl.ds(c * L, L)
        q = q_ref[rows, :]
        s_qk = _dot(q, kt_ref[c])
        qc = _dot(q, cs_ref[c])
        vaug = v_aug(c)
        ms = ms_ref[c]
        b = chunk_rows(cum_ref, c)
        gi = chunk_rows(g_ref.at[1], c)
        h = (direction(s_qk, qc[:, :2 * dh], vaug, gi[0:1], b[0:1], ms[0:1, 0:1], col_i <= row_i)
             + direction(s_qk, qc[:, 2 * dh:], vaug, gi[1:2], b[1:2], ms[1:2, 0:1], col_i >= row_i))
        hn = _rms(h, ng)
        o_ref[rows, :] = (hn + sk * xc_ref[rows, :]) * _silu(z_ref[rows, :])
        return carry

    lax.fori_loop(0, nc, phase_c, 0, unroll=2)


def _mlstm(q, kt, v, gates, xc, z, norm_g, skip):
    b, seq, _ = q.shape
    nc = seq // MLSTM_CHUNK
    dh = MLSTM_HEAD_DIM
    head = lambda i, j: (i, 0, j)
    vec = lambda i, j: (0, j)
    blk = pl.BlockSpec((None, seq, dh), head)
    gate_rows = pltpu.VMEM((nc * 8, MLSTM_CHUNK), F32)
    return pl.pallas_call(
        _mlstm_kernel,
        grid=(b, MLSTM_HEADS),
        in_specs=[
            blk,
            pl.BlockSpec((None, nc, dh, MLSTM_CHUNK), lambda i, j: (i, 0, j, 0)),
            blk,
            pl.BlockSpec((None, None, 2, nc * 8, MLSTM_CHUNK), lambda i, j: (i, j, 0, 0, 0)),
            blk, blk,
            pl.BlockSpec((1, dh), vec),
            pl.BlockSpec((1, dh), vec),
        ],
        out_specs=blk,
        out_shape=jax.ShapeDtypeStruct((b, seq, D_MLSTM), F32),
        scratch_shapes=[
            gate_rows, gate_rows, gate_rows, gate_rows,
            pltpu.VMEM((nc, 2 * dh, 2 * dh), F32),
            pltpu.VMEM((nc, dh, 4 * dh), BF16),
            pltpu.VMEM((nc, 8, MLSTM_CHUNK), F32),
        ],
        compiler_params=_params(("parallel", "parallel")),
        name="mlstm",
    )(q, kt, v, gates, xc, z, norm_g, skip)


def _band_attn_kernel(q_ref, k_ref, v_ref, o_ref, lse_ref, *, half):
    lsub, width = q_ref.shape
    tq = ATTN_Q_TILE
    win = min(lsub, 2 * tq)
    first = lax.broadcasted_iota(jnp.int32, (1, LANES), 1) < ATTN_HEAD_DIM
    lane = lax.broadcasted_iota(jnp.int32, (tq, LANES), 1)
    rel = (lax.broadcasted_iota(jnp.int32, (2 * tq, win), 1)
           - lax.broadcasted_iota(jnp.int32, (2 * tq, win), 0) % tq)

    def tile(t, carry):
        qs = pl.multiple_of(t * tq, tq)
        ws = pl.multiple_of(jnp.clip(qs - half, 0, lsub - win), half)
        keep = jnp.abs(rel + (ws - qs)) <= half
        for c in range(width // D_ATTN):
            lse_tile = jnp.zeros((tq, LANES), F32)
            for p in range(D_ATTN // LANES):
                lanes = slice(c * D_ATTN + p * LANES, c * D_ATTN + (p + 1) * LANES)
                q = q_ref[pl.ds(qs, tq), lanes]
                kw = k_ref[pl.ds(ws, win), lanes]
                vw = v_ref[pl.ds(ws, win), lanes]
                zero = jnp.zeros_like(q)
                q2 = jnp.concatenate([jnp.where(first, q, zero), jnp.where(first, zero, q)], axis=0)
                s = jnp.where(keep, _dot_nt(q2, kw), NEG_INF)
                m = jnp.max(s, axis=1, keepdims=True)
                e = jnp.exp(s - m)
                l = jnp.sum(e, axis=1, keepdims=True)
                o2 = _dot(e.astype(BF16), vw) / l
                o_ref[pl.ds(qs, tq), lanes] = jnp.where(first, o2[:tq], o2[tq:])
                lse2 = m + jnp.log(l)
                lse_tile = jnp.where(lane == 2 * p, lse2[:tq],
                                     jnp.where(lane == 2 * p + 1, lse2[tq:], lse_tile))
            lse_ref[pl.ds(qs, tq), c * LANES:(c + 1) * LANES] = lse_tile
        return carry

    lax.fori_loop(0, lsub // tq, tile, 0)


def _band_attn(q, k, v, win, dil):
    b, seq, _ = q.shape
    lsub = seq // dil
    half = win // (2 * dil)
    assert lsub % ATTN_Q_TILE == 0 and half % 16 == 0 and ATTN_Q_TILE + 2 * half <= 2 * ATTN_Q_TILE
    classes = min(dil, ATTN_CLASSES_PER_STEP)
    width = classes * D_ATTN
    view = lambda t: t.reshape(b, lsub, dil * D_ATTN)
    blk = pl.BlockSpec((None, lsub, width), lambda i, g: (i, 0, g))
    o, lse = pl.pallas_call(
        functools.partial(_band_attn_kernel, half=half),
        grid=(b, dil // classes),
        in_specs=[blk, blk, blk],
        out_specs=[blk, pl.BlockSpec((None, lsub, classes * LANES), lambda i, g: (i, 0, g))],
        out_shape=[
            jax.ShapeDtypeStruct((b, lsub, dil * D_ATTN), F32),
            jax.ShapeDtypeStruct((b, lsub, dil * LANES), F32),
        ],
        compiler_params=_params(("parallel", "parallel")),
        name=f"band_attn_d{dil}",
    )(view(q), view(k), view(v))
    return o.reshape(b * seq, D_ATTN), lse.reshape(b * seq, LANES)


def _out_proj_kernel(ym_ref, o1_ref, o2_ref, o3_ref, l1_ref, l2_ref, l3_ref, sp_ref, x_ref, ag_ref,
                     w_ref, n2_ref, wr_ref, x2_ref, h2_ref, lg_ref):
    lses = [r[...] for r in (l1_ref, l2_ref, l3_ref)]
    top = jnp.maximum(jnp.maximum(lses[0], lses[1]), lses[2])
    wts = [jnp.exp(l - top) for l in lses]
    total = wts[0] + wts[1] + wts[2]
    spread = sp_ref[...]

    def per_lane(w):
        hi = w.astype(BF16)
        lo = (w - hi.astype(F32)).astype(BF16)
        return _dot(jnp.concatenate([hi, lo], axis=1), spread)

    ya = sum(per_lane(w / total) * o[...] for w, o in zip(wts, (o1_ref, o2_ref, o3_ref)))
    ya = _rms(ya, ag_ref[...])
    mixed = jnp.concatenate([ym_ref[...], ya], axis=1).astype(BF16)
    x2 = x_ref[...] + _dot(mixed, w_ref[...])
    x2_ref[...] = x2
    h2 = _rms(x2, n2_ref[...])
    hi = h2.astype(BF16)
    h2_ref[...] = hi
    lo = (h2 - hi.astype(F32)).astype(BF16)
    wr = wr_ref[...]
    w_hi = wr.astype(BF16)
    w_lo = (wr - w_hi.astype(F32)).astype(BF16)
    lg_ref[...] = _dot(jnp.concatenate([hi, lo, hi], axis=1),
                       jnp.concatenate([w_hi, w_hi, w_lo], axis=0))


def _out_proj(ym, branch_o, branch_lse, x2d, attn_g, w_bf, n2g, wr_pad):
    n = x2d.shape[0]
    spread = (jnp.arange(LANES)[:, None] == jnp.arange(D_ATTN)[None, :] // ATTN_HEAD_DIM)
    spread = jnp.tile(spread.astype(BF16), (2, 1))
    row = lambda i: (i, 0)
    fixed = lambda i: (0, 0)
    return pl.pallas_call(
        _out_proj_kernel,
        grid=(n // ROW_TILE,),
        in_specs=[
            pl.BlockSpec((ROW_TILE, D_MLSTM), row),
            pl.BlockSpec((ROW_TILE, D_ATTN), row),
            pl.BlockSpec((ROW_TILE, D_ATTN), row),
            pl.BlockSpec((ROW_TILE, D_ATTN), row),
            pl.BlockSpec((ROW_TILE, LANES), row),
            pl.BlockSpec((ROW_TILE, LANES), row),
            pl.BlockSpec((ROW_TILE, LANES), row),
            pl.BlockSpec((2 * LANES, D_ATTN), fixed),
            pl.BlockSpec((ROW_TILE, D_MODEL), row),
            pl.BlockSpec((1, D_ATTN), fixed),
            pl.BlockSpec((D_MODEL, D_MODEL), fixed),
            pl.BlockSpec((1, D_MODEL), fixed),
            pl.BlockSpec((D_MODEL, LANES), fixed),
        ],
        out_specs=[
            pl.BlockSpec((ROW_TILE, D_MODEL), row),
            pl.BlockSpec((ROW_TILE, D_MODEL), row),
            pl.BlockSpec((ROW_TILE, LANES), row),
        ],
        out_shape=[
            jax.ShapeDtypeStruct((n, D_MODEL), F32),
            jax.ShapeDtypeStruct((n, D_MODEL), BF16),
            jax.ShapeDtypeStruct((n, LANES), F32),
        ],
        compiler_params=_params(("parallel",)),
        name="out_proj",
    )(ym, *branch_o, *branch_lse, spread, x2d, attn_g, w_bf, n2g, wr_pad)


def _route_kernel(lg_ref, tri_ref, eye_ref, slot_ref, slot_t_ref, aff_ref, *, cap):
    lg = lg_ref[...]
    valid = lax.broadcasted_iota(jnp.int32, (1, LANES), 1) < N_EXPERTS
    lg = jnp.where(valid, lg, NEG_INF)
    e = jnp.exp(lg - jnp.max(lg, axis=1, keepdims=True))
    aff = e / jnp.sum(e, axis=1, keepdims=True)
    aff_ref[...] = aff
    def enough(cand):
        return jnp.sum(jnp.where(aff >= cand, 1.0, 0.0), axis=0, keepdims=True) >= cap

    tiny = jnp.full((1, LANES), 2.0 ** -126, F32)
    normal = enough(tiny)
    p = tiny
    for bit in range(6, -1, -1):
        cand = p * (2.0 ** (2 ** bit))
        p = jnp.where(enough(cand), cand, p)
    lo = jnp.where(normal, p, 0.0)
    hi = jnp.where(normal, p * 2.0, tiny)
    step = jnp.where(normal, p * 0.5, 0.0)
    for _ in range(THRESHOLD_MANTISSA_STEPS):
        cand = lo + step
        ok = enough(cand)
        lo = jnp.where(ok, cand, lo)
        hi = jnp.where(ok, hi, cand)
        step = step * 0.5
    gt = jnp.where(aff >= hi, 1.0, 0.0)
    eq = jnp.where(aff >= lo, 1.0, 0.0) - gt
    need = cap - jnp.sum(gt, axis=0, keepdims=True)
    tri = tri_ref[...]

    def count_before(x):
        blk = tri.shape[0]
        run = jnp.zeros((1, LANES), F32)
        outs = []
        for j in range(x.shape[0] // blk):
            xb = x[j * blk:(j + 1) * blk]
            outs.append(_dot(tri, xb.astype(BF16)) + run)
            run = run + jnp.sum(xb, axis=0, keepdims=True)
        return jnp.concatenate(outs, axis=0)

    sel = gt + eq * jnp.where(count_before(eq) < need, 1.0, 0.0)
    pos = count_before(sel)
    slot = jnp.where(valid & (sel > 0.0), pos, -1.0)
    slot_ref[...] = slot
    slot_t_ref[...] = _dot_nt(eye_ref[...], slot.astype(BF16))


def _route(logits, cap):
    b, seq, _ = logits.shape
    tri = (jnp.arange(ROUTE_BLOCK)[None, :] < jnp.arange(ROUTE_BLOCK)[:, None]).astype(BF16)
    eye = jnp.eye(LANES, dtype=BF16)
    assert seq % ROUTE_BLOCK == 0
    per_b = lambda i: (i, 0, 0)
    fixed = lambda i: (0, 0)
    return pl.pallas_call(
        functools.partial(_route_kernel, cap=cap),
        grid=(b,),
        in_specs=[
            pl.BlockSpec((None, seq, LANES), per_b),
            pl.BlockSpec((ROUTE_BLOCK, ROUTE_BLOCK), fixed),
            pl.BlockSpec((LANES, LANES), fixed),
        ],
        out_specs=[
            pl.BlockSpec((None, seq, LANES), per_b),
            pl.BlockSpec((None, LANES, seq), per_b),
            pl.BlockSpec((None, seq, LANES), per_b),
        ],
        out_shape=[
            jax.ShapeDtypeStruct((b, seq, LANES), F32),
            jax.ShapeDtypeStruct((b, LANES, seq), F32),
            jax.ShapeDtypeStruct((b, seq, LANES), F32),
        ],
        compiler_params=_params(("parallel",)),
        name="route",
    )(logits, tri, eye)


def _moe_gather_kernel(slot_ref, h_ref, xs_ref):
    srow = slot_ref[...]
    cap, seq = xs_ref.shape[0], srow.shape[1]
    ci = lax.broadcasted_iota(jnp.int32, (cap, seq), 0).astype(F32)
    onehot = jnp.where(srow == ci, 1.0, 0.0).astype(BF16)
    xs_ref[...] = _dot(onehot, h_ref[...]).astype(BF16)


def _moe_gather(slot_t, h2, cap):
    b, seq, _ = h2.shape
    return pl.pallas_call(
        _moe_gather_kernel,
        grid=(b, N_EXPERTS),
        in_specs=[
            pl.BlockSpec((None, None, 1, seq), lambda i, e: (i, e, 0, 0)),
            pl.BlockSpec((None, seq, D_MODEL), lambda i, e: (i, 0, 0)),
        ],
        out_specs=pl.BlockSpec((None, None, cap, D_MODEL), lambda i, e: (i, e, 0, 0)),
        out_shape=jax.ShapeDtypeStruct((b, N_EXPERTS, cap, D_MODEL), BF16),
        compiler_params=_params(("parallel", "parallel")),
        name="moe_gather",
    )(slot_t, h2)


def _moe_ffn_kernel(xs_ref, w1_ref, w3_ref, w2_ref, y_ref, act_ref, w1b_ref, w3b_ref, w2b_ref):
    s = pl.program_id(1)
    nb, cap, _ = xs_ref.shape
    nf = act_ref.shape[0]
    per = FFN_ROW_TILE // cap
    row_tiles = nb // per

    @pl.when(s < nf)
    def _():
        w1b_ref[...] = w1_ref[...].astype(BF16)
        w3b_ref[...] = w3_ref[...].astype(BF16)
        for r in range(row_tiles):
            x = xs_ref[r * per:(r + 1) * per].reshape(FFN_ROW_TILE, D_MODEL)
            up = _dot(x, w1b_ref[...])
            gt = _dot(x, w3b_ref[...])
            act_ref[s, r * FFN_ROW_TILE:(r + 1) * FFN_ROW_TILE, :] = (_silu(up) * gt).astype(BF16)

    @pl.when(s >= nf)
    def _():
        w2b_ref[...] = w2_ref[...].astype(BF16)
        for r in range(row_tiles):
            rows = slice(r * FFN_ROW_TILE, (r + 1) * FFN_ROW_TILE)
            act = jnp.concatenate([act_ref[f, rows, :] for f in range(nf)], axis=1)
            y = _dot(act, w2b_ref[...])
            y_ref[r * per:(r + 1) * per] = y.astype(BF16).reshape(per, cap, y.shape[1])


def _moe_ffn(xs, w1, w3, w2):
    b, ne, cap, _ = xs.shape
    nf = D_EXPERT // FFN_F_TILE
    nn = D_MODEL // FFN_N_TILE
    hidden = lambda e, s: (e, 0, jnp.minimum(s, nf - 1))
    out_col = lambda e, s: jnp.maximum(s - nf, 0)
    return pl.pallas_call(
        _moe_ffn_kernel,
        grid=(ne, nf + nn),
        in_specs=[
            pl.BlockSpec((b, None, cap, D_MODEL), lambda e, s: (0, e, 0, 0)),
            pl.BlockSpec((None, D_MODEL, FFN_F_TILE), hidden),
            pl.BlockSpec((None, D_MODEL, FFN_F_TILE), hidden),
            pl.BlockSpec((None, D_EXPERT, FFN_N_TILE), lambda e, s: (e, 0, out_col(e, s))),
        ],
        out_specs=pl.BlockSpec((b, None, cap, FFN_N_TILE), lambda e, s: (0, e, 0, out_col(e, s))),
        out_shape=jax.ShapeDtypeStruct(xs.shape, BF16),
        scratch_shapes=[
            pltpu.VMEM((nf, b * cap, FFN_F_TILE), BF16),
            pltpu.VMEM((D_MODEL, FFN_F_TILE), BF16),
            pltpu.VMEM((D_MODEL, FFN_F_TILE), BF16),
            pltpu.VMEM((D_EXPERT, FFN_N_TILE), BF16),
        ],
        compiler_params=_params(("parallel", "arbitrary")),
        name="moe_ffn",
    )(xs, w1, w3, w2)


def _moe_scatter_kernel(slot_ref, aff_ref, y_ref, x2_ref, g_ref, o_ref):
    slot = slot_ref[...]
    aff = aff_ref[...]
    rows, cap = slot.shape[0], y_ref.shape[1]
    ci = lax.broadcasted_iota(jnp.int32, (rows, cap), 1).astype(F32)
    acc = x2_ref[...]
    for e in range(N_EXPERTS):
        onehot = jnp.where(slot[:, e:e + 1] == ci, 1.0, 0.0).astype(BF16)
        acc = acc + aff[:, e:e + 1] * _dot(onehot, y_ref[e])
    o_ref[...] = _rms(acc, g_ref[...])


def _moe_scatter(slot, aff, y, x2, norm_g):
    b, seq, _ = x2.shape
    cap = y.shape[2]
    tile = lambda i, r: (i, r, 0)
    return pl.pallas_call(
        _moe_scatter_kernel,
        grid=(b, seq // ROW_TILE),
        in_specs=[
            pl.BlockSpec((None, ROW_TILE, LANES), tile),
            pl.BlockSpec((None, ROW_TILE, LANES), tile),
            pl.BlockSpec((None, N_EXPERTS, cap, D_MODEL), lambda i, r: (i, 0, 0, 0)),
            pl.BlockSpec((None, ROW_TILE, D_MODEL), tile),
            pl.BlockSpec((1, D_MODEL), lambda i, r: (0, 0)),
        ],
        out_specs=pl.BlockSpec((None, ROW_TILE, D_MODEL), tile),
        out_shape=jax.ShapeDtypeStruct((b, seq, D_MODEL), F32),
        compiler_params=_params(("parallel", "parallel")),
        name="moe_scatter",
    )(slot, aff, y, x2, norm_g)


def kernel(x, norm1_g, w_in, conv_w, conv_b, wq_m, wk_m, wv_m, w_if_fwd, b_if_fwd,
           w_if_bwd, b_if_bwd, mlstm_norm_g, mlstm_skip, attn_norm_g, w_out, norm2_g,
           w_router, w1, w3, w2, norm_f_g):
    b, seq, _ = x.shape
    assert w_in.shape[0] == 1, "single-layer problem"
    assert seq % ROW_TILE == 0 and seq % MLSTM_CHUNK == 0 and seq % ATTN_Q_TILE == 0
    cap = EC_CAPACITY * seq // N_EXPERTS
    assert FFN_ROW_TILE % cap == 0 and (b * cap) % FFN_ROW_TILE == 0
    nc = seq // MLSTM_CHUNK
    l = 0
    x2d = x.reshape(b * seq, D_MODEL)
    xm, z, qa, ka, va = _in_proj(x2d, norm1_g[l][None, :], w_in[l].astype(BF16), seq)
    shp = lambda t: t.reshape(b, seq, t.shape[-1])
    wif_rows, bif_rows = _gate_rows(w_if_fwd[l], b_if_fwd[l], w_if_bwd[l], b_if_bwd[l])
    xc, qm, ktm, vm, gates = _mlstm_pre(
        shp(xm), conv_w[l], conv_b[l][None, :],
        _block_diag(wq_m[l]).astype(BF16), _block_diag(wk_m[l]).T.astype(BF16),
        _block_diag(wv_m[l]).astype(BF16), wif_rows.astype(BF16), bif_rows)
    gates = gates.reshape(b, 2, MLSTM_HEADS, 8, nc, MLSTM_CHUNK).transpose(0, 2, 1, 4, 3, 5)
    gates = gates.reshape(b, MLSTM_HEADS, 2, nc * 8, MLSTM_CHUNK)
    ym = _mlstm(qm, ktm, vm, gates, xc, shp(z), mlstm_norm_g[l][None, :],
                mlstm_skip[l][None, :])
    branches = [_band_attn(shp(qa), shp(ka), shp(va), win, dil) for win, dil in DILATED_PATTERNS]
    wr_pad = jnp.pad(w_router[l], ((0, 0), (0, LANES - N_EXPERTS)))
    x2, h2, logits = _out_proj(
        ym.reshape(b * seq, D_MLSTM), [o for o, _ in branches], [s for _, s in branches], x2d,
        attn_norm_g[l][None, :], w_out[l].astype(BF16), norm2_g[l][None, :], wr_pad)
    slot, slot_t, aff = _route(logits.reshape(b, seq, LANES), cap)
    xs = _moe_gather(slot_t.reshape(b, LANES, 1, seq), h2.reshape(b, seq, D_MODEL), cap)
    y = _moe_ffn(xs, w1[l], w3[l], w2[l])
    return _moe_scatter(slot, aff, y, x2.reshape(b, seq, D_MODEL), norm_f_g[None, :])
```

```python
import functools

import jax
import jax.numpy as jnp
from jax import lax
from jax.experimental import pallas as pl
from jax.experimental.pallas import tpu as pltpu

F32 = jnp.float32
BF16 = jnp.bfloat16

D_MODEL = 1024
D_MLSTM = 512
D_ATTN = 512
D_IN_PROJ = 2 * D_MLSTM + 3 * D_ATTN
MLSTM_HEADS = 4
MLSTM_HEAD_DIM = 128
MLSTM_QKV_BLOCK = 4
MLSTM_CONV = 5
ATTN_HEADS = 8
ATTN_HEAD_DIM = 64
ROPE_DIM = 16
ROPE_THETA = 500000.0
DILATED_PATTERNS = ((128, 1), (512, 4), (2048, 16))
N_EXPERTS = 16
EC_CAPACITY = 2
D_EXPERT = 2816
NORM_EPS = 1e-6
NEG_INF = -1e30

LANES = 128
MLSTM_CHUNK = 128
ROW_TILE = 512
ATTN_Q_TILE = 128
ROUTE_BLOCK = 256
FFN_F_TILE = 256
FFN_N_TILE = 256
FFN_ROW_TILE = 512
THRESHOLD_MANTISSA_STEPS = 40
VMEM_LIMIT = 56 * 1024 * 1024


def _params(sem):
    return pltpu.CompilerParams(dimension_semantics=sem, vmem_limit_bytes=VMEM_LIMIT)


def _rms(x, g):
    return x * lax.rsqrt(jnp.mean(x * x, axis=-1, keepdims=True) + NORM_EPS) * g


def _silu(x):
    return x * (1.0 / (1.0 + jnp.exp(-x)))


def _dot(a, b):
    return jnp.dot(a, b, preferred_element_type=F32)


def _dot_nt(a, b):
    return lax.dot_general(a, b, (((1,), (1,)), ((), ())), preferred_element_type=F32)


def _in_proj_kernel(x_ref, g_ref, w_ref, cos_ref, sa_ref, sb_ref, xm_ref, z_ref, *rest):
    qkv_refs, scr_ref = rest[:-1], rest[-1]
    h = _rms(x_ref[...], g_ref[...])
    p = _dot(h.astype(BF16), w_ref[...])
    xm_ref[...] = p[:, :D_MLSTM]
    z_ref[...] = p[:, D_MLSTM:2 * D_MLSTM]
    cos, sa, sb = cos_ref[...], sa_ref[...], sb_ref[...]
    half = ROPE_DIM // 2

    def rope(t):
        outs = []
        for j in range(D_ATTN // LANES):
            tj = t[:, j * LANES:(j + 1) * LANES]
            up = pltpu.roll(tj, LANES - half, axis=1)
            dn = pltpu.roll(tj, half, axis=1)
            outs.append(tj * cos + up * sa + dn * sb)
        return jnp.concatenate(outs, axis=1)

    o = 2 * D_MLSTM
    qkv = (rope(p[:, o:o + D_ATTN]) * (ATTN_HEAD_DIM ** -0.5),
           rope(p[:, o + D_ATTN:o + 2 * D_ATTN]),
           p[:, o + 2 * D_ATTN:])
    rows = p.shape[0]
    groups = D_ATTN // LANES
    for a, val in enumerate(qkv):
        for j in range(groups):
            scr_ref[j] = val[:, j * LANES:(j + 1) * LANES]
        for d, (_, dil) in enumerate(DILATED_PATTERNS):
            ref = qkv_refs[3 * d + a]
            if dil == 1:
                ref[...] = val.astype(BF16)
                continue
            for r in range(dil):
                for j in range(groups):
                    piece = scr_ref[j, pl.ds(r, rows // dil, stride=dil), :]
                    ref[:, r * D_ATTN + j * LANES:r * D_ATTN + (j + 1) * LANES] = piece.astype(BF16)


def _rope_tables(seq):
    half = ROPE_DIM // 2
    inv_freq = ROPE_THETA ** (-2.0 * jnp.arange(half, dtype=F32) / ROPE_DIM)
    ang = jnp.arange(seq).astype(F32)[:, None] * inv_freq[None, :]
    cos, sin = jnp.cos(ang), jnp.sin(ang)
    pad = jnp.zeros((seq, ATTN_HEAD_DIM - ROPE_DIM), F32)
    cos_h = jnp.concatenate([cos, cos, pad + 1.0], axis=1)
    sa_h = jnp.concatenate([-sin, jnp.zeros_like(sin), pad], axis=1)
    sb_h = jnp.concatenate([jnp.zeros_like(sin), sin, pad], axis=1)
    rep = LANES // ATTN_HEAD_DIM
    return tuple(jnp.tile(t, (1, rep)) for t in (cos_h, sa_h, sb_h))


def _in_proj(x2d, g, w_bf, seq):
    n = x2d.shape[0]
    tiles_per_seq = seq // ROW_TILE
    cos, sa, sb = _rope_tables(seq)
    row = lambda i: (i, 0)
    fixed = lambda i: (0, 0)
    pos = lambda i: (i % tiles_per_seq, 0)
    return pl.pallas_call(
        _in_proj_kernel,
        grid=(n // ROW_TILE,),
        in_specs=[
            pl.BlockSpec((ROW_TILE, D_MODEL), row),
            pl.BlockSpec((1, D_MODEL), fixed),
            pl.BlockSpec((D_MODEL, D_IN_PROJ), fixed),
            pl.BlockSpec((ROW_TILE, LANES), pos),
            pl.BlockSpec((ROW_TILE, LANES), pos),
            pl.BlockSpec((ROW_TILE, LANES), pos),
        ],
        out_specs=[
            pl.BlockSpec((ROW_TILE, D_MLSTM), row),
            pl.BlockSpec((ROW_TILE, D_MLSTM), row),
        ] + [pl.BlockSpec((ROW_TILE // dil, dil * D_ATTN), row)
             for _, dil in DILATED_PATTERNS for _ in range(3)],
        out_shape=[
            jax.ShapeDtypeStruct((n, D_MLSTM), F32),
            jax.ShapeDtypeStruct((n, D_MLSTM), F32),
        ] + [jax.ShapeDtypeStruct((n // dil, dil * D_ATTN), BF16)
             for _, dil in DILATED_PATTERNS for _ in range(3)],
        scratch_shapes=[pltpu.VMEM((D_ATTN // LANES, ROW_TILE, LANES), F32)],
        compiler_params=_params(("parallel",)),
        name="in_proj",
    )(x2d, g, w_bf, cos, sa, sb)


def _mlstm_pre_kernel(xm_ref, cw_ref, cb_ref, wq_ref, wkt_ref, wv_ref, wif_ref, bif_ref,
                      xc_ref, q_ref, kt_ref, v_ref, g_ref):
    x = xm_ref[...]
    seq = x.shape[0]
    t = lax.broadcasted_iota(jnp.int32, x.shape, 0)
    acc = jnp.zeros_like(x) + cb_ref[...]
    for j in range(MLSTM_CONV):
        d = j - MLSTM_CONV // 2
        if d == 0:
            tap = x
        else:
            tap = pltpu.roll(x, (-d) % seq, axis=0)
            tap = jnp.where((t + d >= 0) & (t + d < seq), tap, 0.0)
        acc = acc + tap * cw_ref[j:j + 1, :]
    xc = _silu(acc)
    xc_ref[...] = xc
    xcb = xc.astype(BF16)
    q = _dot(xcb, wq_ref[...]).astype(BF16)
    kt = (_dot_nt(wkt_ref[...], xcb) * (MLSTM_HEAD_DIM ** -0.5)).astype(BF16)
    v = _dot(x.astype(BF16), wv_ref[...]).astype(BF16)
    q_ref[...] = q
    v_ref[...] = v
    L = MLSTM_CHUNK
    for c in range(seq // L):
        kt_ref[c] = kt[:, c * L:(c + 1) * L]
    wif = wif_ref[...]
    g_ref[...] = (_dot_nt(wif[:, :D_MLSTM], q) + _dot(wif[:, D_MLSTM:2 * D_MLSTM], kt)
                  + _dot_nt(wif[:, 2 * D_MLSTM:], v) + bif_ref[...])


def _block_diag(w):
    nblk = w.shape[0]
    n = nblk * MLSTM_QKV_BLOCK
    tiled = jnp.tile(w.reshape(n, MLSTM_QKV_BLOCK), (1, nblk))
    blk = jnp.arange(n) // MLSTM_QKV_BLOCK
    return jnp.where(blk[:, None] == blk[None, :], tiled, 0.0)


def _gate_rows(w_f, b_f, w_b, b_b):
    h = MLSTM_HEADS
    cols = []
    bias = []
    zero_w = jnp.zeros((w_f.shape[0],), F32)
    for off in (h, 0):
        for hd in range(h):
            cols += [w_f[:, off + hd], w_b[:, off + hd]] + [zero_w] * 6
            bias += [b_f[off + hd], b_b[off + hd]] + [jnp.zeros((), F32)] * 6
    return jnp.stack(cols, axis=0), jnp.stack(bias)[:, None]


def _mlstm_pre(xm, conv_w, conv_b, wq, wkt, wv, wif_rows, bif_rows):
    b, seq, _ = xm.shape
    nrow = wif_rows.shape[0]
    nc = seq // MLSTM_CHUNK
    per_b = lambda i: (i, 0, 0)
    fixed = lambda i: (0, 0)
    return pl.pallas_call(
        _mlstm_pre_kernel,
        grid=(b,),
        in_specs=[
            pl.BlockSpec((None, seq, D_MLSTM), per_b),
            pl.BlockSpec((MLSTM_CONV, D_MLSTM), fixed),
            pl.BlockSpec((1, D_MLSTM), fixed),
            pl.BlockSpec((D_MLSTM, D_MLSTM), fixed),
            pl.BlockSpec((D_MLSTM, D_MLSTM), fixed),
            pl.BlockSpec((D_MLSTM, D_MLSTM), fixed),
            pl.BlockSpec((nrow, 3 * D_MLSTM), fixed),
            pl.BlockSpec((nrow, 1), fixed),
        ],
        out_specs=[
            pl.BlockSpec((None, seq, D_MLSTM), per_b),
            pl.BlockSpec((None, seq, D_MLSTM), per_b),
            pl.BlockSpec((None, nc, D_MLSTM, MLSTM_CHUNK), lambda i: (i, 0, 0, 0)),
            pl.BlockSpec((None, seq, D_MLSTM), per_b),
            pl.BlockSpec((None, nrow, seq), per_b),
        ],
        out_shape=[
            jax.ShapeDtypeStruct((b, seq, D_MLSTM), F32),
            jax.ShapeDtypeStruct((b, seq, D_MLSTM), BF16),
            jax.ShapeDtypeStruct((b, nc, D_MLSTM, MLSTM_CHUNK), BF16),
            jax.ShapeDtypeStruct((b, seq, D_MLSTM), BF16),
            jax.ShapeDtypeStruct((b, nrow, seq), F32),
        ],
        compiler_params=_params(("parallel",)),
        name="mlstm_pre",
    )(xm, conv_w, conv_b, wq, wkt, wv, wif_rows, bif_rows)


def _log_sigmoid(x):
    return jnp.minimum(x, 0.0) - jnp.log1p(jnp.exp(-jnp.abs(x)))


def _mlstm_kernel(q_ref, kt_ref, v_ref, g_ref, xc_ref, z_ref, ng_ref, sk_ref, o_ref,
                  cum_ref, w_ref, ml_ref, tot_ref, dc_ref, cs_ref, ms_ref):
    L = MLSTM_CHUNK
    dh = MLSTM_HEAD_DIM
    nc = kt_ref.shape[0]
    rows_all = g_ref.shape[1]
    lane = lax.broadcasted_iota(jnp.int32, (rows_all, L), 1)
    fwd_row = lax.broadcasted_iota(jnp.int32, (rows_all, L), 0) % 8 == 0
    row_i = lax.broadcasted_iota(jnp.int32, (L, L), 0)
    col_i = lax.broadcasted_iota(jnp.int32, (L, L), 1)
    ones_col = jnp.where(lax.broadcasted_iota(jnp.int32, (L, dh), 1) == 0, 1.0, 0.0).astype(BF16)

    lf = _log_sigmoid(g_ref[0])
    pre, suf = lf, lf
    d = 1
    while d < L:
        pre = pre + jnp.where(lane >= d, pltpu.roll(pre, d, axis=1), 0.0)
        suf = suf + jnp.where(lane < L - d, pltpu.roll(suf, L - d, axis=1), 0.0)
        d *= 2
    cum = jnp.where(fwd_row, pre, suf)
    tot = jnp.where(fwd_row, cum[:, L - 1:L], cum[:, 0:1])
    a = tot - cum + g_ref[1]
    ml = jnp.max(a, axis=1, keepdims=True)
    cum_ref[...] = cum
    w_ref[...] = jnp.exp(a - ml)
    ml_ref[...] = jnp.broadcast_to(ml, (rows_all, L))
    tot_ref[...] = tot

    def chunk_rows(ref, c):
        return ref[pl.ds(pl.multiple_of(c * 8, 8), 8), :]

    def v_aug(c):
        return jnp.concatenate([v_ref[pl.ds(c * L, L), :], ones_col], axis=1)

    def phase_a(c, carry):
        kt = kt_ref[c].astype(F32)
        w = chunk_rows(w_ref, c)
        kw = jnp.concatenate([kt * w[0:1], kt * w[1:2]], axis=0)
        dc_ref[c] = _dot(kw.astype(BF16), v_aug(c))
        return carry

    lax.fori_loop(0, nc, phase_a, 0, unroll=2)

    def scan_dir(direction_row, reverse):
        off_rows = direction_row * dh
        off_cols = direction_row * 2 * dh

        def body(i, carry):
            c = (nc - 1 - i) if reverse else i
            state, m = carry
            cs_ref[c, :, off_cols:off_cols + 2 * dh] = state.astype(BF16)
            ms_ref[c, direction_row:direction_row + 1, :] = m
            ml_c = chunk_rows(ml_ref, c)[direction_row:direction_row + 1]
            g_c = chunk_rows(tot_ref, c)[direction_row:direction_row + 1]
            m_new = jnp.maximum(g_c + m, ml_c)
            alpha = jnp.exp(g_c + m - m_new)
            beta = jnp.exp(ml_c - m_new)
            alpha2 = jnp.concatenate([alpha, alpha], axis=1)
            beta2 = jnp.concatenate([beta, beta], axis=1)
            state = alpha2 * state + beta2 * dc_ref[c, off_rows:off_rows + dh, :]
            return state, m_new

        init = (jnp.zeros((dh, 2 * dh), F32), jnp.zeros((1, L), F32))
        lax.fori_loop(0, nc, body, init)

    scan_dir(0, False)
    scan_dir(1, True)

    ng = ng_ref[...]
    sk = sk_ref[...]

    def to_col(r):
        return jnp.sum(jnp.where(row_i == col_i, jnp.broadcast_to(r, (L, L)), 0.0),
                       axis=1, keepdims=True)

    def direction(s_qk, qc, vaug, i_r, b_r, m_prev, keep):
        b_c = to_col(b_r)
        log_d = jnp.where(keep, b_c - b_r + i_r, NEG_INF)
        inter = b_c + m_prev
        m_t = jnp.maximum(inter, jnp.max(log_d, axis=1, keepdims=True))
        w = jnp.exp(log_d - m_t) * s_qk
        scale = jnp.exp(inter - m_t)
        intra = _dot(w.astype(BF16), vaug)
        tot_c = intra + scale * qc
        den = tot_c[:, dh:dh + 1]
        return tot_c[:, :dh] / jnp.maximum(jnp.abs(den), jnp.exp(-m_t))

    def phase_c(c, carry):
        rows = pl.ds(c * L, L)
        q = q_ref[rows, :]
        s_qk = _dot(q, kt_ref[c])
        qc = _dot(q, cs_ref[c])
        vaug = v_aug(c)
        ms = ms_ref[c]
        b = chunk_rows(cum_ref, c)
        gi = chunk_rows(g_ref.at[1], c)
        h = (direction(s_qk, qc[:, :2 * dh], vaug, gi[0:1], b[0:1], ms[0:1, 0:1], col_i <= row_i)
             + direction(s_qk, qc[:, 2 * dh:], vaug, gi[1:2], b[1:2], ms[1:2, 0:1], col_i >= row_i))
        hn = _rms(h, ng)
        o_ref[rows, :] = (hn + sk * xc_ref[rows, :]) * _silu(z_ref[rows, :])
        return carry

    lax.fori_loop(0, nc, phase_c, 0, unroll=2)


def _mlstm(q, kt, v, gates, xc, z, norm_g, skip):
    b, seq, _ = q.shape
    nc = seq // MLSTM_CHUNK
    dh = MLSTM_HEAD_DIM
    head = lambda i, j: (i, 0, j)
    vec = lambda i, j: (0, j)
    blk = pl.BlockSpec((None, seq, dh), head)
    gate_rows = pltpu.VMEM((nc * 8, MLSTM_CHUNK), F32)
    return pl.pallas_call(
        _mlstm_kernel,
        grid=(b, MLSTM_HEADS),
        in_specs=[
            blk,
            pl.BlockSpec((None, nc, dh, MLSTM_CHUNK), lambda i, j: (i, 0, j, 0)),
            blk,
            pl.BlockSpec((None, None, 2, nc * 8, MLSTM_CHUNK), lambda i, j: (i, j, 0, 0, 0)),
            blk, blk,
            pl.BlockSpec((1, dh), vec),
            pl.BlockSpec((1, dh), vec),
        ],
        out_specs=blk,
        out_shape=jax.ShapeDtypeStruct((b, seq, D_MLSTM), F32),
        scratch_shapes=[
            gate_rows, gate_rows, gate_rows, gate_rows,
            pltpu.VMEM((nc, 2 * dh, 2 * dh), F32),
            pltpu.VMEM((nc, dh, 4 * dh), BF16),
            pltpu.VMEM((nc, 8, MLSTM_CHUNK), F32),
        ],
        compiler_params=_params(("parallel", "parallel")),
        name="mlstm",
    )(q, kt, v, gates, xc, z, norm_g, skip)


def _band_attn_kernel(q_ref, k_ref, v_ref, o_ref, lse_ref, *, half, dil):
    lsub = q_ref.shape[0]
    tq = ATTN_Q_TILE
    win = min(lsub, 2 * tq)
    first = lax.broadcasted_iota(jnp.int32, (1, LANES), 1) < ATTN_HEAD_DIM
    lane = lax.broadcasted_iota(jnp.int32, (tq, LANES), 1)
    rel = (lax.broadcasted_iota(jnp.int32, (2 * tq, win), 1)
           - lax.broadcasted_iota(jnp.int32, (2 * tq, win), 0) % tq)

    def tile(qs, ws, out_rows):
        keep = jnp.abs(rel + (ws - qs)) <= half
        for c in range(dil):
            lse_tile = jnp.zeros((tq, LANES), F32)
            for p in range(D_ATTN // LANES):
                lanes = slice(c * D_ATTN + p * LANES, c * D_ATTN + (p + 1) * LANES)
                q = q_ref[pl.ds(qs, tq), lanes]
                kw = k_ref[pl.ds(ws, win), lanes]
                vw = v_ref[pl.ds(ws, win), lanes]
                zero = jnp.zeros_like(q)
                q2 = jnp.concatenate([jnp.where(first, q, zero), jnp.where(first, zero, q)], axis=0)
                s = jnp.where(keep, _dot_nt(q2, kw), NEG_INF)
                m = jnp.max(s, axis=1, keepdims=True)
                e = jnp.exp(s - m)
                l = jnp.sum(e, axis=1, keepdims=True)
                o2 = _dot(e.astype(BF16), vw) / l
                o_ref[p, out_rows(c), :] = jnp.where(first, o2[:tq], o2[tq:])
                lse2 = m + jnp.log(l)
                lse_tile = jnp.where(lane == 2 * p, lse2[:tq],
                                     jnp.where(lane == 2 * p + 1, lse2[tq:], lse_tile))
            lse_ref[out_rows(c), :] = lse_tile

    if dil == 1:
        def body(t, carry):
            qs = pl.multiple_of(t * tq, tq)
            ws = pl.multiple_of(jnp.clip(qs - half, 0, lsub - win), half)
            tile(qs, ws, lambda c: pl.ds(qs, tq))
            return carry

        lax.fori_loop(0, lsub // tq, body, 0)
    else:
        for t in range(lsub // tq):
            qs = t * tq
            ws = min(max(qs - half, 0), lsub - win)
            tile(qs, ws, lambda c, qs=qs: pl.ds(qs * dil + c, tq, stride=dil))


def _band_attn(q, k, v, seq, win, dil):
    lsub = seq // dil
    b = q.shape[0] // lsub
    half = win // (2 * dil)
    pairs = D_ATTN // LANES
    assert lsub % ATTN_Q_TILE == 0 and half % 16 == 0 and ATTN_Q_TILE + 2 * half <= 2 * ATTN_Q_TILE
    blk = pl.BlockSpec((lsub, dil * D_ATTN), lambda i: (i, 0))
    return pl.pallas_call(
        functools.partial(_band_attn_kernel, half=half, dil=dil),
        grid=(b,),
        in_specs=[blk, blk, blk],
        out_specs=[pl.BlockSpec((None, pairs, seq, LANES), lambda i: (i, 0, 0, 0)),
                   pl.BlockSpec((seq, LANES), lambda i: (i, 0))],
        out_shape=[
            jax.ShapeDtypeStruct((b, pairs, seq, LANES), F32),
            jax.ShapeDtypeStruct((b * seq, LANES), F32),
        ],
        compiler_params=_params(("parallel",)),
        name=f"band_attn_d{dil}",
    )(q, k, v)


def _out_proj_kernel(ym_ref, o1_ref, o2_ref, o3_ref, l1_ref, l2_ref, l3_ref, sp_ref, x_ref, ag_ref,
                     w_ref, n2_ref, wr_ref, x2_ref, h2_ref, lg_ref):
    lses = [r[...] for r in (l1_ref, l2_ref, l3_ref)]
    top = jnp.maximum(jnp.maximum(lses[0], lses[1]), lses[2])
    wts = [jnp.exp(l - top) for l in lses]
    total = wts[0] + wts[1] + wts[2]
    spread = sp_ref[...]

    def per_lane(w):
        hi = w.astype(BF16)
        lo = (w - hi.astype(F32)).astype(BF16)
        return _dot(jnp.concatenate([hi, lo], axis=1), spread)

    def heads(o_ref):
        return jnp.concatenate([o_ref[p] for p in range(o_ref.shape[0])], axis=1)

    ya = sum(per_lane(w / total) * heads(o) for w, o in zip(wts, (o1_ref, o2_ref, o3_ref)))
    ya = _rms(ya, ag_ref[...])
    mixed = jnp.concatenate([ym_ref[...], ya], axis=1).astype(BF16)
    x2 = x_ref[...] + _dot(mixed, w_ref[...])
    x2_ref[...] = x2
    h2 = _rms(x2, n2_ref[...])
    hi = h2.astype(BF16)
    h2_ref[...] = hi
    lo = (h2 - hi.astype(F32)).astype(BF16)
    wr = wr_ref[...]
    w_hi = wr.astype(BF16)
    w_lo = (wr - w_hi.astype(F32)).astype(BF16)
    lg_ref[...] = _dot(jnp.concatenate([hi, lo, hi], axis=1),
                       jnp.concatenate([w_hi, w_hi, w_lo], axis=0))


def _out_proj(ym, branch_o, branch_lse, x2d, attn_g, w_bf, n2g, wr_pad):
    n = x2d.shape[0]
    spread = (jnp.arange(LANES)[:, None] == jnp.arange(D_ATTN)[None, :] // ATTN_HEAD_DIM)
    spread = jnp.tile(spread.astype(BF16), (2, 1))
    pairs, seq = branch_o[0].shape[1:3]
    tiles_per_seq = seq // ROW_TILE
    branch = pl.BlockSpec((None, pairs, ROW_TILE, LANES),
                          lambda i: (i // tiles_per_seq, 0, i % tiles_per_seq, 0))
    row = lambda i: (i, 0)
    fixed = lambda i: (0, 0)
    return pl.pallas_call(
        _out_proj_kernel,
        grid=(n // ROW_TILE,),
        in_specs=[
            pl.BlockSpec((ROW_TILE, D_MLSTM), row),
            branch, branch, branch,
            pl.BlockSpec((ROW_TILE, LANES), row),
            pl.BlockSpec((ROW_TILE, LANES), row),
            pl.BlockSpec((ROW_TILE, LANES), row),
            pl.BlockSpec((2 * LANES, D_ATTN), fixed),
            pl.BlockSpec((ROW_TILE, D_MODEL), row),
            pl.BlockSpec((1, D_ATTN), fixed),
            pl.BlockSpec((D_MODEL, D_MODEL), fixed),
            pl.BlockSpec((1, D_MODEL), fixed),
            pl.BlockSpec((D_MODEL, LANES), fixed),
        ],
        out_specs=[
            pl.BlockSpec((ROW_TILE, D_MODEL), row),
            pl.BlockSpec((ROW_TILE, D_MODEL), row),
            pl.BlockSpec((ROW_TILE, LANES), row),
        ],
        out_shape=[
            jax.ShapeDtypeStruct((n, D_MODEL), F32),
            jax.ShapeDtypeStruct((n, D_MODEL), BF16),
            jax.ShapeDtypeStruct((n, LANES), F32),
        ],
        compiler_params=_params(("parallel",)),
        name="out_proj",
    )(ym, *branch_o, *branch_lse, spread, x2d, attn_g, w_bf, n2g, wr_pad)


def _route_kernel(lg_ref, tri_ref, eye_ref, slot_ref, slot_t_ref, aff_ref, *, cap):
    lg = lg_ref[...]
    valid = lax.broadcasted_iota(jnp.int32, (1, LANES), 1) < N_EXPERTS
    lg = jnp.where(valid, lg, NEG_INF)
    e = jnp.exp(lg - jnp.max(lg, axis=1, keepdims=True))
    aff = e / jnp.sum(e, axis=1, keepdims=True)
    aff_ref[...] = aff
    def enough(cand):
        return jnp.sum(jnp.where(aff >= cand, 1.0, 0.0), axis=0, keepdims=True) >= cap

    tiny = jnp.full((1, LANES), 2.0 ** -126, F32)
    normal = enough(tiny)
    p = tiny
    for bit in range(6, -1, -1):
        cand = p * (2.0 ** (2 ** bit))
        p = jnp.where(enough(cand), cand, p)
    lo = jnp.where(normal, p, 0.0)
    hi = jnp.where(normal, p * 2.0, tiny)
    step = jnp.where(normal, p * 0.5, 0.0)
    for _ in range(THRESHOLD_MANTISSA_STEPS):
        cand = lo + step
        ok = enough(cand)
        lo = jnp.where(ok, cand, lo)
        hi = jnp.where(ok, hi, cand)
        step = step * 0.5
    gt = jnp.where(aff >= hi, 1.0, 0.0)
    eq = jnp.where(aff >= lo, 1.0, 0.0) - gt
    need = cap - jnp.sum(gt, axis=0, keepdims=True)
    tri = tri_ref[...]

    def count_before(x):
        blk = tri.shape[0]
        run = jnp.zeros((1, LANES), F32)
        outs = []
        for j in range(x.shape[0] // blk):
            xb = x[j * blk:(j + 1) * blk]
            outs.append(_dot(tri, xb.astype(BF16)) + run)
            run = run + jnp.sum(xb, axis=0, keepdims=True)
        return jnp.concatenate(outs, axis=0)

    sel = gt + eq * jnp.where(count_before(eq) < need, 1.0, 0.0)
    pos = count_before(sel)
    slot = jnp.where(valid & (sel > 0.0), pos, -1.0)
    slot_ref[...] = slot
    slot_t_ref[...] = _dot_nt(eye_ref[...], slot.astype(BF16))


def _route(logits, cap):
    b, seq, _ = logits.shape
    tri = (jnp.arange(ROUTE_BLOCK)[None, :] < jnp.arange(ROUTE_BLOCK)[:, None]).astype(BF16)
    eye = jnp.eye(LANES, dtype=BF16)
    assert seq % ROUTE_BLOCK == 0
    per_b = lambda i: (i, 0, 0)
    fixed = lambda i: (0, 0)
    return pl.pallas_call(
        functools.partial(_route_kernel, cap=cap),
        grid=(b,),
        in_specs=[
            pl.BlockSpec((None, seq, LANES), per_b),
            pl.BlockSpec((ROUTE_BLOCK, ROUTE_BLOCK), fixed),
            pl.BlockSpec((LANES, LANES), fixed),
        ],
        out_specs=[
            pl.BlockSpec((None, seq, LANES), per_b),
            pl.BlockSpec((None, LANES, seq), per_b),
            pl.BlockSpec((None, seq, LANES), per_b),
        ],
        out_shape=[
            jax.ShapeDtypeStruct((b, seq, LANES), F32),
            jax.ShapeDtypeStruct((b, LANES, seq), F32),
            jax.ShapeDtypeStruct((b, seq, LANES), F32),
        ],
        compiler_params=_params(("parallel",)),
        name="route",
    )(logits, tri, eye)


def _moe_gather_kernel(slot_ref, h_ref, xs_ref):
    srow = slot_ref[...]
    cap, seq = xs_ref.shape[0], srow.shape[1]
    ci = lax.broadcasted_iota(jnp.int32, (cap, seq), 0).astype(F32)
    onehot = jnp.where(srow == ci, 1.0, 0.0).astype(BF16)
    xs_ref[...] = _dot(onehot, h_ref[...]).astype(BF16)


def _moe_gather(slot_t, h2, cap):
    b, seq, _ = h2.shape
    return pl.pallas_call(
        _moe_gather_kernel,
        grid=(b, N_EXPERTS),
        in_specs=[
            pl.BlockSpec((None, None, 1, seq), lambda i, e: (i, e, 0, 0)),
            pl.BlockSpec((None, seq, D_MODEL), lambda i, e: (i, 0, 0)),
        ],
        out_specs=pl.BlockSpec((None, None, cap, D_MODEL), lambda i, e: (i, e, 0, 0)),
        out_shape=jax.ShapeDtypeStruct((b, N_EXPERTS, cap, D_MODEL), BF16),
        compiler_params=_params(("parallel", "parallel")),
        name="moe_gather",
    )(slot_t, h2)


def _moe_ffn_kernel(xs_ref, w1_ref, w3_ref, w2_ref, y_ref, act_ref, w1b_ref, w3b_ref, w2b_ref):
    s = pl.program_id(1)
    nb, cap, _ = xs_ref.shape
    nf = act_ref.shape[0]
    per = FFN_ROW_TILE // cap
    row_tiles = nb // per

    @pl.when(s < nf)
    def _():
        w1b_ref[...] = w1_ref[...].astype(BF16)
        w3b_ref[...] = w3_ref[...].astype(BF16)
        for r in range(row_tiles):
            x = xs_ref[r * per:(r + 1) * per].reshape(FFN_ROW_TILE, D_MODEL)
            up = _dot(x, w1b_ref[...])
            gt = _dot(x, w3b_ref[...])
            act_ref[s, r * FFN_ROW_TILE:(r + 1) * FFN_ROW_TILE, :] = (_silu(up) * gt).astype(BF16)

    @pl.when(s >= nf)
    def _():
        w2b_ref[...] = w2_ref[...].astype(BF16)
        for r in range(row_tiles):
            rows = slice(r * FFN_ROW_TILE, (r + 1) * FFN_ROW_TILE)
            act = jnp.concatenate([act_ref[f, rows, :] for f in range(nf)], axis=1)
            y = _dot(act, w2b_ref[...])
            y_ref[r * per:(r + 1) * per] = y.astype(BF16).reshape(per, cap, y.shape[1])


def _moe_ffn(xs, w1, w3, w2):
    b, ne, cap, _ = xs.shape
    nf = D_EXPERT // FFN_F_TILE
    nn = D_MODEL // FFN_N_TILE
    hidden = lambda e, s: (e, 0, jnp.minimum(s, nf - 1))
    out_col = lambda e, s: jnp.maximum(s - nf, 0)
    return pl.pallas_call(
        _moe_ffn_kernel,
        grid=(ne, nf + nn),
        in_specs=[
            pl.BlockSpec((b, None, cap, D_MODEL), lambda e, s: (0, e, 0, 0)),
            pl.BlockSpec((None, D_MODEL, FFN_F_TILE), hidden),
            pl.BlockSpec((None, D_MODEL, FFN_F_TILE), hidden),
            pl.BlockSpec((None, D_EXPERT, FFN_N_TILE), lambda e, s: (e, 0, out_col(e, s))),
        ],
        out_specs=pl.BlockSpec((b, None, cap, FFN_N_TILE), lambda e, s: (0, e, 0, out_col(e, s))),
        out_shape=jax.ShapeDtypeStruct(xs.shape, BF16),
        scratch_shapes=[
            pltpu.VMEM((nf, b * cap, FFN_F_TILE), BF16),
            pltpu.VMEM((D_MODEL, FFN_F_TILE), BF16),
            pltpu.VMEM((D_MODEL, FFN_F_TILE), BF16),
            pltpu.VMEM((D_EXPERT, FFN_N_TILE), BF16),
        ],
        compiler_params=_params(("parallel", "arbitrary")),
        name="moe_ffn",
    )(xs, w1, w3, w2)


def _moe_scatter_kernel(slot_ref, aff_ref, y_ref, x2_ref, g_ref, o_ref):
    slot = slot_ref[...]
    aff = aff_ref[...]
    rows, cap = slot.shape[0], y_ref.shape[1]
    ci = lax.broadcasted_iota(jnp.int32, (rows, cap), 1).astype(F32)
    acc = x2_ref[...]
    for e in range(N_EXPERTS):
        onehot = jnp.where(slot[:, e:e + 1] == ci, 1.0, 0.0).astype(BF16)
        acc = acc + aff[:, e:e + 1] * _dot(onehot, y_ref[e])
    o_ref[...] = _rms(acc, g_ref[...])


def _moe_scatter(slot, aff, y, x2, norm_g):
    b, seq, _ = x2.shape
    cap = y.shape[2]
    tile = lambda i, r: (i, r, 0)
    return pl.pallas_call(
        _moe_scatter_kernel,
        grid=(b, seq // ROW_TILE),
        in_specs=[
            pl.BlockSpec((None, ROW_TILE, LANES), tile),
            pl.BlockSpec((None, ROW_TILE, LANES), tile),
            pl.BlockSpec((None, N_EXPERTS, cap, D_MODEL), lambda i, r: (i, 0, 0, 0)),
            pl.BlockSpec((None, ROW_TILE, D_MODEL), tile),
            pl.BlockSpec((1, D_MODEL), lambda i, r: (0, 0)),
        ],
        out_specs=pl.BlockSpec((None, ROW_TILE, D_MODEL), tile),
        out_shape=jax.ShapeDtypeStruct((b, seq, D_MODEL), F32),
        compiler_params=_params(("parallel", "parallel")),
        name="moe_scatter",
    )(slot, aff, y, x2, norm_g)


def kernel(x, norm1_g, w_in, conv_w, conv_b, wq_m, wk_m, wv_m, w_if_fwd, b_if_fwd,
           w_if_bwd, b_if_bwd, mlstm_norm_g, mlstm_skip, attn_norm_g, w_out, norm2_g,
           w_router, w1, w3, w2, norm_f_g):
    b, seq, _ = x.shape
    assert w_in.shape[0] == 1, "single-layer problem"
    assert seq % ROW_TILE == 0 and seq % MLSTM_CHUNK == 0 and seq % ATTN_Q_TILE == 0
    cap = EC_CAPACITY * seq // N_EXPERTS
    assert FFN_ROW_TILE % cap == 0 and (b * cap) % FFN_ROW_TILE == 0
    nc = seq // MLSTM_CHUNK
    l = 0
    x2d = x.reshape(b * seq, D_MODEL)
    xm, z, *qkv_views = _in_proj(x2d, norm1_g[l][None, :], w_in[l].astype(BF16), seq)
    shp = lambda t: t.reshape(b, seq, t.shape[-1])
    wif_rows, bif_rows = _gate_rows(w_if_fwd[l], b_if_fwd[l], w_if_bwd[l], b_if_bwd[l])
    xc, qm, ktm, vm, gates = _mlstm_pre(
        shp(xm), conv_w[l], conv_b[l][None, :],
        _block_diag(wq_m[l]).astype(BF16), _block_diag(wk_m[l]).T.astype(BF16),
        _block_diag(wv_m[l]).astype(BF16), wif_rows.astype(BF16), bif_rows)
    gates = gates.reshape(b, 2, MLSTM_HEADS, 8, nc, MLSTM_CHUNK).transpose(0, 2, 1, 4, 3, 5)
    gates = gates.reshape(b, MLSTM_HEADS, 2, nc * 8, MLSTM_CHUNK)
    ym = _mlstm(qm, ktm, vm, gates, xc, shp(z), mlstm_norm_g[l][None, :],
                mlstm_skip[l][None, :])
    branches = [_band_attn(*qkv_views[3 * d:3 * d + 3], seq, win, dil)
                for d, (win, dil) in enumerate(DILATED_PATTERNS)]
    wr_pad = jnp.pad(w_router[l], ((0, 0), (0, LANES - N_EXPERTS)))
    x2, h2, logits = _out_proj(
        ym.reshape(b * seq, D_MLSTM), [o for o, _ in branches], [s for _, s in branches], x2d,
        attn_norm_g[l][None, :], w_out[l].astype(BF16), norm2_g[l][None, :], wr_pad)
    slot, slot_t, aff = _route(logits.reshape(b, seq, LANES), cap)
    xs = _moe_gather(slot_t.reshape(b, LANES, 1, seq), h2.reshape(b, seq, D_MODEL), cap)
    y = _moe_ffn(xs, w1[l], w3[l], w2[l])
    return _moe_scatter(slot, aff, y, x2.reshape(b, seq, D_MODEL), norm_f_g[None, :])
```

```python
import functools

import jax
import jax.numpy as jnp
from jax import lax
from jax.experimental import pallas as pl
from jax.experimental.pallas import tpu as pltpu

F32 = jnp.float32
BF16 = jnp.bfloat16

D_MODEL = 1024
D_MLSTM = 512
D_ATTN = 512
D_IN_PROJ = 2 * D_MLSTM + 3 * D_ATTN
MLSTM_HEADS = 4
MLSTM_HEAD_DIM = 128
MLSTM_QKV_BLOCK = 4
MLSTM_CONV = 5
ATTN_HEADS = 8
ATTN_HEAD_DIM = 64
ROPE_DIM = 16
ROPE_THETA = 500000.0
DILATED_PATTERNS = ((128, 1), (512, 4), (2048, 16))
N_EXPERTS = 16
EC_CAPACITY = 2
D_EXPERT = 2816
NORM_EPS = 1e-6
NEG_INF = -1e30

LANES = 128
MLSTM_CHUNK = 128
ROW_TILE = 512
ATTN_Q_TILE = 128
FFN_F_TILE = 256
FFN_N_TILE = 256
FFN_ROW_TILE = 512
THRESHOLD_RADIX_BITS = (4, 4, 4, 4, 4, 3, 4, 4, 4, 4)
VMEM_LIMIT = 56 * 1024 * 1024


def _params(sem):
    return pltpu.CompilerParams(dimension_semantics=sem, vmem_limit_bytes=VMEM_LIMIT)


def _rms(x, g):
    return x * lax.rsqrt(jnp.mean(x * x, axis=-1, keepdims=True) + NORM_EPS) * g


def _silu(x):
    return x * (1.0 / (1.0 + jnp.exp(-x)))


def _dot(a, b):
    return jnp.dot(a, b, preferred_element_type=F32)


def _dot_nt(a, b):
    return lax.dot_general(a, b, (((1,), (1,)), ((), ())), preferred_element_type=F32)


def _in_proj_kernel(x_ref, g_ref, w_ref, cos_ref, sa_ref, sb_ref, xm_ref, z_ref, *rest):
    qkv_refs, scr_ref = rest[:-1], rest[-1]
    h = _rms(x_ref[...], g_ref[...])
    p = _dot(h.astype(BF16), w_ref[...])
    xm_ref[...] = p[:, :D_MLSTM]
    z_ref[...] = p[:, D_MLSTM:2 * D_MLSTM]
    cos, sa, sb = cos_ref[...], sa_ref[...], sb_ref[...]
    half = ROPE_DIM // 2

    def rope(t):
        outs = []
        for j in range(D_ATTN // LANES):
            tj = t[:, j * LANES:(j + 1) * LANES]
            up = pltpu.roll(tj, LANES - half, axis=1)
            dn = pltpu.roll(tj, half, axis=1)
            outs.append(tj * cos + up * sa + dn * sb)
        return jnp.concatenate(outs, axis=1)

    o = 2 * D_MLSTM
    qkv = (rope(p[:, o:o + D_ATTN]) * (ATTN_HEAD_DIM ** -0.5),
           rope(p[:, o + D_ATTN:o + 2 * D_ATTN]),
           p[:, o + 2 * D_ATTN:])
    rows = p.shape[0]
    groups = D_ATTN // LANES
    for a, val in enumerate(qkv):
        for j in range(groups):
            scr_ref[j] = val[:, j * LANES:(j + 1) * LANES]
        for d, (_, dil) in enumerate(DILATED_PATTERNS):
            ref = qkv_refs[3 * d + a]
            if dil == 1:
                ref[...] = val.astype(BF16)
                continue
            for r in range(dil):
                for j in range(groups):
                    piece = scr_ref[j, pl.ds(r, rows // dil, stride=dil), :]
                    ref[:, r * D_ATTN + j * LANES:r * D_ATTN + (j + 1) * LANES] = piece.astype(BF16)


def _rope_tables(seq):
    half = ROPE_DIM // 2
    inv_freq = ROPE_THETA ** (-2.0 * jnp.arange(half, dtype=F32) / ROPE_DIM)
    ang = jnp.arange(seq).astype(F32)[:, None] * inv_freq[None, :]
    cos, sin = jnp.cos(ang), jnp.sin(ang)
    pad = jnp.zeros((seq, ATTN_HEAD_DIM - ROPE_DIM), F32)
    cos_h = jnp.concatenate([cos, cos, pad + 1.0], axis=1)
    sa_h = jnp.concatenate([-sin, jnp.zeros_like(sin), pad], axis=1)
    sb_h = jnp.concatenate([jnp.zeros_like(sin), sin, pad], axis=1)
    rep = LANES // ATTN_HEAD_DIM
    return tuple(jnp.tile(t, (1, rep)) for t in (cos_h, sa_h, sb_h))


def _in_proj(x2d, g, w_bf, seq):
    n = x2d.shape[0]
    tiles_per_seq = seq // ROW_TILE
    cos, sa, sb = _rope_tables(seq)
    row = lambda i: (i, 0)
    fixed = lambda i: (0, 0)
    pos = lambda i: (i % tiles_per_seq, 0)
    return pl.pallas_call(
        _in_proj_kernel,
        grid=(n // ROW_TILE,),
        in_specs=[
            pl.BlockSpec((ROW_TILE, D_MODEL), row),
            pl.BlockSpec((1, D_MODEL), fixed),
            pl.BlockSpec((D_MODEL, D_IN_PROJ), fixed),
            pl.BlockSpec((ROW_TILE, LANES), pos),
            pl.BlockSpec((ROW_TILE, LANES), pos),
            pl.BlockSpec((ROW_TILE, LANES), pos),
        ],
        out_specs=[
            pl.BlockSpec((ROW_TILE, D_MLSTM), row),
            pl.BlockSpec((ROW_TILE, D_MLSTM), row),
        ] + [pl.BlockSpec((ROW_TILE // dil, dil * D_ATTN), row)
             for _, dil in DILATED_PATTERNS for _ in range(3)],
        out_shape=[
            jax.ShapeDtypeStruct((n, D_MLSTM), F32),
            jax.ShapeDtypeStruct((n, D_MLSTM), F32),
        ] + [jax.ShapeDtypeStruct((n // dil, dil * D_ATTN), BF16)
             for _, dil in DILATED_PATTERNS for _ in range(3)],
        scratch_shapes=[pltpu.VMEM((D_ATTN // LANES, ROW_TILE, LANES), F32)],
        compiler_params=_params(("parallel",)),
        name="in_proj",
    )(x2d, g, w_bf, cos, sa, sb)


def _mlstm_pre_kernel(xm_ref, cw_ref, cb_ref, wq_ref, wkt_ref, wv_ref, wif_ref, bif_ref,
                      xc_ref, q_ref, kt_ref, v_ref, g_ref):
    x = xm_ref[...]
    seq = x.shape[0]
    t = lax.broadcasted_iota(jnp.int32, x.shape, 0)
    acc = jnp.zeros_like(x) + cb_ref[...]
    for j in range(MLSTM_CONV):
        d = j - MLSTM_CONV // 2
        if d == 0:
            tap = x
        else:
            tap = pltpu.roll(x, (-d) % seq, axis=0)
            tap = jnp.where((t + d >= 0) & (t + d < seq), tap, 0.0)
        acc = acc + tap * cw_ref[j:j + 1, :]
    xc = _silu(acc)
    xc_ref[...] = xc
    xcb = xc.astype(BF16)
    q = _dot(xcb, wq_ref[...]).astype(BF16)
    kt = (_dot_nt(wkt_ref[...], xcb) * (MLSTM_HEAD_DIM ** -0.5)).astype(BF16)
    v = _dot(x.astype(BF16), wv_ref[...]).astype(BF16)
    q_ref[...] = q
    v_ref[...] = v
    L = MLSTM_CHUNK
    for c in range(seq // L):
        kt_ref[c] = kt[:, c * L:(c + 1) * L]
    wif = wif_ref[...]
    g_ref[...] = (_dot_nt(wif[:, :D_MLSTM], q) + _dot(wif[:, D_MLSTM:2 * D_MLSTM], kt)
                  + _dot_nt(wif[:, 2 * D_MLSTM:], v) + bif_ref[...])


def _block_diag(w):
    nblk = w.shape[0]
    n = nblk * MLSTM_QKV_BLOCK
    tiled = jnp.tile(w.reshape(n, MLSTM_QKV_BLOCK), (1, nblk))
    blk = jnp.arange(n) // MLSTM_QKV_BLOCK
    return jnp.where(blk[:, None] == blk[None, :], tiled, 0.0)


def _gate_rows(w_f, b_f, w_b, b_b):
    h = MLSTM_HEADS
    cols = []
    bias = []
    zero_w = jnp.zeros((w_f.shape[0],), F32)
    for off in (h, 0):
        for hd in range(h):
            cols += [w_f[:, off + hd], w_b[:, off + hd]] + [zero_w] * 6
            bias += [b_f[off + hd], b_b[off + hd]] + [jnp.zeros((), F32)] * 6
    return jnp.stack(cols, axis=0), jnp.stack(bias)[:, None]


def _mlstm_pre(xm, conv_w, conv_b, wq, wkt, wv, wif_rows, bif_rows):
    b, seq, _ = xm.shape
    nrow = wif_rows.shape[0]
    nc = seq // MLSTM_CHUNK
    per_b = lambda i: (i, 0, 0)
    fixed = lambda i: (0, 0)
    return pl.pallas_call(
        _mlstm_pre_kernel,
        grid=(b,),
        in_specs=[
            pl.BlockSpec((None, seq, D_MLSTM), per_b),
            pl.BlockSpec((MLSTM_CONV, D_MLSTM), fixed),
            pl.BlockSpec((1, D_MLSTM), fixed),
            pl.BlockSpec((D_MLSTM, D_MLSTM), fixed),
            pl.BlockSpec((D_MLSTM, D_MLSTM), fixed),
            pl.BlockSpec((D_MLSTM, D_MLSTM), fixed),
            pl.BlockSpec((nrow, 3 * D_MLSTM), fixed),
            pl.BlockSpec((nrow, 1), fixed),
        ],
        out_specs=[
            pl.BlockSpec((None, seq, D_MLSTM), per_b),
            pl.BlockSpec((None, seq, D_MLSTM), per_b),
            pl.BlockSpec((None, nc, D_MLSTM, MLSTM_CHUNK), lambda i: (i, 0, 0, 0)),
            pl.BlockSpec((None, seq, D_MLSTM), per_b),
            pl.BlockSpec((None, nrow, seq), per_b),
        ],
        out_shape=[
            jax.ShapeDtypeStruct((b, seq, D_MLSTM), F32),
            jax.ShapeDtypeStruct((b, seq, D_MLSTM), BF16),
            jax.ShapeDtypeStruct((b, nc, D_MLSTM, MLSTM_CHUNK), BF16),
            jax.ShapeDtypeStruct((b, seq, D_MLSTM), BF16),
            jax.ShapeDtypeStruct((b, nrow, seq), F32),
        ],
        compiler_params=_params(("parallel",)),
        name="mlstm_pre",
    )(xm, conv_w, conv_b, wq, wkt, wv, wif_rows, bif_rows)


def _log_sigmoid(x):
    return jnp.minimum(x, 0.0) - jnp.log1p(jnp.exp(-jnp.abs(x)))


def _mlstm_kernel(q_ref, kt_ref, v_ref, g_ref, xc_ref, z_ref, ng_ref, sk_ref, o_ref,
                  cum_ref, w_ref, ml_ref, tot_ref, dc_ref, cs_ref, ms_ref):
    L = MLSTM_CHUNK
    dh = MLSTM_HEAD_DIM
    nc = kt_ref.shape[0]
    rows_all = g_ref.shape[1]
    lane = lax.broadcasted_iota(jnp.int32, (rows_all, L), 1)
    fwd_row = lax.broadcasted_iota(jnp.int32, (rows_all, L), 0) % 8 == 0
    row_i = lax.broadcasted_iota(jnp.int32, (L, L), 0)
    col_i = lax.broadcasted_iota(jnp.int32, (L, L), 1)
    ones_col = jnp.where(lax.broadcasted_iota(jnp.int32, (L, dh), 1) == 0, 1.0, 0.0).astype(BF16)

    lf = _log_sigmoid(g_ref[0])
    pre, suf = lf, lf
    d = 1
    while d < L:
        pre = pre + jnp.where(lane >= d, pltpu.roll(pre, d, axis=1), 0.0)
        suf = suf + jnp.where(lane < L - d, pltpu.roll(suf, L - d, axis=1), 0.0)
        d *= 2
    cum = jnp.where(fwd_row, pre, suf)
    tot = jnp.where(fwd_row, cum[:, L - 1:L], cum[:, 0:1])
    a = tot - cum + g_ref[1]
    ml = jnp.max(a, axis=1, keepdims=True)
    cum_ref[...] = cum
    w_ref[...] = jnp.exp(a - ml)
    ml_ref[...] = jnp.broadcast_to(ml, (rows_all, L))
    tot_ref[...] = tot

    def chunk_rows(ref, c):
        return ref[pl.ds(pl.multiple_of(c * 8, 8), 8), :]

    def v_aug(c):
        return jnp.concatenate([v_ref[pl.ds(c * L, L), :], ones_col], axis=1)

    def phase_a(c, carry):
        kt = kt_ref[c].astype(F32)
        w = chunk_rows(w_ref, c)
        kw = jnp.concatenate([kt * w[0:1], kt * w[1:2]], axis=0)
        dc_ref[c] = _dot(kw.astype(BF16), v_aug(c))
        return carry

    lax.fori_loop(0, nc, phase_a, 0, unroll=2)

    def scan_dir(direction_row, reverse):
        off_rows = direction_row * dh
        off_cols = direction_row * 2 * dh

        def body(i, carry):
            c = (nc - 1 - i) if reverse else i
            state, m = carry
            cs_ref[c, :, off_cols:off_cols + 2 * dh] = state.astype(BF16)
            ms_ref[c, direction_row:direction_row + 1, :] = m
            ml_c = chunk_rows(ml_ref, c)[direction_row:direction_row + 1]
            g_c = chunk_rows(tot_ref, c)[direction_row:direction_row + 1]
            m_new = jnp.maximum(g_c + m, ml_c)
            alpha = jnp.exp(g_c + m - m_new)
            beta = jnp.exp(ml_c - m_new)
            alpha2 = jnp.concatenate([alpha, alpha], axis=1)
            beta2 = jnp.concatenate([beta, beta], axis=1)
            state = alpha2 * state + beta2 * dc_ref[c, off_rows:off_rows + dh, :]
            return state, m_new

        init = (jnp.zeros((dh, 2 * dh), F32), jnp.zeros((1, L), F32))
        lax.fori_loop(0, nc, body, init)

    scan_dir(0, False)
    scan_dir(1, True)

    ng = ng_ref[...]
    sk = sk_ref[...]

    def to_col(r):
        return jnp.sum(jnp.where(row_i == col_i, jnp.broadcast_to(r, (L, L)), 0.0),
                       axis=1, keepdims=True)

    def direction(s_qk, qc, vaug, i_r, b_r, m_prev, keep):
        b_c = to_col(b_r)
        log_d = jnp.where(keep, b_c - b_r + i_r, NEG_INF)
        inter = b_c + m_prev
        m_t = jnp.maximum(inter, jnp.max(log_d, axis=1, keepdims=True))
        w = jnp.exp(log_d - m_t) * s_qk
        scale = jnp.exp(inter - m_t)
        intra = _dot(w.astype(BF16), vaug)
        tot_c = intra + scale * qc
        den = tot_c[:, dh:dh + 1]
        return tot_c[:, :dh] / jnp.maximum(jnp.abs(den), jnp.exp(-m_t))

    def phase_c(c, carry):
        rows = pl.ds(c * L, L)
        q = q_ref[rows, :]
        s_qk = _dot(q, kt_ref[c])
        qc = _dot(q, cs_ref[c])
        vaug = v_aug(c)
        ms = ms_ref[c]
        b = chunk_rows(cum_ref, c)
        gi = chunk_rows(g_ref.at[1], c)
        h = (direction(s_qk, qc[:, :2 * dh], vaug, gi[0:1], b[0:1], ms[0:1, 0:1], col_i <= row_i)
             + direction(s_qk, qc[:, 2 * dh:], vaug, gi[1:2], b[1:2], ms[1:2, 0:1], col_i >= row_i))
        hn = _rms(h, ng)
        o_ref[rows, :] = (hn + sk * xc_ref[rows, :]) * _silu(z_ref[rows, :])
        return carry

    lax.fori_loop(0, nc, phase_c, 0, unroll=2)


def _mlstm(q, kt, v, gates, xc, z, norm_g, skip):
    b, seq, _ = q.shape
    nc = seq // MLSTM_CHUNK
    dh = MLSTM_HEAD_DIM
    head = lambda i, j: (i, 0, j)
    vec = lambda i, j: (0, j)
    blk = pl.BlockSpec((None, seq, dh), head)
    gate_rows = pltpu.VMEM((nc * 8, MLSTM_CHUNK), F32)
    return pl.pallas_call(
        _mlstm_kernel,
        grid=(b, MLSTM_HEADS),
        in_specs=[
            blk,
            pl.BlockSpec((None, nc, dh, MLSTM_CHUNK), lambda i, j: (i, 0, j, 0)),
            blk,
            pl.BlockSpec((None, None, 2, nc * 8, MLSTM_CHUNK), lambda i, j: (i, j, 0, 0, 0)),
            blk, blk,
            pl.BlockSpec((1, dh), vec),
            pl.BlockSpec((1, dh), vec),
        ],
        out_specs=blk,
        out_shape=jax.ShapeDtypeStruct((b, seq, D_MLSTM), F32),
        scratch_shapes=[
            gate_rows, gate_rows, gate_rows, gate_rows,
            pltpu.VMEM((nc, 2 * dh, 2 * dh), F32),
            pltpu.VMEM((nc, dh, 4 * dh), BF16),
            pltpu.VMEM((nc, 8, MLSTM_CHUNK), F32),
        ],
        compiler_params=_params(("parallel", "parallel")),
        name="mlstm",
    )(q, kt, v, gates, xc, z, norm_g, skip)


def _band_attn_kernel(q_ref, k_ref, v_ref, o_ref, lse_ref, *, half, dil):
    lsub = q_ref.shape[0]
    tq = ATTN_Q_TILE
    win = min(lsub, 2 * tq)
    first = lax.broadcasted_iota(jnp.int32, (1, LANES), 1) < ATTN_HEAD_DIM
    lane = lax.broadcasted_iota(jnp.int32, (tq, LANES), 1)
    rel = (lax.broadcasted_iota(jnp.int32, (2 * tq, win), 1)
           - lax.broadcasted_iota(jnp.int32, (2 * tq, win), 0) % tq)

    def tile(qs, ws, out_rows):
        keep = jnp.abs(rel + (ws - qs)) <= half
        for c in range(dil):
            lse_tile = jnp.zeros((tq, LANES), F32)
            for p in range(D_ATTN // LANES):
                lanes = slice(c * D_ATTN + p * LANES, c * D_ATTN + (p + 1) * LANES)
                q = q_ref[pl.ds(qs, tq), lanes]
                kw = k_ref[pl.ds(ws, win), lanes]
                vw = v_ref[pl.ds(ws, win), lanes]
                zero = jnp.zeros_like(q)
                q2 = jnp.concatenate([jnp.where(first, q, zero), jnp.where(first, zero, q)], axis=0)
                s = jnp.where(keep, _dot_nt(q2, kw), NEG_INF)
                m = jnp.max(s, axis=1, keepdims=True)
                e = jnp.exp(s - m)
                l = jnp.sum(e, axis=1, keepdims=True)
                o2 = _dot(e.astype(BF16), vw) / l
                o_ref[p, out_rows(c), :] = jnp.where(first, o2[:tq], o2[tq:])
                lse2 = m + jnp.log(l)
                lse_tile = jnp.where(lane == 2 * p, lse2[:tq],
                                     jnp.where(lane == 2 * p + 1, lse2[tq:], lse_tile))
            lse_ref[out_rows(c), :] = lse_tile

    if dil == 1:
        def body(t, carry):
            qs = pl.multiple_of(t * tq, tq)
            ws = pl.multiple_of(jnp.clip(qs - half, 0, lsub - win), half)
            tile(qs, ws, lambda c: pl.ds(qs, tq))
            return carry

        lax.fori_loop(0, lsub // tq, body, 0)
    else:
        for t in range(lsub // tq):
            qs = t * tq
            ws = min(max(qs - half, 0), lsub - win)
            tile(qs, ws, lambda c, qs=qs: pl.ds(qs * dil + c, tq, stride=dil))


def _band_attn(q, k, v, seq, win, dil):
    lsub = seq // dil
    b = q.shape[0] // lsub
    half = win // (2 * dil)
    pairs = D_ATTN // LANES
    assert lsub % ATTN_Q_TILE == 0 and half % 16 == 0 and ATTN_Q_TILE + 2 * half <= 2 * ATTN_Q_TILE
    blk = pl.BlockSpec((lsub, dil * D_ATTN), lambda i: (i, 0))
    return pl.pallas_call(
        functools.partial(_band_attn_kernel, half=half, dil=dil),
        grid=(b,),
        in_specs=[blk, blk, blk],
        out_specs=[pl.BlockSpec((None, pairs, seq, LANES), lambda i: (i, 0, 0, 0)),
                   pl.BlockSpec((seq, LANES), lambda i: (i, 0))],
        out_shape=[
            jax.ShapeDtypeStruct((b, pairs, seq, LANES), F32),
            jax.ShapeDtypeStruct((b * seq, LANES), F32),
        ],
        compiler_params=_params(("parallel",)),
        name=f"band_attn_d{dil}",
    )(q, k, v)


def _out_proj_kernel(ym_ref, o1_ref, o2_ref, o3_ref, l1_ref, l2_ref, l3_ref, sp_ref, x_ref, ag_ref,
                     w_ref, n2_ref, wr_ref, x2_ref, h2_ref, lg_ref):
    lses = [r[...] for r in (l1_ref, l2_ref, l3_ref)]
    top = jnp.maximum(jnp.maximum(lses[0], lses[1]), lses[2])
    wts = [jnp.exp(l - top) for l in lses]
    total = wts[0] + wts[1] + wts[2]
    spread = sp_ref[...]

    def per_lane(w):
        hi = w.astype(BF16)
        lo = (w - hi.astype(F32)).astype(BF16)
        return _dot(jnp.concatenate([hi, lo], axis=1), spread)

    def heads(o_ref):
        return jnp.concatenate([o_ref[p] for p in range(o_ref.shape[0])], axis=1)

    ya = sum(per_lane(w / total) * heads(o) for w, o in zip(wts, (o1_ref, o2_ref, o3_ref)))
    ya = _rms(ya, ag_ref[...])
    mixed = jnp.concatenate([ym_ref[...], ya], axis=1).astype(BF16)
    x2 = x_ref[...] + _dot(mixed, w_ref[...])
    x2_ref[...] = x2
    h2 = _rms(x2, n2_ref[...])
    hi = h2.astype(BF16)
    h2_ref[...] = hi
    lo = (h2 - hi.astype(F32)).astype(BF16)
    wr = wr_ref[...]
    w_hi = wr.astype(BF16)
    w_lo = (wr - w_hi.astype(F32)).astype(BF16)
    lg_ref[...] = _dot(jnp.concatenate([hi, lo, hi], axis=1),
                       jnp.concatenate([w_hi, w_hi, w_lo], axis=0))


def _out_proj(ym, branch_o, branch_lse, x2d, attn_g, w_bf, n2g, wr_pad):
    n = x2d.shape[0]
    spread = (jnp.arange(LANES)[:, None] == jnp.arange(D_ATTN)[None, :] // ATTN_HEAD_DIM)
    spread = jnp.tile(spread.astype(BF16), (2, 1))
    pairs, seq = branch_o[0].shape[1:3]
    tiles_per_seq = seq // ROW_TILE
    branch = pl.BlockSpec((None, pairs, ROW_TILE, LANES),
                          lambda i: (i // tiles_per_seq, 0, i % tiles_per_seq, 0))
    row = lambda i: (i, 0)
    fixed = lambda i: (0, 0)
    return pl.pallas_call(
        _out_proj_kernel,
        grid=(n // ROW_TILE,),
        in_specs=[
            pl.BlockSpec((ROW_TILE, D_MLSTM), row),
            branch, branch, branch,
            pl.BlockSpec((ROW_TILE, LANES), row),
            pl.BlockSpec((ROW_TILE, LANES), row),
            pl.BlockSpec((ROW_TILE, LANES), row),
            pl.BlockSpec((2 * LANES, D_ATTN), fixed),
            pl.BlockSpec((ROW_TILE, D_MODEL), row),
            pl.BlockSpec((1, D_ATTN), fixed),
            pl.BlockSpec((D_MODEL, D_MODEL), fixed),
            pl.BlockSpec((1, D_MODEL), fixed),
            pl.BlockSpec((D_MODEL, LANES), fixed),
        ],
        out_specs=[
            pl.BlockSpec((ROW_TILE, D_MODEL), row),
            pl.BlockSpec((ROW_TILE, D_MODEL), row),
            pl.BlockSpec((ROW_TILE, LANES), row),
        ],
        out_shape=[
            jax.ShapeDtypeStruct((n, D_MODEL), F32),
            jax.ShapeDtypeStruct((n, D_MODEL), BF16),
            jax.ShapeDtypeStruct((n, LANES), F32),
        ],
        compiler_params=_params(("parallel",)),
        name="out_proj",
    )(ym, *branch_o, *branch_lse, spread, x2d, attn_g, w_bf, n2g, wr_pad)


def _route_kernel(lg_ref, tri_ref, eye_ref, slot_ref, slot_t_ref, aff_ref, *, cap):
    lg = lg_ref[...]
    valid = lax.broadcasted_iota(jnp.int32, (1, LANES), 1) < N_EXPERTS
    lg = jnp.where(valid, lg, NEG_INF)
    e = jnp.exp(lg - jnp.max(lg, axis=1, keepdims=True))
    aff = e / jnp.sum(e, axis=1, keepdims=True)
    aff_ref[...] = aff
    groups = LANES // N_EXPERTS
    rpg = aff.shape[0] // groups
    lane = lax.broadcasted_iota(jnp.int32, (1, LANES), 1)
    packed = aff[:rpg]
    for g in range(1, groups):
        packed = packed + pltpu.roll(aff[g * rpg:(g + 1) * rpg], g * N_EXPERTS, axis=1)

    def over_groups(x):
        shift = N_EXPERTS
        while shift < LANES:
            x = x + pltpu.roll(x, shift, axis=1)
            shift *= 2
        return x

    def enough(cand):
        part = jnp.sum(jnp.where(packed >= cand, 1.0, 0.0), axis=0, keepdims=True)
        return over_groups(part) >= cap

    def narrow(lo, hi, cands):
        new_lo, new_hi = lo, hi
        for cand in cands:
            ok = enough(cand)
            new_lo = jnp.maximum(new_lo, jnp.where(ok, cand, lo))
            new_hi = jnp.minimum(new_hi, jnp.where(ok, hi, cand))
        return new_lo, new_hi

    tiny = jnp.full((1, LANES), 2.0 ** -126, F32)
    normal = enough(tiny)
    p = tiny
    for span, count in ((16, 7), (1, 15)):
        p, _ = narrow(p, p, [p * (2.0 ** (span * j)) for j in range(1, count + 1)])
    lo = jnp.where(normal, p, 0.0)
    hi = jnp.where(normal, p * 2.0, tiny)
    width = jnp.where(normal, p, 0.0)
    for bits in THRESHOLD_RADIX_BITS:
        width = width * (0.5 ** bits)
        lo, hi = narrow(lo, hi, [lo + j * width for j in range(1, 2 ** bits)])
    gt = jnp.where(packed >= hi, 1.0, 0.0)
    eq = jnp.where(packed >= lo, 1.0, 0.0) - gt
    need = cap - over_groups(jnp.sum(gt, axis=0, keepdims=True))
    tri = tri_ref[...]

    def count_before(x):
        per_group = jnp.sum(x, axis=0, keepdims=True)
        upto = per_group
        shift = N_EXPERTS
        while shift < LANES:
            upto = upto + jnp.where(lane >= shift, pltpu.roll(upto, shift, axis=1), 0.0)
            shift *= 2
        return _dot(tri, x.astype(BF16)) + (upto - per_group)

    sel = gt + eq * jnp.where(count_before(eq) < need, 1.0, 0.0)
    pos = count_before(sel)
    slot_packed = jnp.where(sel > 0.0, pos, -1.0)
    slot = jnp.concatenate(
        [jnp.where(valid, slot_packed if g == 0 else
                   pltpu.roll(slot_packed, LANES - g * N_EXPERTS, axis=1), -1.0)
         for g in range(groups)], axis=0)
    slot_ref[...] = slot
    slot_t_ref[...] = _dot_nt(eye_ref[...], slot.astype(BF16))


def _route(logits, cap):
    b, seq, _ = logits.shape
    assert LANES % N_EXPERTS == 0 and seq % (LANES // N_EXPERTS) == 0
    rpg = seq // (LANES // N_EXPERTS)
    tri = (jnp.arange(rpg)[None, :] < jnp.arange(rpg)[:, None]).astype(BF16)
    eye = jnp.eye(LANES, dtype=BF16)
    per_b = lambda i: (i, 0, 0)
    fixed = lambda i: (0, 0)
    return pl.pallas_call(
        functools.partial(_route_kernel, cap=cap),
        grid=(b,),
        in_specs=[
            pl.BlockSpec((None, seq, LANES), per_b),
            pl.BlockSpec((rpg, rpg), fixed),
            pl.BlockSpec((LANES, LANES), fixed),
        ],
        out_specs=[
            pl.BlockSpec((None, seq, LANES), per_b),
            pl.BlockSpec((None, LANES, seq), per_b),
            pl.BlockSpec((None, seq, LANES), per_b),
        ],
        out_shape=[
            jax.ShapeDtypeStruct((b, seq, LANES), F32),
            jax.ShapeDtypeStruct((b, LANES, seq), F32),
            jax.ShapeDtypeStruct((b, seq, LANES), F32),
        ],
        compiler_params=_params(("parallel",)),
        name="route",
    )(logits, tri, eye)


def _moe_gather_kernel(slot_ref, h_ref, xs_ref):
    srow = slot_ref[...]
    cap, seq = xs_ref.shape[0], srow.shape[1]
    ci = lax.broadcasted_iota(jnp.int32, (cap, seq), 0).astype(F32)
    onehot = jnp.where(srow == ci, 1.0, 0.0).astype(BF16)
    xs_ref[...] = _dot(onehot, h_ref[...]).astype(BF16)


def _moe_gather(slot_t, h2, cap):
    b, seq, _ = h2.shape
    return pl.pallas_call(
        _moe_gather_kernel,
        grid=(b, N_EXPERTS),
        in_specs=[
            pl.BlockSpec((None, None, 1, seq), lambda i, e: (i, e, 0, 0)),
            pl.BlockSpec((None, seq, D_MODEL), lambda i, e: (i, 0, 0)),
        ],
        out_specs=pl.BlockSpec((None, None, cap, D_MODEL), lambda i, e: (i, e, 0, 0)),
        out_shape=jax.ShapeDtypeStruct((b, N_EXPERTS, cap, D_MODEL), BF16),
        compiler_params=_params(("parallel", "parallel")),
        name="moe_gather",
    )(slot_t, h2)


def _moe_ffn_kernel(xs_ref, w1_ref, w3_ref, w2_ref, y_ref, act_ref, w1b_ref, w3b_ref, w2b_ref):
    s = pl.program_id(1)
    nb, cap, _ = xs_ref.shape
    nf = act_ref.shape[0]
    per = FFN_ROW_TILE // cap
    row_tiles = nb // per

    @pl.when(s < nf)
    def _():
        w1b_ref[...] = w1_ref[...].astype(BF16)
        w3b_ref[...] = w3_ref[...].astype(BF16)
        for r in range(row_tiles):
            x = xs_ref[r * per:(r + 1) * per].reshape(FFN_ROW_TILE, D_MODEL)
            up = _dot(x, w1b_ref[...])
            gt = _dot(x, w3b_ref[...])
            act_ref[s, r * FFN_ROW_TILE:(r + 1) * FFN_ROW_TILE, :] = (_silu(up) * gt).astype(BF16)

    @pl.when(s >= nf)
    def _():
        w2b_ref[...] = w2_ref[...].astype(BF16)
        for r in range(row_tiles):
            rows = slice(r * FFN_ROW_TILE, (r + 1) * FFN_ROW_TILE)
            act = jnp.concatenate([act_ref[f, rows, :] for f in range(nf)], axis=1)
            y = _dot(act, w2b_ref[...])
            y_ref[r * per:(r + 1) * per] = y.astype(BF16).reshape(per, cap, y.shape[1])


def _moe_ffn(xs, w1, w3, w2):
    b, ne, cap, _ = xs.shape
    nf = D_EXPERT // FFN_F_TILE
    nn = D_MODEL // FFN_N_TILE
    hidden = lambda e, s: (e, 0, jnp.minimum(s, nf - 1))
    out_col = lambda e, s: jnp.maximum(s - nf, 0)
    return pl.pallas_call(
        _moe_ffn_kernel,
        grid=(ne, nf + nn),
        in_specs=[
            pl.BlockSpec((b, None, cap, D_MODEL), lambda e, s: (0, e, 0, 0)),
            pl.BlockSpec((None, D_MODEL, FFN_F_TILE), hidden),
            pl.BlockSpec((None, D_MODEL, FFN_F_TILE), hidden),
            pl.BlockSpec((None, D_EXPERT, FFN_N_TILE), lambda e, s: (e, 0, out_col(e, s))),
        ],
        out_specs=pl.BlockSpec((b, None, cap, FFN_N_TILE), lambda e, s: (0, e, 0, out_col(e, s))),
        out_shape=jax.ShapeDtypeStruct(xs.shape, BF16),
        scratch_shapes=[
            pltpu.VMEM((nf, b * cap, FFN_F_TILE), BF16),
            pltpu.VMEM((D_MODEL, FFN_F_TILE), BF16),
            pltpu.VMEM((D_MODEL, FFN_F_TILE), BF16),
            pltpu.VMEM((D_EXPERT, FFN_N_TILE), BF16),
        ],
        compiler_params=_params(("parallel", "arbitrary")),
        name="moe_ffn",
    )(xs, w1, w3, w2)


def _moe_scatter_kernel(slot_ref, aff_ref, y_ref, x2_ref, g_ref, o_ref):
    slot = slot_ref[...]
    aff = aff_ref[...]
    rows, cap = slot.shape[0], y_ref.shape[1]
    ci = lax.broadcasted_iota(jnp.int32, (rows, cap), 1).astype(F32)
    acc = x2_ref[...]
    for e in range(N_EXPERTS):
        onehot = jnp.where(slot[:, e:e + 1] == ci, 1.0, 0.0).astype(BF16)
        acc = acc + aff[:, e:e + 1] * _dot(onehot, y_ref[e])
    o_ref[...] = _rms(acc, g_ref[...])


def _moe_scatter(slot, aff, y, x2, norm_g):
    b, seq, _ = x2.shape
    cap = y.shape[2]
    tile = lambda i, r: (i, r, 0)
    return pl.pallas_call(
        _moe_scatter_kernel,
        grid=(b, seq // ROW_TILE),
        in_specs=[
            pl.BlockSpec((None, ROW_TILE, LANES), tile),
            pl.BlockSpec((None, ROW_TILE, LANES), tile),
            pl.BlockSpec((None, N_EXPERTS, cap, D_MODEL), lambda i, r: (i, 0, 0, 0)),
            pl.BlockSpec((None, ROW_TILE, D_MODEL), tile),
            pl.BlockSpec((1, D_MODEL), lambda i, r: (0, 0)),
        ],
        out_specs=pl.BlockSpec((None, ROW_TILE, D_MODEL), tile),
        out_shape=jax.ShapeDtypeStruct((b, seq, D_MODEL), F32),
        compiler_params=_params(("parallel", "parallel")),
        name="moe_scatter",
    )(slot, aff, y, x2, norm_g)


def kernel(x, norm1_g, w_in, conv_w, conv_b, wq_m, wk_m, wv_m, w_if_fwd, b_if_fwd,
           w_if_bwd, b_if_bwd, mlstm_norm_g, mlstm_skip, attn_norm_g, w_out, norm2_g,
           w_router, w1, w3, w2, norm_f_g):
    b, seq, _ = x.shape
    assert w_in.shape[0] == 1, "single-layer problem"
    assert seq % ROW_TILE == 0 and seq % MLSTM_CHUNK == 0 and seq % ATTN_Q_TILE == 0
    cap = EC_CAPACITY * seq // N_EXPERTS
    assert FFN_ROW_TILE % cap == 0 and (b * cap) % FFN_ROW_TILE == 0
    nc = seq // MLSTM_CHUNK
    l = 0
    x2d = x.reshape(b * seq, D_MODEL)
    xm, z, *qkv_views = _in_proj(x2d, norm1_g[l][None, :], w_in[l].astype(BF16), seq)
    shp = lambda t: t.reshape(b, seq, t.shape[-1])
    wif_rows, bif_rows = _gate_rows(w_if_fwd[l], b_if_fwd[l], w_if_bwd[l], b_if_bwd[l])
    xc, qm, ktm, vm, gates = _mlstm_pre(
        shp(xm), conv_w[l], conv_b[l][None, :],
        _block_diag(wq_m[l]).astype(BF16), _block_diag(wk_m[l]).T.astype(BF16),
        _block_diag(wv_m[l]).astype(BF16), wif_rows.astype(BF16), bif_rows)
    gates = gates.reshape(b, 2, MLSTM_HEADS, 8, nc, MLSTM_CHUNK).transpose(0, 2, 1, 4, 3, 5)
    gates = gates.reshape(b, MLSTM_HEADS, 2, nc * 8, MLSTM_CHUNK)
    ym = _mlstm(qm, ktm, vm, gates, xc, shp(z), mlstm_norm_g[l][None, :],
                mlstm_skip[l][None, :])
    branches = [_band_attn(*qkv_views[3 * d:3 * d + 3], seq, win, dil)
                for d, (win, dil) in enumerate(DILATED_PATTERNS)]
    wr_pad = jnp.pad(w_router[l], ((0, 0), (0, LANES - N_EXPERTS)))
    x2, h2, logits = _out_proj(
        ym.reshape(b * seq, D_MLSTM), [o for o, _ in branches], [s for _, s in branches], x2d,
        attn_norm_g[l][None, :], w_out[l].astype(BF16), norm2_g[l][None, :], wr_pad)
    slot, slot_t, aff = _route(logits.reshape(b, seq, LANES), cap)
    xs = _moe_gather(slot_t.reshape(b, LANES, 1, seq), h2.reshape(b, seq, D_MODEL), cap)
    y = _moe_ffn(xs, w1[l], w3[l], w2[l])
    return _moe_scatter(slot, aff, y, x2.reshape(b, seq, D_MODEL), norm_f_g[None, :])
```

```python
import functools

import jax
import jax.numpy as jnp
from jax import lax
from jax.experimental import pallas as pl
from jax.experimental.pallas import tpu as pltpu

F32 = jnp.float32
BF16 = jnp.bfloat16

D_MODEL = 1024
D_MLSTM = 512
D_ATTN = 512
D_IN_PROJ = 2 * D_MLSTM + 3 * D_ATTN
MLSTM_HEADS = 4
MLSTM_HEAD_DIM = 128
MLSTM_QKV_BLOCK = 4
MLSTM_CONV = 5
ATTN_HEADS = 8
ATTN_HEAD_DIM = 64
ROPE_DIM = 16
ROPE_THETA = 500000.0
DILATED_PATTERNS = ((128, 1), (512, 4), (2048, 16))
N_EXPERTS = 16
EC_CAPACITY = 2
D_EXPERT = 2816
NORM_EPS = 1e-6
NEG_INF = -1e30

LANES = 128
MLSTM_CHUNK = 128
ROW_TILE = 512
ATTN_Q_TILE = 128
FFN_F_TILE = 256
FFN_N_TILE = 256
FFN_ROW_TILE = 512
THRESHOLD_RADIX_BITS = (4, 4, 4, 4, 4, 3, 4, 4, 4, 4)
VMEM_LIMIT = 56 * 1024 * 1024


def _params(sem):
    return pltpu.CompilerParams(dimension_semantics=sem, vmem_limit_bytes=VMEM_LIMIT)


def _rms(x, g):
    return x * lax.rsqrt(jnp.mean(x * x, axis=-1, keepdims=True) + NORM_EPS) * g


def _silu(x):
    return x * (1.0 / (1.0 + jnp.exp(-x)))


def _dot(a, b):
    return jnp.dot(a, b, preferred_element_type=F32)


def _dot_nt(a, b):
    return lax.dot_general(a, b, (((1,), (1,)), ((), ())), preferred_element_type=F32)


def _in_proj_kernel(x_ref, g_ref, w_ref, cos_ref, sa_ref, sb_ref, xm_ref, z_ref, *rest):
    qkv_refs, scr_ref = rest[:-1], rest[-1]
    h = _rms(x_ref[...], g_ref[...])
    p = _dot(h.astype(BF16), w_ref[...])
    xm_ref[...] = p[:, :D_MLSTM]
    z_ref[...] = p[:, D_MLSTM:2 * D_MLSTM]
    cos, sa, sb = cos_ref[...], sa_ref[...], sb_ref[...]
    half = ROPE_DIM // 2

    def rope(t):
        outs = []
        for j in range(D_ATTN // LANES):
            tj = t[:, j * LANES:(j + 1) * LANES]
            up = pltpu.roll(tj, LANES - half, axis=1)
            dn = pltpu.roll(tj, half, axis=1)
            outs.append(tj * cos + up * sa + dn * sb)
        return jnp.concatenate(outs, axis=1)

    o = 2 * D_MLSTM
    qkv = (rope(p[:, o:o + D_ATTN]) * (ATTN_HEAD_DIM ** -0.5),
           rope(p[:, o + D_ATTN:o + 2 * D_ATTN]),
           p[:, o + 2 * D_ATTN:])
    rows = p.shape[0]
    groups = D_ATTN // LANES
    for a, val in enumerate(qkv):
        for j in range(groups):
            scr_ref[j] = val[:, j * LANES:(j + 1) * LANES]
        for d, (_, dil) in enumerate(DILATED_PATTERNS):
            ref = qkv_refs[3 * d + a]
            if dil == 1:
                ref[...] = val.astype(BF16)
                continue
            for r in range(dil):
                for j in range(groups):
                    piece = scr_ref[j, pl.ds(r, rows // dil, stride=dil), :]
                    ref[:, r * D_ATTN + j * LANES:r * D_ATTN + (j + 1) * LANES] = piece.astype(BF16)


def _rope_tables(seq):
    half = ROPE_DIM // 2
    inv_freq = ROPE_THETA ** (-2.0 * jnp.arange(half, dtype=F32) / ROPE_DIM)
    ang = jnp.arange(seq).astype(F32)[:, None] * inv_freq[None, :]
    cos, sin = jnp.cos(ang), jnp.sin(ang)
    pad = jnp.zeros((seq, ATTN_HEAD_DIM - ROPE_DIM), F32)
    cos_h = jnp.concatenate([cos, cos, pad + 1.0], axis=1)
    sa_h = jnp.concatenate([-sin, jnp.zeros_like(sin), pad], axis=1)
    sb_h = jnp.concatenate([jnp.zeros_like(sin), sin, pad], axis=1)
    rep = LANES // ATTN_HEAD_DIM
    return tuple(jnp.tile(t, (1, rep)) for t in (cos_h, sa_h, sb_h))


def _in_proj(x2d, g, w_bf, seq):
    n = x2d.shape[0]
    tiles_per_seq = seq // ROW_TILE
    cos, sa, sb = _rope_tables(seq)
    row = lambda i: (i, 0)
    fixed = lambda i: (0, 0)
    pos = lambda i: (i % tiles_per_seq, 0)
    return pl.pallas_call(
        _in_proj_kernel,
        grid=(n // ROW_TILE,),
        in_specs=[
            pl.BlockSpec((ROW_TILE, D_MODEL), row),
            pl.BlockSpec((1, D_MODEL), fixed),
            pl.BlockSpec((D_MODEL, D_IN_PROJ), fixed),
            pl.BlockSpec((ROW_TILE, LANES), pos),
            pl.BlockSpec((ROW_TILE, LANES), pos),
            pl.BlockSpec((ROW_TILE, LANES), pos),
        ],
        out_specs=[
            pl.BlockSpec((ROW_TILE, D_MLSTM), row),
            pl.BlockSpec((ROW_TILE, D_MLSTM), row),
        ] + [pl.BlockSpec((ROW_TILE // dil, dil * D_ATTN), row)
             for _, dil in DILATED_PATTERNS for _ in range(3)],
        out_shape=[
            jax.ShapeDtypeStruct((n, D_MLSTM), F32),
            jax.ShapeDtypeStruct((n, D_MLSTM), F32),
        ] + [jax.ShapeDtypeStruct((n // dil, dil * D_ATTN), BF16)
             for _, dil in DILATED_PATTERNS for _ in range(3)],
        scratch_shapes=[pltpu.VMEM((D_ATTN // LANES, ROW_TILE, LANES), F32)],
        compiler_params=_params(("parallel",)),
        name="in_proj",
    )(x2d, g, w_bf, cos, sa, sb)


def _mlstm_pre_kernel(xm_ref, cw_ref, cb_ref, wq_ref, wkt_ref, wv_ref, wif_ref, bif_ref,
                      xc_ref, q_ref, kt_ref, v_ref, g_ref):
    x = xm_ref[...]
    seq = x.shape[0]
    t = lax.broadcasted_iota(jnp.int32, x.shape, 0)
    acc = jnp.zeros_like(x) + cb_ref[...]
    for j in range(MLSTM_CONV):
        d = j - MLSTM_CONV // 2
        if d == 0:
            tap = x
        else:
            tap = pltpu.roll(x, (-d) % seq, axis=0)
            tap = jnp.where((t + d >= 0) & (t + d < seq), tap, 0.0)
        acc = acc + tap * cw_ref[j:j + 1, :]
    xc = _silu(acc)
    xc_ref[...] = xc
    xcb = xc.astype(BF16)
    q = _dot(xcb, wq_ref[...]).astype(BF16)
    kt = (_dot_nt(wkt_ref[...], xcb) * (MLSTM_HEAD_DIM ** -0.5)).astype(BF16)
    v = _dot(x.astype(BF16), wv_ref[...]).astype(BF16)
    q_ref[...] = q
    v_ref[...] = v
    L = MLSTM_CHUNK
    for c in range(seq // L):
        kt_ref[c] = kt[:, c * L:(c + 1) * L]
    wif = wif_ref[...]
    g_ref[...] = (_dot_nt(wif[:, :D_MLSTM], q) + _dot(wif[:, D_MLSTM:2 * D_MLSTM], kt)
                  + _dot_nt(wif[:, 2 * D_MLSTM:], v) + bif_ref[...])


def _block_diag(w):
    nblk = w.shape[0]
    n = nblk * MLSTM_QKV_BLOCK
    tiled = jnp.tile(w.reshape(n, MLSTM_QKV_BLOCK), (1, nblk))
    blk = jnp.arange(n) // MLSTM_QKV_BLOCK
    return jnp.where(blk[:, None] == blk[None, :], tiled, 0.0)


def _gate_rows(w_f, b_f, w_b, b_b):
    h = MLSTM_HEADS
    cols = []
    bias = []
    zero_w = jnp.zeros((w_f.shape[0],), F32)
    for off in (h, 0):
        for hd in range(h):
            cols += [w_f[:, off + hd], w_b[:, off + hd]] + [zero_w] * 6
            bias += [b_f[off + hd], b_b[off + hd]] + [jnp.zeros((), F32)] * 6
    return jnp.stack(cols, axis=0), jnp.stack(bias)[:, None]


def _mlstm_pre(xm, conv_w, conv_b, wq, wkt, wv, wif_rows, bif_rows):
    b, seq, _ = xm.shape
    nrow = wif_rows.shape[0]
    nc = seq // MLSTM_CHUNK
    per_b = lambda i: (i, 0, 0)
    fixed = lambda i: (0, 0)
    return pl.pallas_call(
        _mlstm_pre_kernel,
        grid=(b,),
        in_specs=[
            pl.BlockSpec((None, seq, D_MLSTM), per_b),
            pl.BlockSpec((MLSTM_CONV, D_MLSTM), fixed),
            pl.BlockSpec((1, D_MLSTM), fixed),
            pl.BlockSpec((D_MLSTM, D_MLSTM), fixed),
            pl.BlockSpec((D_MLSTM, D_MLSTM), fixed),
            pl.BlockSpec((D_MLSTM, D_MLSTM), fixed),
            pl.BlockSpec((nrow, 3 * D_MLSTM), fixed),
            pl.BlockSpec((nrow, 1), fixed),
        ],
        out_specs=[
            pl.BlockSpec((None, seq, D_MLSTM), per_b),
            pl.BlockSpec((None, seq, D_MLSTM), per_b),
            pl.BlockSpec((None, nc, D_MLSTM, MLSTM_CHUNK), lambda i: (i, 0, 0, 0)),
            pl.BlockSpec((None, seq, D_MLSTM), per_b),
            pl.BlockSpec((None, nrow, seq), per_b),
        ],
        out_shape=[
            jax.ShapeDtypeStruct((b, seq, D_MLSTM), F32),
            jax.ShapeDtypeStruct((b, seq, D_MLSTM), BF16),
            jax.ShapeDtypeStruct((b, nc, D_MLSTM, MLSTM_CHUNK), BF16),
            jax.ShapeDtypeStruct((b, seq, D_MLSTM), BF16),
            jax.ShapeDtypeStruct((b, nrow, seq), F32),
        ],
        compiler_params=_params(("parallel",)),
        name="mlstm_pre",
    )(xm, conv_w, conv_b, wq, wkt, wv, wif_rows, bif_rows)


def _log_sigmoid(x):
    return jnp.minimum(x, 0.0) - jnp.log1p(jnp.exp(-jnp.abs(x)))


def _split3(x):
    hi = x.astype(BF16).astype(F32)
    mid = (x - hi).astype(BF16).astype(F32)
    lo = (x - hi - mid).astype(BF16).astype(F32)
    return hi, mid, lo


def _mlstm_kernel(q_ref, kt_ref, v_ref, g_ref, xc_ref, z_ref, ng_ref, sk_ref, o_ref,
                  w_ref, ml_ref, tot_ref, pm_ref, tb_ref, tr_ref, dc_ref, cs_ref, ms_ref):
    L = MLSTM_CHUNK
    dh = MLSTM_HEAD_DIM
    nc = kt_ref.shape[0]
    rows_all = g_ref.shape[1]
    lane = lax.broadcasted_iota(jnp.int32, (rows_all, L), 1)
    sub = lax.broadcasted_iota(jnp.int32, (rows_all, L), 0) % 8
    fwd_row = sub == 0
    row_i = lax.broadcasted_iota(jnp.int32, (L, L), 0)
    col_i = lax.broadcasted_iota(jnp.int32, (L, L), 1)
    ones_col = jnp.ones((L, dh), BF16)

    lf = _log_sigmoid(g_ref[0])
    pre, suf = lf, lf
    d = 1
    while d < L:
        pre = pre + jnp.where(lane >= d, pltpu.roll(pre, d, axis=1), 0.0)
        suf = suf + jnp.where(lane < L - d, pltpu.roll(suf, L - d, axis=1), 0.0)
        d *= 2
    cum = jnp.where(fwd_row, pre, suf)
    tot = jnp.where(fwd_row, cum[:, L - 1:L], cum[:, 0:1])
    a = tot - cum + g_ref[1]
    ml = jnp.max(a, axis=1, keepdims=True)
    w_ref[...] = jnp.exp(a - ml)
    ml_ref[...] = jnp.broadcast_to(ml, (rows_all, L))
    tot_ref[...] = tot
    r = g_ref[1] - cum
    pmax, smax = r, r
    d = 1
    while d < L:
        pmax = jnp.maximum(pmax, jnp.where(lane >= d, pltpu.roll(pmax, d, axis=1), NEG_INF))
        smax = jnp.maximum(smax, jnp.where(lane < L - d, pltpu.roll(smax, L - d, axis=1), NEG_INF))
        d *= 2
    pm_ref[...] = jnp.where(fwd_row, pmax, smax)

    def tile_bcast(x, src):
        y = jnp.where(sub == src, x, 0.0)
        if src:
            y = pltpu.roll(y, rows_all - src, axis=0)
        for s in (1, 2, 4):
            y = y + pltpu.roll(y, s, axis=0)
        return y

    for dr in range(2):
        b_hi, b_mid, b_lo = _split3(tile_bcast(cum, dr))
        r_hi, r_mid, r_lo = _split3(tile_bcast(r, dr))
        tb_ref[dr] = jnp.where(sub == 1, b_hi, jnp.where(sub == 2, b_mid, jnp.where(
            sub == 3, b_lo, jnp.where((sub >= 4) & (sub <= 6), 1.0, 0.0))))
        tr_ref[dr] = jnp.where(sub == 0, 1.0, jnp.where(sub == 4, r_hi, jnp.where(
            sub == 5, r_mid, jnp.where(sub == 6, r_lo, 0.0))))

    def chunk_rows(ref, c):
        return ref[pl.ds(pl.multiple_of(c * 8, 8), 8), :]

    def v_aug(c):
        return jnp.concatenate([v_ref[pl.ds(c * L, L), :], ones_col], axis=1)

    def phase_a(c, carry):
        kt = kt_ref[c].astype(F32)
        w = chunk_rows(w_ref, c)
        kw = jnp.concatenate([kt * w[0:1], kt * w[1:2]], axis=0)
        dc_ref[c] = _dot(kw.astype(BF16), v_aug(c))
        return carry

    lax.fori_loop(0, nc, phase_a, 0, unroll=4)

    def scan_dir(direction_row, reverse):
        off_rows = direction_row * dh
        off_cols = direction_row * 2 * dh

        def body(i, carry):
            c = (nc - 1 - i) if reverse else i
            state, m = carry
            cs_ref[c, :, off_cols:off_cols + 2 * dh] = state.astype(BF16)
            ms_ref[c, direction_row:direction_row + 1, :] = m
            ml_c = chunk_rows(ml_ref, c)[direction_row:direction_row + 1]
            g_c = chunk_rows(tot_ref, c)[direction_row:direction_row + 1]
            m_new = jnp.maximum(g_c + m, ml_c)
            alpha = jnp.exp(g_c + m - m_new)
            beta = jnp.exp(ml_c - m_new)
            alpha2 = jnp.concatenate([alpha, alpha], axis=1)
            beta2 = jnp.concatenate([beta, beta], axis=1)
            state = alpha2 * state + beta2 * dc_ref[c, off_rows:off_rows + dh, :]
            return state, m_new

        init = (jnp.zeros((dh, 2 * dh), F32), jnp.zeros((1, L), F32))
        lax.fori_loop(0, nc, body, init)

    scan_dir(0, False)
    scan_dir(1, True)

    ng = ng_ref[...]
    sk = sk_ref[...]

    sub8 = lax.broadcasted_iota(jnp.int32, (8, L), 0)
    floor_rows = jnp.where(sub8 == 0, 1.0, jnp.where(sub8 <= 3, -1.0, 0.0))
    no_rows = jnp.zeros((8, 3 * L), F32)

    def direction(s_qk, qc, vaug, ex, keep):
        w = jnp.exp(jnp.where(keep, ex[:, :L], NEG_INF)) * s_qk
        scale = jnp.exp(ex[:, L:2 * L])
        intra = _dot(w.astype(BF16), vaug)
        tot_c = intra + jnp.concatenate([scale, scale], axis=1) * qc
        return tot_c[:, :dh] / jnp.maximum(jnp.abs(tot_c[:, dh:]), jnp.exp(ex[:, 2 * L:]))

    def phase_c(c, carry):
        rows = pl.ds(c * L, L)
        q = q_ref[rows, :]
        s_qk = _dot(q, kt_ref[c])
        qc = _dot(q, cs_ref[c])
        vaug = v_aug(c)
        ms = ms_ref[c]
        e = -jnp.maximum(ms, chunk_rows(pm_ref, c))
        e = (e - jnp.abs(e) * (2.0 ** -7)).astype(BF16).astype(F32)
        lhs_tiles, rhs_tiles = [], []
        for dr in range(2):
            e_rows = jnp.broadcast_to(e[dr:dr + 1], (8, L))
            lhs_tiles.append(jnp.where(sub8 == 0, e_rows, chunk_rows(tb_ref.at[dr], c)))
            m_hi, m_mid, m_lo = _split3(jnp.broadcast_to(ms[dr:dr + 1], (8, L)))
            scale_rows = jnp.where(sub8 == 0, 1.0, jnp.where(sub8 == 4, m_hi, jnp.where(
                sub8 == 5, m_mid, jnp.where(sub8 == 6, m_lo, 0.0))))
            rhs_tiles.append(jnp.concatenate(
                [chunk_rows(tr_ref.at[dr], c), scale_rows, floor_rows], axis=1))
        lhs = jnp.concatenate(lhs_tiles, axis=0).astype(BF16)
        rhs = jnp.concatenate([jnp.concatenate([rhs_tiles[0], no_rows], axis=1),
                               jnp.concatenate([no_rows, rhs_tiles[1]], axis=1)],
                              axis=0).astype(BF16)
        ex = lax.dot_general(lhs, rhs, (((0,), (0,)), ((), ())),
                             preferred_element_type=F32)
        h = (direction(s_qk, qc[:, :2 * dh], vaug, ex[:, :3 * L], col_i <= row_i)
             + direction(s_qk, qc[:, 2 * dh:], vaug, ex[:, 3 * L:], col_i >= row_i))
        hn = _rms(h, ng)
        o_ref[rows, :] = (hn + sk * xc_ref[rows, :]) * _silu(z_ref[rows, :])
        return carry

    lax.fori_loop(0, nc, phase_c, 0, unroll=8)


def _mlstm(q, kt, v, gates, xc, z, norm_g, skip):
    b, seq, _ = q.shape
    nc = seq // MLSTM_CHUNK
    dh = MLSTM_HEAD_DIM
    head = lambda i, j: (i, 0, j)
    vec = lambda i, j: (0, j)
    blk = pl.BlockSpec((None, seq, dh), head)
    gate_rows = pltpu.VMEM((nc * 8, MLSTM_CHUNK), F32)
    gate_tiles = pltpu.VMEM((2, nc * 8, MLSTM_CHUNK), F32)
    return pl.pallas_call(
        _mlstm_kernel,
        grid=(b, MLSTM_HEADS),
        in_specs=[
            blk,
            pl.BlockSpec((None, nc, dh, MLSTM_CHUNK), lambda i, j: (i, 0, j, 0)),
            blk,
            pl.BlockSpec((None, None, 2, nc * 8, MLSTM_CHUNK), lambda i, j: (i, j, 0, 0, 0)),
            blk, blk,
            pl.BlockSpec((1, dh), vec),
            pl.BlockSpec((1, dh), vec),
        ],
        out_specs=blk,
        out_shape=jax.ShapeDtypeStruct((b, seq, D_MLSTM), F32),
        scratch_shapes=[
            gate_rows, gate_rows, gate_rows, gate_rows, gate_tiles, gate_tiles,
            pltpu.VMEM((nc, 2 * dh, 2 * dh), F32),
            pltpu.VMEM((nc, dh, 4 * dh), BF16),
            pltpu.VMEM((nc, 8, MLSTM_CHUNK), F32),
        ],
        compiler_params=_params(("parallel", "parallel")),
        name="mlstm",
    )(q, kt, v, gates, xc, z, norm_g, skip)


def _band_attn_kernel(q_ref, k_ref, v_ref, o_ref, lse_ref, *, half, dil):
    lsub = q_ref.shape[0]
    tq = ATTN_Q_TILE
    win = min(lsub, 2 * tq)
    first = lax.broadcasted_iota(jnp.int32, (1, LANES), 1) < ATTN_HEAD_DIM
    lane = lax.broadcasted_iota(jnp.int32, (tq, LANES), 1)
    rel = (lax.broadcasted_iota(jnp.int32, (2 * tq, win), 1)
           - lax.broadcasted_iota(jnp.int32, (2 * tq, win), 0) % tq)

    def tile(qs, ws, out_rows):
        keep = jnp.abs(rel + (ws - qs)) <= half
        for c in range(dil):
            lse_tile = jnp.zeros((tq, LANES), F32)
            for p in range(D_ATTN // LANES):
                lanes = slice(c * D_ATTN + p * LANES, c * D_ATTN + (p + 1) * LANES)
                q = q_ref[pl.ds(qs, tq), lanes]
                kw = k_ref[pl.ds(ws, win), lanes]
                vw = v_ref[pl.ds(ws, win), lanes]
                zero = jnp.zeros_like(q)
                q2 = jnp.concatenate([jnp.where(first, q, zero), jnp.where(first, zero, q)], axis=0)
                s = jnp.where(keep, _dot_nt(q2, kw), NEG_INF)
                m = jnp.max(s, axis=1, keepdims=True)
                e = jnp.exp(s - m)
                l = jnp.sum(e, axis=1, keepdims=True)
                o2 = _dot(e.astype(BF16), vw) / l
                o_ref[p, out_rows(c), :] = jnp.where(first, o2[:tq], o2[tq:])
                lse2 = m + jnp.log(l)
                lse_tile = jnp.where(lane == 2 * p, lse2[:tq],
                                     jnp.where(lane == 2 * p + 1, lse2[tq:], lse_tile))
            lse_ref[out_rows(c), :] = lse_tile

    if dil == 1:
        def body(t, carry):
            qs = pl.multiple_of(t * tq, tq)
            ws = pl.multiple_of(jnp.clip(qs - half, 0, lsub - win), half)
            tile(qs, ws, lambda c: pl.ds(qs, tq))
            return carry

        lax.fori_loop(0, lsub // tq, body, 0, unroll=2)
    else:
        for t in range(lsub // tq):
            qs = t * tq
            ws = min(max(qs - half, 0), lsub - win)
            tile(qs, ws, lambda c, qs=qs: pl.ds(qs * dil + c, tq, stride=dil))


def _band_attn(q, k, v, seq, win, dil):
    lsub = seq // dil
    b = q.shape[0] // lsub
    half = win // (2 * dil)
    pairs = D_ATTN // LANES
    assert lsub % ATTN_Q_TILE == 0 and half % 16 == 0 and ATTN_Q_TILE + 2 * half <= 2 * ATTN_Q_TILE
    blk = pl.BlockSpec((lsub, dil * D_ATTN), lambda i: (i, 0))
    return pl.pallas_call(
        functools.partial(_band_attn_kernel, half=half, dil=dil),
        grid=(b,),
        in_specs=[blk, blk, blk],
        out_specs=[pl.BlockSpec((None, pairs, seq, LANES), lambda i: (i, 0, 0, 0)),
                   pl.BlockSpec((seq, LANES), lambda i: (i, 0))],
        out_shape=[
            jax.ShapeDtypeStruct((b, pairs, seq, LANES), F32),
            jax.ShapeDtypeStruct((b * seq, LANES), F32),
        ],
        compiler_params=_params(("parallel",)),
        name=f"band_attn_d{dil}",
    )(q, k, v)


def _out_proj_kernel(ym_ref, o1_ref, o2_ref, o3_ref, l1_ref, l2_ref, l3_ref, sp_ref, x_ref, ag_ref,
                     w_ref, n2_ref, wr_ref, x2_ref, h2_ref, lg_ref):
    lses = [r[...] for r in (l1_ref, l2_ref, l3_ref)]
    top = jnp.maximum(jnp.maximum(lses[0], lses[1]), lses[2])
    wts = [jnp.exp(l - top) for l in lses]
    total = wts[0] + wts[1] + wts[2]
    spread = sp_ref[...]

    def per_lane(w):
        hi = w.astype(BF16)
        lo = (w - hi.astype(F32)).astype(BF16)
        return _dot(jnp.concatenate([hi, lo], axis=1), spread)

    def heads(o_ref):
        return jnp.concatenate([o_ref[p] for p in range(o_ref.shape[0])], axis=1)

    ya = sum(per_lane(w / total) * heads(o) for w, o in zip(wts, (o1_ref, o2_ref, o3_ref)))
    ya = _rms(ya, ag_ref[...])
    mixed = jnp.concatenate([ym_ref[...], ya], axis=1).astype(BF16)
    x2 = x_ref[...] + _dot(mixed, w_ref[...])
    x2_ref[...] = x2
    h2 = _rms(x2, n2_ref[...])
    hi = h2.astype(BF16)
    h2_ref[...] = hi
    lo = (h2 - hi.astype(F32)).astype(BF16)
    wr = wr_ref[...]
    w_hi = wr.astype(BF16)
    w_lo = (wr - w_hi.astype(F32)).astype(BF16)
    lg_ref[...] = _dot(jnp.concatenate([hi, lo, hi], axis=1),
                       jnp.concatenate([w_hi, w_hi, w_lo], axis=0))


def _out_proj(ym, branch_o, branch_lse, x2d, attn_g, w_bf, n2g, wr_pad):
    n = x2d.shape[0]
    spread = (jnp.arange(LANES)[:, None] == jnp.arange(D_ATTN)[None, :] // ATTN_HEAD_DIM)
    spread = jnp.tile(spread.astype(BF16), (2, 1))
    pairs, seq = branch_o[0].shape[1:3]
    tiles_per_seq = seq // ROW_TILE
    branch = pl.BlockSpec((None, pairs, ROW_TILE, LANES),
                          lambda i: (i // tiles_per_seq, 0, i % tiles_per_seq, 0))
    row = lambda i: (i, 0)
    fixed = lambda i: (0, 0)
    return pl.pallas_call(
        _out_proj_kernel,
        grid=(n // ROW_TILE,),
        in_specs=[
            pl.BlockSpec((ROW_TILE, D_MLSTM), row),
            branch, branch, branch,
            pl.BlockSpec((ROW_TILE, LANES), row),
            pl.BlockSpec((ROW_TILE, LANES), row),
            pl.BlockSpec((ROW_TILE, LANES), row),
            pl.BlockSpec((2 * LANES, D_ATTN), fixed),
            pl.BlockSpec((ROW_TILE, D_MODEL), row),
            pl.BlockSpec((1, D_ATTN), fixed),
            pl.BlockSpec((D_MODEL, D_MODEL), fixed),
            pl.BlockSpec((1, D_MODEL), fixed),
            pl.BlockSpec((D_MODEL, LANES), fixed),
        ],
        out_specs=[
            pl.BlockSpec((ROW_TILE, D_MODEL), row),
            pl.BlockSpec((ROW_TILE, D_MODEL), row),
            pl.BlockSpec((ROW_TILE, LANES), row),
        ],
        out_shape=[
            jax.ShapeDtypeStruct((n, D_MODEL), F32),
            jax.ShapeDtypeStruct((n, D_MODEL), BF16),
            jax.ShapeDtypeStruct((n, LANES), F32),
        ],
        compiler_params=_params(("parallel",)),
        name="out_proj",
    )(ym, *branch_o, *branch_lse, spread, x2d, attn_g, w_bf, n2g, wr_pad)


def _route_kernel(lg_ref, tri_ref, eye_ref, slot_ref, slot_t_ref, aff_ref, *, cap):
    lg = lg_ref[...]
    valid = lax.broadcasted_iota(jnp.int32, (1, LANES), 1) < N_EXPERTS
    lg = jnp.where(valid, lg, NEG_INF)
    e = jnp.exp(lg - jnp.max(lg, axis=1, keepdims=True))
    aff = e / jnp.sum(e, axis=1, keepdims=True)
    aff_ref[...] = aff
    groups = LANES // N_EXPERTS
    rpg = aff.shape[0] // groups
    lane = lax.broadcasted_iota(jnp.int32, (1, LANES), 1)
    packed = aff[:rpg]
    for g in range(1, groups):
        packed = packed + pltpu.roll(aff[g * rpg:(g + 1) * rpg], g * N_EXPERTS, axis=1)

    def over_groups(x):
        shift = N_EXPERTS
        while shift < LANES:
            x = x + pltpu.roll(x, shift, axis=1)
            shift *= 2
        return x

    def enough(cand):
        part = jnp.sum(jnp.where(packed >= cand, 1.0, 0.0), axis=0, keepdims=True)
        return over_groups(part) >= cap

    def narrow(lo, hi, cands):
        new_lo, new_hi = lo, hi
        for cand in cands:
            ok = enough(cand)
            new_lo = jnp.maximum(new_lo, jnp.where(ok, cand, lo))
            new_hi = jnp.minimum(new_hi, jnp.where(ok, hi, cand))
        return new_lo, new_hi

    tiny = jnp.full((1, LANES), 2.0 ** -126, F32)
    normal = enough(tiny)
    p = tiny
    for span, count in ((16, 7), (1, 15)):
        p, _ = narrow(p, p, [p * (2.0 ** (span * j)) for j in range(1, count + 1)])
    lo = jnp.where(normal, p, 0.0)
    hi = jnp.where(normal, p * 2.0, tiny)
    width = jnp.where(normal, p, 0.0)
    for bits in THRESHOLD_RADIX_BITS:
        width = width * (0.5 ** bits)
        lo, hi = narrow(lo, hi, [lo + j * width for j in range(1, 2 ** bits)])
    gt = jnp.where(packed >= hi, 1.0, 0.0)
    eq = jnp.where(packed >= lo, 1.0, 0.0) - gt
    need = cap - over_groups(jnp.sum(gt, axis=0, keepdims=True))
    tri = tri_ref[...]

    def count_before(x):
        per_group = jnp.sum(x, axis=0, keepdims=True)
        upto = per_group
        shift = N_EXPERTS
        while shift < LANES:
            upto = upto + jnp.where(lane >= shift, pltpu.roll(upto, shift, axis=1), 0.0)
            shift *= 2
        return _dot(tri, x.astype(BF16)) + (upto - per_group)

    sel = gt + eq * jnp.where(count_before(eq) < need, 1.0, 0.0)
    pos = count_before(sel)
    slot_packed = jnp.where(sel > 0.0, pos, -1.0)
    slot = jnp.concatenate(
        [jnp.where(valid, slot_packed if g == 0 else
                   pltpu.roll(slot_packed, LANES - g * N_EXPERTS, axis=1), -1.0)
         for g in range(groups)], axis=0)
    slot_ref[...] = slot
    slot_t_ref[...] = _dot_nt(eye_ref[...], slot.astype(BF16))


def _route(logits, cap):
    b, seq, _ = logits.shape
    assert LANES % N_EXPERTS == 0 and seq % (LANES // N_EXPERTS) == 0
    rpg = seq // (LANES // N_EXPERTS)
    tri = (jnp.arange(rpg)[None, :] < jnp.arange(rpg)[:, None]).astype(BF16)
    eye = jnp.eye(LANES, dtype=BF16)
    per_b = lambda i: (i, 0, 0)
    fixed = lambda i: (0, 0)
    return pl.pallas_call(
        functools.partial(_route_kernel, cap=cap),
        grid=(b,),
        in_specs=[
            pl.BlockSpec((None, seq, LANES), per_b),
            pl.BlockSpec((rpg, rpg), fixed),
            pl.BlockSpec((LANES, LANES), fixed),
        ],
        out_specs=[
            pl.BlockSpec((None, seq, LANES), per_b),
            pl.BlockSpec((None, LANES, seq), per_b),
            pl.BlockSpec((None, seq, LANES), per_b),
        ],
        out_shape=[
            jax.ShapeDtypeStruct((b, seq, LANES), F32),
            jax.ShapeDtypeStruct((b, LANES, seq), F32),
            jax.ShapeDtypeStruct((b, seq, LANES), F32),
        ],
        compiler_params=_params(("parallel",)),
        name="route",
    )(logits, tri, eye)


def _moe_gather_kernel(slot_ref, h_ref, xs_ref):
    srow = slot_ref[...]
    cap, seq = xs_ref.shape[0], srow.shape[1]
    ci = lax.broadcasted_iota(jnp.int32, (cap, seq), 0).astype(F32)
    onehot = jnp.where(srow == ci, 1.0, 0.0).astype(BF16)
    xs_ref[...] = _dot(onehot, h_ref[...]).astype(BF16)


def _moe_gather(slot_t, h2, cap):
    b, seq, _ = h2.shape
    return pl.pallas_call(
        _moe_gather_kernel,
        grid=(b, N_EXPERTS),
        in_specs=[
            pl.BlockSpec((None, None, 1, seq), lambda i, e: (i, e, 0, 0)),
            pl.BlockSpec((None, seq, D_MODEL), lambda i, e: (i, 0, 0)),
        ],
        out_specs=pl.BlockSpec((None, None, cap, D_MODEL), lambda i, e: (i, e, 0, 0)),
        out_shape=jax.ShapeDtypeStruct((b, N_EXPERTS, cap, D_MODEL), BF16),
        compiler_params=_params(("parallel", "parallel")),
        name="moe_gather",
    )(slot_t, h2)


def _moe_ffn_kernel(xs_ref, w1_ref, w3_ref, w2_ref, y_ref, act_ref, w1b_ref, w3b_ref, w2b_ref):
    s = pl.program_id(1)
    nb, cap, _ = xs_ref.shape
    nf = act_ref.shape[0]
    per = FFN_ROW_TILE // cap
    row_tiles = nb // per

    @pl.when(s < nf)
    def _():
        w1b_ref[...] = w1_ref[...].astype(BF16)
        w3b_ref[...] = w3_ref[...].astype(BF16)
        for r in range(row_tiles):
            x = xs_ref[r * per:(r + 1) * per].reshape(FFN_ROW_TILE, D_MODEL)
            up = _dot(x, w1b_ref[...])
            gt = _dot(x, w3b_ref[...])
            act_ref[s, r * FFN_ROW_TILE:(r + 1) * FFN_ROW_TILE, :] = (_silu(up) * gt).astype(BF16)

    @pl.when(s >= nf)
    def _():
        w2b_ref[...] = w2_ref[...].astype(BF16)
        for r in range(row_tiles):
            rows = slice(r * FFN_ROW_TILE, (r + 1) * FFN_ROW_TILE)
            act = jnp.concatenate([act_ref[f, rows, :] for f in range(nf)], axis=1)
            y = _dot(act, w2b_ref[...])
            y_ref[r * per:(r + 1) * per] = y.astype(BF16).reshape(per, cap, y.shape[1])


def _moe_ffn(xs, w1, w3, w2):
    b, ne, cap, _ = xs.shape
    nf = D_EXPERT // FFN_F_TILE
    nn = D_MODEL // FFN_N_TILE
    hidden = lambda e, s: (e, 0, jnp.minimum(s, nf - 1))
    out_col = lambda e, s: jnp.maximum(s - nf, 0)
    return pl.pallas_call(
        _moe_ffn_kernel,
        grid=(ne, nf + nn),
        in_specs=[
            pl.BlockSpec((b, None, cap, D_MODEL), lambda e, s: (0, e, 0, 0)),
            pl.BlockSpec((None, D_MODEL, FFN_F_TILE), hidden),
            pl.BlockSpec((None, D_MODEL, FFN_F_TILE), hidden),
            pl.BlockSpec((None, D_EXPERT, FFN_N_TILE), lambda e, s: (e, 0, out_col(e, s))),
        ],
        out_specs=pl.BlockSpec((b, None, cap, FFN_N_TILE), lambda e, s: (0, e, 0, out_col(e, s))),
        out_shape=jax.ShapeDtypeStruct(xs.shape, BF16),
        scratch_shapes=[
            pltpu.VMEM((nf, b * cap, FFN_F_TILE), BF16),
            pltpu.VMEM((D_MODEL, FFN_F_TILE), BF16),
            pltpu.VMEM((D_MODEL, FFN_F_TILE), BF16),
            pltpu.VMEM((D_EXPERT, FFN_N_TILE), BF16),
        ],
        compiler_params=_params(("parallel", "arbitrary")),
        name="moe_ffn",
    )(xs, w1, w3, w2)


def _moe_scatter_kernel(slot_ref, aff_ref, y_ref, x2_ref, g_ref, o_ref):
    slot = slot_ref[...]
    aff = aff_ref[...]
    rows, cap = slot.shape[0], y_ref.shape[1]
    ci = lax.broadcasted_iota(jnp.int32, (rows, cap), 1).astype(F32)
    acc = x2_ref[...]
    for e in range(N_EXPERTS):
        onehot = jnp.where(slot[:, e:e + 1] == ci, 1.0, 0.0).astype(BF16)
        acc = acc + aff[:, e:e + 1] * _dot(onehot, y_ref[e])
    o_ref[...] = _rms(acc, g_ref[...])


def _moe_scatter(slot, aff, y, x2, norm_g):
    b, seq, _ = x2.shape
    cap = y.shape[2]
    tile = lambda i, r: (i, r, 0)
    return pl.pallas_call(
        _moe_scatter_kernel,
        grid=(b, seq // ROW_TILE),
        in_specs=[
            pl.BlockSpec((None, ROW_TILE, LANES), tile),
            pl.BlockSpec((None, ROW_TILE, LANES), tile),
            pl.BlockSpec((None, N_EXPERTS, cap, D_MODEL), lambda i, r: (i, 0, 0, 0)),
            pl.BlockSpec((None, ROW_TILE, D_MODEL), tile),
            pl.BlockSpec((1, D_MODEL), lambda i, r: (0, 0)),
        ],
        out_specs=pl.BlockSpec((None, ROW_TILE, D_MODEL), tile),
        out_shape=jax.ShapeDtypeStruct((b, seq, D_MODEL), F32),
        compiler_params=_params(("parallel", "parallel")),
        name="moe_scatter",
    )(slot, aff, y, x2, norm_g)


def kernel(x, norm1_g, w_in, conv_w, conv_b, wq_m, wk_m, wv_m, w_if_fwd, b_if_fwd,
           w_if_bwd, b_if_bwd, mlstm_norm_g, mlstm_skip, attn_norm_g, w_out, norm2_g,
           w_router, w1, w3, w2, norm_f_g):
    b, seq, _ = x.shape
    assert w_in.shape[0] == 1, "single-layer problem"
    assert seq % ROW_TILE == 0 and seq % MLSTM_CHUNK == 0 and seq % ATTN_Q_TILE == 0
    cap = EC_CAPACITY * seq // N_EXPERTS
    assert FFN_ROW_TILE % cap == 0 and (b * cap) % FFN_ROW_TILE == 0
    nc = seq // MLSTM_CHUNK
    l = 0
    x2d = x.reshape(b * seq, D_MODEL)
    xm, z, *qkv_views = _in_proj(x2d, norm1_g[l][None, :], w_in[l].astype(BF16), seq)
    shp = lambda t: t.reshape(b, seq, t.shape[-1])
    wif_rows, bif_rows = _gate_rows(w_if_fwd[l], b_if_fwd[l], w_if_bwd[l], b_if_bwd[l])
    xc, qm, ktm, vm, gates = _mlstm_pre(
        shp(xm), conv_w[l], conv_b[l][None, :],
        _block_diag(wq_m[l]).astype(BF16), _block_diag(wk_m[l]).T.astype(BF16),
        _block_diag(wv_m[l]).astype(BF16), wif_rows.astype(BF16), bif_rows)
    gates = gates.reshape(b, 2, MLSTM_HEADS, 8, nc, MLSTM_CHUNK).transpose(0, 2, 1, 4, 3, 5)
    gates = gates.reshape(b, MLSTM_HEADS, 2, nc * 8, MLSTM_CHUNK)
    ym = _mlstm(qm, ktm, vm, gates, xc, shp(z), mlstm_norm_g[l][None, :],
                mlstm_skip[l][None, :])
    branches = [_band_attn(*qkv_views[3 * d:3 * d + 3], seq, win, dil)
                for d, (win, dil) in enumerate(DILATED_PATTERNS)]
    wr_pad = jnp.pad(w_router[l], ((0, 0), (0, LANES - N_EXPERTS)))
    x2, h2, logits = _out_proj(
        ym.reshape(b * seq, D_MLSTM), [o for o, _ in branches], [s for _, s in branches], x2d,
        attn_norm_g[l][None, :], w_out[l].astype(BF16), norm2_g[l][None, :], wr_pad)
    slot, slot_t, aff = _route(logits.reshape(b, seq, LANES), cap)
    xs = _moe_gather(slot_t.reshape(b, LANES, 1, seq), h2.reshape(b, seq, D_MODEL), cap)
    y = _moe_ffn(xs, w1[l], w3[l], w2[l])
    return _moe_scatter(slot, aff, y, x2.reshape(b, seq, D_MODEL), norm_f_g[None, :])
```

```python
import functools

import jax
import jax.numpy as jnp
from jax import lax
from jax.experimental import pallas as pl
from jax.experimental.pallas import tpu as pltpu

F32 = jnp.float32
BF16 = jnp.bfloat16

D_MODEL = 1024
D_MLSTM = 512
D_ATTN = 512
D_IN_PROJ = 2 * D_MLSTM + 3 * D_ATTN
MLSTM_HEADS = 4
MLSTM_HEAD_DIM = 128
MLSTM_QKV_BLOCK = 4
MLSTM_CONV = 5
ATTN_HEADS = 8
ATTN_HEAD_DIM = 64
ROPE_DIM = 16
ROPE_THETA = 500000.0
DILATED_PATTERNS = ((128, 1), (512, 4), (2048, 16))
N_EXPERTS = 16
EC_CAPACITY = 2
D_EXPERT = 2816
NORM_EPS = 1e-6
NEG_INF = -1e30

LANES = 128
MLSTM_CHUNK = 128
ROW_TILE = 512
OUT_PROJ_SPLITS = 2
ATTN_Q_TILE = 128
FFN_F_TILE = 256
FFN_N_TILE = 256
FFN_ROW_TILE = 512
THRESHOLD_RADIX_BITS = (4, 4, 4, 4, 4, 3, 4, 4, 4, 4)
VMEM_LIMIT = 56 * 1024 * 1024


def _params(sem):
    return pltpu.CompilerParams(dimension_semantics=sem, vmem_limit_bytes=VMEM_LIMIT)


def _rms(x, g):
    return x * lax.rsqrt(jnp.mean(x * x, axis=-1, keepdims=True) + NORM_EPS) * g


def _silu(x):
    return x * (1.0 / (1.0 + jnp.exp(-x)))


def _dot(a, b):
    return jnp.dot(a, b, preferred_element_type=F32)


def _dot_nt(a, b):
    return lax.dot_general(a, b, (((1,), (1,)), ((), ())), preferred_element_type=F32)


def _in_proj_kernel(x_ref, g_ref, w_ref, cos_ref, sa_ref, sb_ref, xm_ref, z_ref, *rest):
    qkv_refs, scr_ref = rest[:-1], rest[-1]
    h = _rms(x_ref[...], g_ref[...])
    p = _dot(h.astype(BF16), w_ref[...])
    xm_ref[...] = p[:, :D_MLSTM]
    z_ref[...] = p[:, D_MLSTM:2 * D_MLSTM]
    cos, sa, sb = cos_ref[...], sa_ref[...], sb_ref[...]
    half = ROPE_DIM // 2

    def rope(t):
        outs = []
        for j in range(D_ATTN // LANES):
            tj = t[:, j * LANES:(j + 1) * LANES]
            up = pltpu.roll(tj, LANES - half, axis=1)
            dn = pltpu.roll(tj, half, axis=1)
            outs.append(tj * cos + up * sa + dn * sb)
        return jnp.concatenate(outs, axis=1)

    o = 2 * D_MLSTM
    qkv = (rope(p[:, o:o + D_ATTN]) * (ATTN_HEAD_DIM ** -0.5),
           rope(p[:, o + D_ATTN:o + 2 * D_ATTN]),
           p[:, o + 2 * D_ATTN:])
    rows = p.shape[0]
    groups = D_ATTN // LANES
    for a, val in enumerate(qkv):
        for j in range(groups):
            scr_ref[j] = val[:, j * LANES:(j + 1) * LANES]
        for d, (_, dil) in enumerate(DILATED_PATTERNS):
            ref = qkv_refs[3 * d + a]
            if dil == 1:
                ref[...] = val.astype(BF16)
                continue
            for r in range(dil):
                for j in range(groups):
                    piece = scr_ref[j, pl.ds(r, rows // dil, stride=dil), :]
                    ref[:, r * D_ATTN + j * LANES:r * D_ATTN + (j + 1) * LANES] = piece.astype(BF16)


def _rope_tables(seq):
    half = ROPE_DIM // 2
    inv_freq = ROPE_THETA ** (-2.0 * jnp.arange(half, dtype=F32) / ROPE_DIM)
    ang = jnp.arange(seq).astype(F32)[:, None] * inv_freq[None, :]
    cos, sin = jnp.cos(ang), jnp.sin(ang)
    pad = jnp.zeros((seq, ATTN_HEAD_DIM - ROPE_DIM), F32)
    cos_h = jnp.concatenate([cos, cos, pad + 1.0], axis=1)
    sa_h = jnp.concatenate([-sin, jnp.zeros_like(sin), pad], axis=1)
    sb_h = jnp.concatenate([jnp.zeros_like(sin), sin, pad], axis=1)
    rep = LANES // ATTN_HEAD_DIM
    return tuple(jnp.tile(t, (1, rep)) for t in (cos_h, sa_h, sb_h))


def _in_proj(x2d, g, w_bf, seq):
    n = x2d.shape[0]
    tiles_per_seq = seq // ROW_TILE
    cos, sa, sb = _rope_tables(seq)
    row = lambda i: (i, 0)
    fixed = lambda i: (0, 0)
    pos = lambda i: (i % tiles_per_seq, 0)
    return pl.pallas_call(
        _in_proj_kernel,
        grid=(n // ROW_TILE,),
        in_specs=[
            pl.BlockSpec((ROW_TILE, D_MODEL), row),
            pl.BlockSpec((1, D_MODEL), fixed),
            pl.BlockSpec((D_MODEL, D_IN_PROJ), fixed),
            pl.BlockSpec((ROW_TILE, LANES), pos),
            pl.BlockSpec((ROW_TILE, LANES), pos),
            pl.BlockSpec((ROW_TILE, LANES), pos),
        ],
        out_specs=[
            pl.BlockSpec((ROW_TILE, D_MLSTM), row),
            pl.BlockSpec((ROW_TILE, D_MLSTM), row),
        ] + [pl.BlockSpec((ROW_TILE // dil, dil * D_ATTN), row)
             for _, dil in DILATED_PATTERNS for _ in range(3)],
        out_shape=[
            jax.ShapeDtypeStruct((n, D_MLSTM), F32),
            jax.ShapeDtypeStruct((n, D_MLSTM), F32),
        ] + [jax.ShapeDtypeStruct((n // dil, dil * D_ATTN), BF16)
             for _, dil in DILATED_PATTERNS for _ in range(3)],
        scratch_shapes=[pltpu.VMEM((D_ATTN // LANES, ROW_TILE, LANES), F32)],
        compiler_params=_params(("parallel",)),
        name="in_proj",
    )(x2d, g, w_bf, cos, sa, sb)


def _mlstm_pre_kernel(xm_ref, cw_ref, cb_ref, wq_ref, wkt_ref, wv_ref, wif_ref, bif_ref,
                      xc_ref, q_ref, kt_ref, v_ref, g_ref):
    x = xm_ref[...]
    seq = x.shape[0]
    t = lax.broadcasted_iota(jnp.int32, x.shape, 0)
    acc = jnp.zeros_like(x) + cb_ref[...]
    for j in range(MLSTM_CONV):
        d = j - MLSTM_CONV // 2
        if d == 0:
            tap = x
        else:
            tap = pltpu.roll(x, (-d) % seq, axis=0)
            tap = jnp.where((t + d >= 0) & (t + d < seq), tap, 0.0)
        acc = acc + tap * cw_ref[j:j + 1, :]
    xc = _silu(acc)
    xc_ref[...] = xc
    xcb = xc.astype(BF16)
    q = _dot(xcb, wq_ref[...]).astype(BF16)
    kt = (_dot_nt(wkt_ref[...], xcb) * (MLSTM_HEAD_DIM ** -0.5)).astype(BF16)
    v = _dot(x.astype(BF16), wv_ref[...]).astype(BF16)
    q_ref[...] = q
    v_ref[...] = v
    L = MLSTM_CHUNK
    for c in range(seq // L):
        kt_ref[c] = kt[:, c * L:(c + 1) * L]
    wif = wif_ref[...]
    g_ref[...] = (_dot_nt(wif[:, :D_MLSTM], q) + _dot(wif[:, D_MLSTM:2 * D_MLSTM], kt)
                  + _dot_nt(wif[:, 2 * D_MLSTM:], v) + bif_ref[...])


def _block_diag(w):
    nblk = w.shape[0]
    n = nblk * MLSTM_QKV_BLOCK
    tiled = jnp.tile(w.reshape(n, MLSTM_QKV_BLOCK), (1, nblk))
    blk = jnp.arange(n) // MLSTM_QKV_BLOCK
    return jnp.where(blk[:, None] == blk[None, :], tiled, 0.0)


def _gate_rows(w_f, b_f, w_b, b_b):
    h = MLSTM_HEADS

    def rows(f, b):
        pair = jnp.stack([f, b], axis=-1)
        pair = jnp.concatenate([pair[..., h:, :], pair[..., :h, :]], axis=-2)
        pair = jnp.pad(pair, [(0, 0)] * (pair.ndim - 1) + [(0, 6)])
        return pair.reshape(*pair.shape[:-2], 2 * h * 8)

    return rows(w_f, w_b).T, rows(b_f, b_b)[:, None]


def _mlstm_pre(xm, conv_w, conv_b, wq, wkt, wv, wif_rows, bif_rows):
    b, seq, _ = xm.shape
    nrow = wif_rows.shape[0]
    nc = seq // MLSTM_CHUNK
    per_b = lambda i: (i, 0, 0)
    fixed = lambda i: (0, 0)
    return pl.pallas_call(
        _mlstm_pre_kernel,
        grid=(b,),
        in_specs=[
            pl.BlockSpec((None, seq, D_MLSTM), per_b),
            pl.BlockSpec((MLSTM_CONV, D_MLSTM), fixed),
            pl.BlockSpec((1, D_MLSTM), fixed),
            pl.BlockSpec((D_MLSTM, D_MLSTM), fixed),
            pl.BlockSpec((D_MLSTM, D_MLSTM), fixed),
            pl.BlockSpec((D_MLSTM, D_MLSTM), fixed),
            pl.BlockSpec((nrow, 3 * D_MLSTM), fixed),
            pl.BlockSpec((nrow, 1), fixed),
        ],
        out_specs=[
            pl.BlockSpec((None, seq, D_MLSTM), per_b),
            pl.BlockSpec((None, seq, D_MLSTM), per_b),
            pl.BlockSpec((None, nc, D_MLSTM, MLSTM_CHUNK), lambda i: (i, 0, 0, 0)),
            pl.BlockSpec((None, seq, D_MLSTM), per_b),
            pl.BlockSpec((None, nrow, seq), per_b),
        ],
        out_shape=[
            jax.ShapeDtypeStruct((b, seq, D_MLSTM), F32),
            jax.ShapeDtypeStruct((b, seq, D_MLSTM), BF16),
            jax.ShapeDtypeStruct((b, nc, D_MLSTM, MLSTM_CHUNK), BF16),
            jax.ShapeDtypeStruct((b, seq, D_MLSTM), BF16),
            jax.ShapeDtypeStruct((b, nrow, seq), F32),
        ],
        compiler_params=_params(("parallel",)),
        name="mlstm_pre",
    )(xm, conv_w, conv_b, wq, wkt, wv, wif_rows, bif_rows)


def _log_sigmoid(x):
    return jnp.minimum(x, 0.0) - jnp.log1p(jnp.exp(-jnp.abs(x)))


def _split3(x):
    hi = x.astype(BF16).astype(F32)
    mid = (x - hi).astype(BF16).astype(F32)
    lo = (x - hi - mid).astype(BF16).astype(F32)
    return hi, mid, lo


def _mlstm_kernel(q_ref, kt_ref, v_ref, g_ref, xc_ref, z_ref, ng_ref, sk_ref, o_ref,
                  w_ref, ml_ref, tot_ref, pm_ref, tb_ref, tr_ref, dc_ref, cs_ref, ms_ref):
    L = MLSTM_CHUNK
    dh = MLSTM_HEAD_DIM
    nc = kt_ref.shape[0]
    rows_all = g_ref.shape[1]
    lane = lax.broadcasted_iota(jnp.int32, (rows_all, L), 1)
    sub = lax.broadcasted_iota(jnp.int32, (rows_all, L), 0) % 8
    fwd_row = sub == 0
    row_i = lax.broadcasted_iota(jnp.int32, (L, L), 0)
    col_i = lax.broadcasted_iota(jnp.int32, (L, L), 1)
    ones_col = jnp.ones((L, dh), BF16)

    lf = _log_sigmoid(g_ref[0])
    pre, suf = lf, lf
    d = 1
    while d < L:
        pre = pre + jnp.where(lane >= d, pltpu.roll(pre, d, axis=1), 0.0)
        suf = suf + jnp.where(lane < L - d, pltpu.roll(suf, L - d, axis=1), 0.0)
        d *= 2
    cum = jnp.where(fwd_row, pre, suf)
    tot = jnp.where(fwd_row, cum[:, L - 1:L], cum[:, 0:1])
    a = tot - cum + g_ref[1]
    ml = jnp.max(a, axis=1, keepdims=True)
    w_ref[...] = jnp.exp(a - ml)
    ml_ref[...] = jnp.broadcast_to(ml, (rows_all, L))
    tot_ref[...] = tot
    r = g_ref[1] - cum
    pmax, smax = r, r
    d = 1
    while d < L:
        pmax = jnp.maximum(pmax, jnp.where(lane >= d, pltpu.roll(pmax, d, axis=1), NEG_INF))
        smax = jnp.maximum(smax, jnp.where(lane < L - d, pltpu.roll(smax, L - d, axis=1), NEG_INF))
        d *= 2
    pm_ref[...] = jnp.where(fwd_row, pmax, smax)

    def tile_bcast(x, src):
        y = jnp.where(sub == src, x, 0.0)
        if src:
            y = pltpu.roll(y, rows_all - src, axis=0)
        for s in (1, 2, 4):
            y = y + pltpu.roll(y, s, axis=0)
        return y

    for dr in range(2):
        b_hi, b_mid, b_lo = _split3(tile_bcast(cum, dr))
        r_hi, r_mid, r_lo = _split3(tile_bcast(r, dr))
        tb_ref[dr] = jnp.where(sub == 1, b_hi, jnp.where(sub == 2, b_mid, jnp.where(
            sub == 3, b_lo, jnp.where((sub >= 4) & (sub <= 6), 1.0, 0.0))))
        tr_ref[dr] = jnp.where(sub == 0, 1.0, jnp.where(sub == 4, r_hi, jnp.where(
            sub == 5, r_mid, jnp.where(sub == 6, r_lo, 0.0))))

    def chunk_rows(ref, c):
        return ref[pl.ds(pl.multiple_of(c * 8, 8), 8), :]

    def v_aug(c):
        return jnp.concatenate([v_ref[pl.ds(c * L, L), :], ones_col], axis=1)

    def phase_a(c, carry):
        kt = kt_ref[c].astype(F32)
        w = chunk_rows(w_ref, c)
        kw = jnp.concatenate([kt * w[0:1], kt * w[1:2]], axis=0)
        dc_ref[c] = _dot(kw.astype(BF16), v_aug(c))
        return carry

    lax.fori_loop(0, nc, phase_a, 0, unroll=4)

    def scan_dir(direction_row, reverse):
        off_rows = direction_row * dh
        off_cols = direction_row * 2 * dh

        def body(i, carry):
            c = (nc - 1 - i) if reverse else i
            state, m = carry
            cs_ref[c, :, off_cols:off_cols + 2 * dh] = state.astype(BF16)
            ms_ref[c, direction_row:direction_row + 1, :] = m
            ml_c = chunk_rows(ml_ref, c)[direction_row:direction_row + 1]
            g_c = chunk_rows(tot_ref, c)[direction_row:direction_row + 1]
            m_new = jnp.maximum(g_c + m, ml_c)
            alpha = jnp.exp(g_c + m - m_new)
            beta = jnp.exp(ml_c - m_new)
            alpha2 = jnp.concatenate([alpha, alpha], axis=1)
            beta2 = jnp.concatenate([beta, beta], axis=1)
            state = alpha2 * state + beta2 * dc_ref[c, off_rows:off_rows + dh, :]
            return state, m_new

        init = (jnp.zeros((dh, 2 * dh), F32), jnp.zeros((1, L), F32))
        lax.fori_loop(0, nc, body, init)

    scan_dir(0, False)
    scan_dir(1, True)

    ng = ng_ref[...]
    sk = sk_ref[...]

    sub8 = lax.broadcasted_iota(jnp.int32, (8, L), 0)
    floor_rows = jnp.where(sub8 == 0, 1.0, jnp.where(sub8 <= 3, -1.0, 0.0))
    no_rows = jnp.zeros((8, 3 * L), F32)

    def direction(s_qk, qc, vaug, ex, keep):
        w = jnp.exp(jnp.where(keep, ex[:, :L], NEG_INF)) * s_qk
        scale = jnp.exp(ex[:, L:2 * L])
        intra = _dot(w.astype(BF16), vaug)
        tot_c = intra + jnp.concatenate([scale, scale], axis=1) * qc
        return tot_c[:, :dh] / jnp.maximum(jnp.abs(tot_c[:, dh:]), jnp.exp(ex[:, 2 * L:]))

    def phase_c(c, carry):
        rows = pl.ds(c * L, L)
        q = q_ref[rows, :]
        s_qk = _dot(q, kt_ref[c])
        qc = _dot(q, cs_ref[c])
        vaug = v_aug(c)
        ms = ms_ref[c]
        e = -jnp.maximum(ms, chunk_rows(pm_ref, c))
        e = (e - jnp.abs(e) * (2.0 ** -7)).astype(BF16).astype(F32)
        lhs_tiles, rhs_tiles = [], []
        for dr in range(2):
            e_rows = jnp.broadcast_to(e[dr:dr + 1], (8, L))
            lhs_tiles.append(jnp.where(sub8 == 0, e_rows, chunk_rows(tb_ref.at[dr], c)))
            m_hi, m_mid, m_lo = _split3(jnp.broadcast_to(ms[dr:dr + 1], (8, L)))
            scale_rows = jnp.where(sub8 == 0, 1.0, jnp.where(sub8 == 4, m_hi, jnp.where(
                sub8 == 5, m_mid, jnp.where(sub8 == 6, m_lo, 0.0))))
            rhs_tiles.append(jnp.concatenate(
                [chunk_rows(tr_ref.at[dr], c), scale_rows, floor_rows], axis=1))
        lhs = jnp.concatenate(lhs_tiles, axis=0).astype(BF16)
        rhs = jnp.concatenate([jnp.concatenate([rhs_tiles[0], no_rows], axis=1),
                               jnp.concatenate([no_rows, rhs_tiles[1]], axis=1)],
                              axis=0).astype(BF16)
        ex = lax.dot_general(lhs, rhs, (((0,), (0,)), ((), ())),
                             preferred_element_type=F32)
        h = (direction(s_qk, qc[:, :2 * dh], vaug, ex[:, :3 * L], col_i <= row_i)
             + direction(s_qk, qc[:, 2 * dh:], vaug, ex[:, 3 * L:], col_i >= row_i))
        hn = _rms(h, ng)
        o_ref[rows, :] = (hn + sk * xc_ref[rows, :]) * _silu(z_ref[rows, :])
        return carry

    lax.fori_loop(0, nc, phase_c, 0, unroll=8)


def _mlstm(q, kt, v, gates, xc, z, norm_g, skip):
    b, seq, _ = q.shape
    nc = seq // MLSTM_CHUNK
    dh = MLSTM_HEAD_DIM
    head = lambda i, j: (i, 0, j)
    vec = lambda i, j: (0, j)
    blk = pl.BlockSpec((None, seq, dh), head)
    gate_rows = pltpu.VMEM((nc * 8, MLSTM_CHUNK), F32)
    gate_tiles = pltpu.VMEM((2, nc * 8, MLSTM_CHUNK), F32)
    return pl.pallas_call(
        _mlstm_kernel,
        grid=(b, MLSTM_HEADS),
        in_specs=[
            blk,
            pl.BlockSpec((None, nc, dh, MLSTM_CHUNK), lambda i, j: (i, 0, j, 0)),
            blk,
            pl.BlockSpec((None, None, 2, nc * 8, MLSTM_CHUNK), lambda i, j: (i, j, 0, 0, 0)),
            blk, blk,
            pl.BlockSpec((1, dh), vec),
            pl.BlockSpec((1, dh), vec),
        ],
        out_specs=blk,
        out_shape=jax.ShapeDtypeStruct((b, seq, D_MLSTM), F32),
        scratch_shapes=[
            gate_rows, gate_rows, gate_rows, gate_rows, gate_tiles, gate_tiles,
            pltpu.VMEM((nc, 2 * dh, 2 * dh), F32),
            pltpu.VMEM((nc, dh, 4 * dh), BF16),
            pltpu.VMEM((nc, 8, MLSTM_CHUNK), F32),
        ],
        compiler_params=_params(("parallel", "parallel")),
        name="mlstm",
    )(q, kt, v, gates, xc, z, norm_g, skip)


def _band_attn_kernel(q_ref, k_ref, v_ref, o_ref, lse_ref, *, half, dil):
    lsub = q_ref.shape[0]
    tq = ATTN_Q_TILE
    win = min(lsub, 2 * tq)
    first = lax.broadcasted_iota(jnp.int32, (1, LANES), 1) < ATTN_HEAD_DIM
    lane = lax.broadcasted_iota(jnp.int32, (tq, LANES), 1)
    rel = (lax.broadcasted_iota(jnp.int32, (2 * tq, win), 1)
           - lax.broadcasted_iota(jnp.int32, (2 * tq, win), 0) % tq)

    def tile(qs, ws, out_rows):
        keep = jnp.abs(rel + (ws - qs)) <= half
        for c in range(dil):
            lse_tile = jnp.zeros((tq, LANES), F32)
            for p in range(D_ATTN // LANES):
                lanes = slice(c * D_ATTN + p * LANES, c * D_ATTN + (p + 1) * LANES)
                q = q_ref[pl.ds(qs, tq), lanes]
                kw = k_ref[pl.ds(ws, win), lanes]
                vw = v_ref[pl.ds(ws, win), lanes]
                zero = jnp.zeros_like(q)
                q2 = jnp.concatenate([jnp.where(first, q, zero), jnp.where(first, zero, q)], axis=0)
                s = jnp.where(keep, _dot_nt(q2, kw), NEG_INF)
                m = jnp.max(s, axis=1, keepdims=True)
                e = jnp.exp(s - m)
                l = jnp.sum(e, axis=1, keepdims=True)
                o2 = _dot(e.astype(BF16), vw) / l
                o_ref[p, out_rows(c), :] = jnp.where(first, o2[:tq], o2[tq:])
                lse2 = m + jnp.log(l)
                lse_tile = jnp.where(lane == 2 * p, lse2[:tq],
                                     jnp.where(lane == 2 * p + 1, lse2[tq:], lse_tile))
            lse_ref[out_rows(c), :] = lse_tile

    if dil == 1:
        def body(t, carry):
            qs = pl.multiple_of(t * tq, tq)
            ws = pl.multiple_of(jnp.clip(qs - half, 0, lsub - win), half)
            tile(qs, ws, lambda c: pl.ds(qs, tq))
            return carry

        lax.fori_loop(0, lsub // tq, body, 0, unroll=2)
    else:
        for t in range(lsub // tq):
            qs = t * tq
            ws = min(max(qs - half, 0), lsub - win)
            tile(qs, ws, lambda c, qs=qs: pl.ds(qs * dil + c, tq, stride=dil))


def _band_attn(q, k, v, seq, win, dil):
    lsub = seq // dil
    b = q.shape[0] // lsub
    half = win // (2 * dil)
    pairs = D_ATTN // LANES
    assert lsub % ATTN_Q_TILE == 0 and half % 16 == 0 and ATTN_Q_TILE + 2 * half <= 2 * ATTN_Q_TILE
    blk = pl.BlockSpec((lsub, dil * D_ATTN), lambda i: (i, 0))
    return pl.pallas_call(
        functools.partial(_band_attn_kernel, half=half, dil=dil),
        grid=(b,),
        in_specs=[blk, blk, blk],
        out_specs=[pl.BlockSpec((None, pairs, seq, LANES), lambda i: (i, 0, 0, 0)),
                   pl.BlockSpec((seq, LANES), lambda i: (i, 0))],
        out_shape=[
            jax.ShapeDtypeStruct((b, pairs, seq, LANES), F32),
            jax.ShapeDtypeStruct((b * seq, LANES), F32),
        ],
        compiler_params=_params(("parallel",)),
        name=f"band_attn_d{dil}",
    )(q, k, v)


def _out_proj_kernel(ym_ref, o1_ref, o2_ref, o3_ref, l1_ref, l2_ref, l3_ref, sp_ref, x_ref, ag_ref,
                     w_ref, n2_ref, wr_ref, x2_ref, h2_ref, lg_ref):
    spread = sp_ref[...]
    wr = wr_ref[...]
    w_hi = wr.astype(BF16)
    w_lo = (wr - w_hi.astype(F32)).astype(BF16)
    wr3 = jnp.concatenate([w_hi, w_hi, w_lo], axis=0)

    def per_lane(w):
        hi = w.astype(BF16)
        lo = (w - hi.astype(F32)).astype(BF16)
        return _dot(jnp.concatenate([hi, lo], axis=1), spread)

    rows = x_ref.shape[0] // OUT_PROJ_SPLITS
    for part in range(OUT_PROJ_SPLITS):
        rs = slice(part * rows, (part + 1) * rows)
        lses = [r[rs, :] for r in (l1_ref, l2_ref, l3_ref)]
        top = jnp.maximum(jnp.maximum(lses[0], lses[1]), lses[2])
        wts = [jnp.exp(l - top) for l in lses]
        total = wts[0] + wts[1] + wts[2]

        def heads(o_ref):
            return jnp.concatenate([o_ref[p, rs, :] for p in range(o_ref.shape[0])], axis=1)

        ya = sum(per_lane(w / total) * heads(o) for w, o in zip(wts, (o1_ref, o2_ref, o3_ref)))
        ya = _rms(ya, ag_ref[...])
        mixed = jnp.concatenate([ym_ref[rs, :], ya], axis=1).astype(BF16)
        x2 = x_ref[rs, :] + _dot(mixed, w_ref[...])
        x2_ref[rs, :] = x2
        h2 = _rms(x2, n2_ref[...])
        hi = h2.astype(BF16)
        h2_ref[rs, :] = hi
        lo = (h2 - hi.astype(F32)).astype(BF16)
        lg_ref[rs, :] = _dot(jnp.concatenate([hi, lo, hi], axis=1), wr3)


def _out_proj(ym, branch_o, branch_lse, x2d, attn_g, w_bf, n2g, wr_pad):
    n = x2d.shape[0]
    spread = (jnp.arange(LANES)[:, None] == jnp.arange(D_ATTN)[None, :] // ATTN_HEAD_DIM)
    spread = jnp.tile(spread.astype(BF16), (2, 1))
    pairs, seq = branch_o[0].shape[1:3]
    tiles_per_seq = seq // ROW_TILE
    branch = pl.BlockSpec((None, pairs, ROW_TILE, LANES),
                          lambda i: (i // tiles_per_seq, 0, i % tiles_per_seq, 0))
    row = lambda i: (i, 0)
    fixed = lambda i: (0, 0)
    return pl.pallas_call(
        _out_proj_kernel,
        grid=(n // ROW_TILE,),
        in_specs=[
            pl.BlockSpec((ROW_TILE, D_MLSTM), row),
            branch, branch, branch,
            pl.BlockSpec((ROW_TILE, LANES), row),
            pl.BlockSpec((ROW_TILE, LANES), row),
            pl.BlockSpec((ROW_TILE, LANES), row),
            pl.BlockSpec((2 * LANES, D_ATTN), fixed),
            pl.BlockSpec((ROW_TILE, D_MODEL), row),
            pl.BlockSpec((1, D_ATTN), fixed),
            pl.BlockSpec((D_MODEL, D_MODEL), fixed),
            pl.BlockSpec((1, D_MODEL), fixed),
            pl.BlockSpec((D_MODEL, LANES), fixed),
        ],
        out_specs=[
            pl.BlockSpec((ROW_TILE, D_MODEL), row),
            pl.BlockSpec((ROW_TILE, D_MODEL), row),
            pl.BlockSpec((ROW_TILE, LANES), row),
        ],
        out_shape=[
            jax.ShapeDtypeStruct((n, D_MODEL), F32),
            jax.ShapeDtypeStruct((n, D_MODEL), BF16),
            jax.ShapeDtypeStruct((n, LANES), F32),
        ],
        compiler_params=_params(("parallel",)),
        name="out_proj",
    )(ym, *branch_o, *branch_lse, spread, x2d, attn_g, w_bf, n2g, wr_pad)


def _route_kernel(lg_ref, tri_ref, eye_ref, slot_ref, slot_t_ref, aff_ref, *, cap):
    lg = lg_ref[...]
    valid = lax.broadcasted_iota(jnp.int32, (1, LANES), 1) < N_EXPERTS
    lg = jnp.where(valid, lg, NEG_INF)
    e = jnp.exp(lg - jnp.max(lg, axis=1, keepdims=True))
    aff = e / jnp.sum(e, axis=1, keepdims=True)
    aff_ref[...] = aff
    groups = LANES // N_EXPERTS
    rpg = aff.shape[0] // groups
    lane = lax.broadcasted_iota(jnp.int32, (1, LANES), 1)
    packed = aff[:rpg]
    for g in range(1, groups):
        packed = packed + pltpu.roll(aff[g * rpg:(g + 1) * rpg], g * N_EXPERTS, axis=1)

    def over_groups(x):
        shift = N_EXPERTS
        while shift < LANES:
            x = x + pltpu.roll(x, shift, axis=1)
            shift *= 2
        return x

    def enough(cand):
        part = jnp.sum(jnp.where(packed >= cand, 1.0, 0.0), axis=0, keepdims=True)
        return over_groups(part) >= cap

    def narrow(lo, hi, cands):
        new_lo, new_hi = lo, hi
        for cand in cands:
            ok = enough(cand)
            new_lo = jnp.maximum(new_lo, jnp.where(ok, cand, lo))
            new_hi = jnp.minimum(new_hi, jnp.where(ok, hi, cand))
        return new_lo, new_hi

    tiny = jnp.full((1, LANES), 2.0 ** -126, F32)
    normal = enough(tiny)
    p = tiny
    for span, count in ((16, 7), (1, 15)):
        p, _ = narrow(p, p, [p * (2.0 ** (span * j)) for j in range(1, count + 1)])
    lo = jnp.where(normal, p, 0.0)
    hi = jnp.where(normal, p * 2.0, tiny)
    width = jnp.where(normal, p, 0.0)
    for bits in THRESHOLD_RADIX_BITS:
        width = width * (0.5 ** bits)
        lo, hi = narrow(lo, hi, [lo + j * width for j in range(1, 2 ** bits)])
    gt = jnp.where(packed >= hi, 1.0, 0.0)
    eq = jnp.where(packed >= lo, 1.0, 0.0) - gt
    need = cap - over_groups(jnp.sum(gt, axis=0, keepdims=True))
    tri = tri_ref[...]

    def count_before(x):
        per_group = jnp.sum(x, axis=0, keepdims=True)
        upto = per_group
        shift = N_EXPERTS
        while shift < LANES:
            upto = upto + jnp.where(lane >= shift, pltpu.roll(upto, shift, axis=1), 0.0)
            shift *= 2
        return _dot(tri, x.astype(BF16)) + (upto - per_group)

    sel = gt + eq * jnp.where(count_before(eq) < need, 1.0, 0.0)
    pos = count_before(sel)
    slot_packed = jnp.where(sel > 0.0, pos, -1.0)
    slot = jnp.concatenate(
        [jnp.where(valid, slot_packed if g == 0 else
                   pltpu.roll(slot_packed, LANES - g * N_EXPERTS, axis=1), -1.0)
         for g in range(groups)], axis=0)
    slot_ref[...] = slot
    slot_t_ref[...] = _dot_nt(eye_ref[...], slot.astype(BF16))


def _route(logits, cap):
    b, seq, _ = logits.shape
    assert LANES % N_EXPERTS == 0 and seq % (LANES // N_EXPERTS) == 0
    rpg = seq // (LANES // N_EXPERTS)
    tri = (jnp.arange(rpg)[None, :] < jnp.arange(rpg)[:, None]).astype(BF16)
    eye = jnp.eye(LANES, dtype=BF16)
    per_b = lambda i: (i, 0, 0)
    fixed = lambda i: (0, 0)
    return pl.pallas_call(
        functools.partial(_route_kernel, cap=cap),
        grid=(b,),
        in_specs=[
            pl.BlockSpec((None, seq, LANES), per_b),
            pl.BlockSpec((rpg, rpg), fixed),
            pl.BlockSpec((LANES, LANES), fixed),
        ],
        out_specs=[
            pl.BlockSpec((None, seq, LANES), per_b),
            pl.BlockSpec((None, LANES, seq), per_b),
            pl.BlockSpec((None, seq, LANES), per_b),
        ],
        out_shape=[
            jax.ShapeDtypeStruct((b, seq, LANES), F32),
            jax.ShapeDtypeStruct((b, LANES, seq), F32),
            jax.ShapeDtypeStruct((b, seq, LANES), F32),
        ],
        compiler_params=_params(("parallel",)),
        name="route",
    )(logits, tri, eye)


def _moe_gather_kernel(slot_ref, h_ref, xs_ref):
    srow = slot_ref[...]
    cap, seq = xs_ref.shape[0], srow.shape[1]
    ci = lax.broadcasted_iota(jnp.int32, (cap, seq), 0).astype(F32)
    onehot = jnp.where(srow == ci, 1.0, 0.0).astype(BF16)
    xs_ref[...] = _dot(onehot, h_ref[...]).astype(BF16)


def _moe_gather(slot_t, h2, cap):
    b, seq, _ = h2.shape
    return pl.pallas_call(
        _moe_gather_kernel,
        grid=(b, N_EXPERTS),
        in_specs=[
            pl.BlockSpec((None, None, 1, seq), lambda i, e: (i, e, 0, 0)),
            pl.BlockSpec((None, seq, D_MODEL), lambda i, e: (i, 0, 0)),
        ],
        out_specs=pl.BlockSpec((None, None, cap, D_MODEL), lambda i, e: (i, e, 0, 0)),
        out_shape=jax.ShapeDtypeStruct((b, N_EXPERTS, cap, D_MODEL), BF16),
        compiler_params=_params(("parallel", "parallel")),
        name="moe_gather",
    )(slot_t, h2)


def _moe_ffn_kernel(xs_ref, w1_ref, w3_ref, w2_ref, y_ref, act_ref, w1b_ref, w3b_ref, w2b_ref):
    s = pl.program_id(1)
    nb, cap, _ = xs_ref.shape
    nf = act_ref.shape[0]
    per = FFN_ROW_TILE // cap
    row_tiles = nb // per

    @pl.when(s < nf)
    def _():
        w1b_ref[...] = w1_ref[...].astype(BF16)
        w3b_ref[...] = w3_ref[...].astype(BF16)
        for r in range(row_tiles):
            x = xs_ref[r * per:(r + 1) * per].reshape(FFN_ROW_TILE, D_MODEL)
            up = _dot(x, w1b_ref[...])
            gt = _dot(x, w3b_ref[...])
            act_ref[s, r * FFN_ROW_TILE:(r + 1) * FFN_ROW_TILE, :] = (_silu(up) * gt).astype(BF16)

    @pl.when(s >= nf)
    def _():
        w2b_ref[...] = w2_ref[...].astype(BF16)
        for r in range(row_tiles):
            rows = slice(r * FFN_ROW_TILE, (r + 1) * FFN_ROW_TILE)
            act = jnp.concatenate([act_ref[f, rows, :] for f in range(nf)], axis=1)
            y = _dot(act, w2b_ref[...])
            y_ref[r * per:(r + 1) * per] = y.astype(BF16).reshape(per, cap, y.shape[1])


def _moe_ffn(xs, w1, w3, w2):
    b, ne, cap, _ = xs.shape
    nf = D_EXPERT // FFN_F_TILE
    nn = D_MODEL // FFN_N_TILE
    hidden = lambda e, s: (e, 0, jnp.minimum(s, nf - 1))
    out_col = lambda e, s: jnp.maximum(s - nf, 0)
    return pl.pallas_call(
        _moe_ffn_kernel,
        grid=(ne, nf + nn),
        in_specs=[
            pl.BlockSpec((b, None, cap, D_MODEL), lambda e, s: (0, e, 0, 0)),
            pl.BlockSpec((None, D_MODEL, FFN_F_TILE), hidden),
            pl.BlockSpec((None, D_MODEL, FFN_F_TILE), hidden),
            pl.BlockSpec((None, D_EXPERT, FFN_N_TILE), lambda e, s: (e, 0, out_col(e, s))),
        ],
        out_specs=pl.BlockSpec((b, None, cap, FFN_N_TILE), lambda e, s: (0, e, 0, out_col(e, s))),
        out_shape=jax.ShapeDtypeStruct(xs.shape, BF16),
        scratch_shapes=[
            pltpu.VMEM((nf, b * cap, FFN_F_TILE), BF16),
            pltpu.VMEM((D_MODEL, FFN_F_TILE), BF16),
            pltpu.VMEM((D_MODEL, FFN_F_TILE), BF16),
            pltpu.VMEM((D_EXPERT, FFN_N_TILE), BF16),
        ],
        compiler_params=_params(("parallel", "arbitrary")),
        name="moe_ffn",
    )(xs, w1, w3, w2)


def _moe_scatter_kernel(slot_ref, aff_ref, y_ref, x2_ref, g_ref, o_ref):
    slot = slot_ref[...]
    aff = aff_ref[...]
    rows, cap = slot.shape[0], y_ref.shape[1]
    ci = lax.broadcasted_iota(jnp.int32, (rows, cap), 1).astype(F32)
    acc = x2_ref[...]
    for e in range(N_EXPERTS):
        onehot = jnp.where(slot[:, e:e + 1] == ci, 1.0, 0.0).astype(BF16)
        acc = acc + aff[:, e:e + 1] * _dot(onehot, y_ref[e])
    o_ref[...] = _rms(acc, g_ref[...])


def _moe_scatter(slot, aff, y, x2, norm_g):
    b, seq, _ = x2.shape
    cap = y.shape[2]
    tile = lambda i, r: (i, r, 0)
    return pl.pallas_call(
        _moe_scatter_kernel,
        grid=(b, seq // ROW_TILE),
        in_specs=[
            pl.BlockSpec((None, ROW_TILE, LANES), tile),
            pl.BlockSpec((None, ROW_TILE, LANES), tile),
            pl.BlockSpec((None, N_EXPERTS, cap, D_MODEL), lambda i, r: (i, 0, 0, 0)),
            pl.BlockSpec((None, ROW_TILE, D_MODEL), tile),
            pl.BlockSpec((1, D_MODEL), lambda i, r: (0, 0)),
        ],
        out_specs=pl.BlockSpec((None, ROW_TILE, D_MODEL), tile),
        out_shape=jax.ShapeDtypeStruct((b, seq, D_MODEL), F32),
        compiler_params=_params(("parallel", "parallel")),
        name="moe_scatter",
    )(slot, aff, y, x2, norm_g)


def kernel(x, norm1_g, w_in, conv_w, conv_b, wq_m, wk_m, wv_m, w_if_fwd, b_if_fwd,
           w_if_bwd, b_if_bwd, mlstm_norm_g, mlstm_skip, attn_norm_g, w_out, norm2_g,
           w_router, w1, w3, w2, norm_f_g):
    b, seq, _ = x.shape
    assert w_in.shape[0] == 1, "single-layer problem"
    assert seq % ROW_TILE == 0 and seq % MLSTM_CHUNK == 0 and seq % ATTN_Q_TILE == 0
    cap = EC_CAPACITY * seq // N_EXPERTS
    assert FFN_ROW_TILE % cap == 0 and (b * cap) % FFN_ROW_TILE == 0
    nc = seq // MLSTM_CHUNK
    l = 0
    x2d = x.reshape(b * seq, D_MODEL)
    xm, z, *qkv_views = _in_proj(x2d, norm1_g[l][None, :], w_in[l].astype(BF16), seq)
    shp = lambda t: t.reshape(b, seq, t.shape[-1])
    wif_rows, bif_rows = _gate_rows(w_if_fwd[l], b_if_fwd[l], w_if_bwd[l], b_if_bwd[l])
    xc, qm, ktm, vm, gates = _mlstm_pre(
        shp(xm), conv_w[l], conv_b[l][None, :],
        _block_diag(wq_m[l]).astype(BF16), _block_diag(wk_m[l]).T.astype(BF16),
        _block_diag(wv_m[l]).astype(BF16), wif_rows.astype(BF16), bif_rows)
    gates = gates.reshape(b, 2, MLSTM_HEADS, 8, nc, MLSTM_CHUNK).transpose(0, 2, 1, 4, 3, 5)
    gates = gates.reshape(b, MLSTM_HEADS, 2, nc * 8, MLSTM_CHUNK)
    ym = _mlstm(qm, ktm, vm, gates, xc, shp(z), mlstm_norm_g[l][None, :],
                mlstm_skip[l][None, :])
    branches = [_band_attn(*qkv_views[3 * d:3 * d + 3], seq, win, dil)
                for d, (win, dil) in enumerate(DILATED_PATTERNS)]
    wr_pad = jnp.pad(w_router[l], ((0, 0), (0, LANES - N_EXPERTS)))
    x2, h2, logits = _out_proj(
        ym.reshape(b * seq, D_MLSTM), [o for o, _ in branches], [s for _, s in branches], x2d,
        attn_norm_g[l][None, :], w_out[l].astype(BF16), norm2_g[l][None, :], wr_pad)
    slot, slot_t, aff = _route(logits.reshape(b, seq, LANES), cap)
    xs = _moe_gather(slot_t.reshape(b, LANES, 1, seq), h2.reshape(b, seq, D_MODEL), cap)
    y = _moe_ffn(xs, w1[l], w3[l], w2[l])
    return _moe_scatter(slot, aff, y, x2.reshape(b, seq, D_MODEL), norm_f_g[None, :])
```

```python
import functools

import jax
import jax.numpy as jnp
from jax import lax
from jax.experimental import pallas as pl
from jax.experimental.pallas import tpu as pltpu

F32 = jnp.float32
BF16 = jnp.bfloat16

D_MODEL = 1024
D_MLSTM = 512
D_ATTN = 512
D_IN_PROJ = 2 * D_MLSTM + 3 * D_ATTN
MLSTM_HEADS = 4
MLSTM_HEAD_DIM = 128
MLSTM_QKV_BLOCK = 4
MLSTM_CONV = 5
ATTN_HEADS = 8
ATTN_HEAD_DIM = 64
ROPE_DIM = 16
ROPE_THETA = 500000.0
DILATED_PATTERNS = ((128, 1), (512, 4), (2048, 16))
N_EXPERTS = 16
EC_CAPACITY = 2
D_EXPERT = 2816
NORM_EPS = 1e-6
NEG_INF = -1e30

LANES = 128
MLSTM_CHUNK = 128
ROW_TILE = 512
OUT_PROJ_SPLITS = 2
ROUTE_BATCHES = 2
ATTN_Q_TILE = 128
FFN_F_TILE = 256
FFN_N_TILE = 256
FFN_ROW_TILE = 512
THRESHOLD_RADIX_BITS = (4, 4, 4, 4, 4, 3, 4, 4, 4, 4)
VMEM_LIMIT = 56 * 1024 * 1024


def _params(sem):
    return pltpu.CompilerParams(dimension_semantics=sem, vmem_limit_bytes=VMEM_LIMIT)


def _rms(x, g):
    return x * lax.rsqrt(jnp.mean(x * x, axis=-1, keepdims=True) + NORM_EPS) * g


def _silu(x):
    return x * (1.0 / (1.0 + jnp.exp(-x)))


def _dot(a, b):
    return jnp.dot(a, b, preferred_element_type=F32)


def _dot_nt(a, b):
    return lax.dot_general(a, b, (((1,), (1,)), ((), ())), preferred_element_type=F32)


def _in_proj_kernel(x_ref, g_ref, w_ref, cos_ref, sa_ref, sb_ref, xm_ref, z_ref, *rest):
    qkv_refs, scr_ref = rest[:-1], rest[-1]
    h = _rms(x_ref[...], g_ref[...])
    p = _dot(h.astype(BF16), w_ref[...])
    xm_ref[...] = p[:, :D_MLSTM]
    z_ref[...] = p[:, D_MLSTM:2 * D_MLSTM]
    cos, sa, sb = cos_ref[...], sa_ref[...], sb_ref[...]
    half = ROPE_DIM // 2

    def rope(t):
        outs = []
        for j in range(D_ATTN // LANES):
            tj = t[:, j * LANES:(j + 1) * LANES]
            up = pltpu.roll(tj, LANES - half, axis=1)
            dn = pltpu.roll(tj, half, axis=1)
            outs.append(tj * cos + up * sa + dn * sb)
        return jnp.concatenate(outs, axis=1)

    o = 2 * D_MLSTM
    qkv = (rope(p[:, o:o + D_ATTN]) * (ATTN_HEAD_DIM ** -0.5),
           rope(p[:, o + D_ATTN:o + 2 * D_ATTN]),
           p[:, o + 2 * D_ATTN:])
    rows = p.shape[0]
    groups = D_ATTN // LANES
    for a, val in enumerate(qkv):
        for j in range(groups):
            scr_ref[j] = val[:, j * LANES:(j + 1) * LANES]
        for d, (_, dil) in enumerate(DILATED_PATTERNS):
            ref = qkv_refs[3 * d + a]
            if dil == 1:
                ref[...] = val.astype(BF16)
                continue
            for r in range(dil):
                for j in range(groups):
                    piece = scr_ref[j, pl.ds(r, rows // dil, stride=dil), :]
                    ref[:, r * D_ATTN + j * LANES:r * D_ATTN + (j + 1) * LANES] = piece.astype(BF16)


def _rope_tables(seq):
    half = ROPE_DIM // 2
    inv_freq = ROPE_THETA ** (-2.0 * jnp.arange(half, dtype=F32) / ROPE_DIM)
    ang = jnp.arange(seq).astype(F32)[:, None] * inv_freq[None, :]
    cos, sin = jnp.cos(ang), jnp.sin(ang)
    pad = jnp.zeros((seq, ATTN_HEAD_DIM - ROPE_DIM), F32)
    cos_h = jnp.concatenate([cos, cos, pad + 1.0], axis=1)
    sa_h = jnp.concatenate([-sin, jnp.zeros_like(sin), pad], axis=1)
    sb_h = jnp.concatenate([jnp.zeros_like(sin), sin, pad], axis=1)
    rep = LANES // ATTN_HEAD_DIM
    return tuple(jnp.tile(t, (1, rep)) for t in (cos_h, sa_h, sb_h))


def _in_proj(x2d, g, w_bf, seq):
    n = x2d.shape[0]
    tiles_per_seq = seq // ROW_TILE
    cos, sa, sb = _rope_tables(seq)
    row = lambda i: (i, 0)
    fixed = lambda i: (0, 0)
    pos = lambda i: (i % tiles_per_seq, 0)
    return pl.pallas_call(
        _in_proj_kernel,
        grid=(n // ROW_TILE,),
        in_specs=[
            pl.BlockSpec((ROW_TILE, D_MODEL), row),
            pl.BlockSpec((1, D_MODEL), fixed),
            pl.BlockSpec((D_MODEL, D_IN_PROJ), fixed),
            pl.BlockSpec((ROW_TILE, LANES), pos),
            pl.BlockSpec((ROW_TILE, LANES), pos),
            pl.BlockSpec((ROW_TILE, LANES), pos),
        ],
        out_specs=[
            pl.BlockSpec((ROW_TILE, D_MLSTM), row),
            pl.BlockSpec((ROW_TILE, D_MLSTM), row),
        ] + [pl.BlockSpec((ROW_TILE // dil, dil * D_ATTN), row)
             for _, dil in DILATED_PATTERNS for _ in range(3)],
        out_shape=[
            jax.ShapeDtypeStruct((n, D_MLSTM), F32),
            jax.ShapeDtypeStruct((n, D_MLSTM), F32),
        ] + [jax.ShapeDtypeStruct((n // dil, dil * D_ATTN), BF16)
             for _, dil in DILATED_PATTERNS for _ in range(3)],
        scratch_shapes=[pltpu.VMEM((D_ATTN // LANES, ROW_TILE, LANES), F32)],
        compiler_params=_params(("parallel",)),
        name="in_proj",
    )(x2d, g, w_bf, cos, sa, sb)


def _mlstm_pre_kernel(xm_ref, cw_ref, cb_ref, wq_ref, wkt_ref, wv_ref, wif_ref, bif_ref,
                      xc_ref, q_ref, kt_ref, v_ref, g_ref):
    x = xm_ref[...]
    seq = x.shape[0]
    t = lax.broadcasted_iota(jnp.int32, x.shape, 0)
    acc = jnp.zeros_like(x) + cb_ref[...]
    for j in range(MLSTM_CONV):
        d = j - MLSTM_CONV // 2
        if d == 0:
            tap = x
        else:
            tap = pltpu.roll(x, (-d) % seq, axis=0)
            tap = jnp.where((t + d >= 0) & (t + d < seq), tap, 0.0)
        acc = acc + tap * cw_ref[j:j + 1, :]
    xc = _silu(acc)
    xc_ref[...] = xc
    xcb = xc.astype(BF16)
    q = _dot(xcb, wq_ref[...]).astype(BF16)
    kt = (_dot_nt(wkt_ref[...], xcb) * (MLSTM_HEAD_DIM ** -0.5)).astype(BF16)
    v = _dot(x.astype(BF16), wv_ref[...]).astype(BF16)
    q_ref[...] = q
    v_ref[...] = v
    L = MLSTM_CHUNK
    for c in range(seq // L):
        kt_ref[c] = kt[:, c * L:(c + 1) * L]
    wif = wif_ref[...]
    g_ref[...] = (_dot_nt(wif[:, :D_MLSTM], q) + _dot(wif[:, D_MLSTM:2 * D_MLSTM], kt)
                  + _dot_nt(wif[:, 2 * D_MLSTM:], v) + bif_ref[...])


def _block_diag(w):
    nblk = w.shape[0]
    n = nblk * MLSTM_QKV_BLOCK
    tiled = jnp.tile(w.reshape(n, MLSTM_QKV_BLOCK), (1, nblk))
    blk = jnp.arange(n) // MLSTM_QKV_BLOCK
    return jnp.where(blk[:, None] == blk[None, :], tiled, 0.0)


def _gate_rows(w_f, b_f, w_b, b_b):
    h = MLSTM_HEADS

    def rows(f, b):
        pair = jnp.stack([f, b], axis=-1)
        pair = jnp.concatenate([pair[..., h:, :], pair[..., :h, :]], axis=-2)
        pair = jnp.pad(pair, [(0, 0)] * (pair.ndim - 1) + [(0, 6)])
        return pair.reshape(*pair.shape[:-2], 2 * h * 8)

    return rows(w_f, w_b).T, rows(b_f, b_b)[:, None]


def _mlstm_pre(xm, conv_w, conv_b, wq, wkt, wv, wif_rows, bif_rows):
    b, seq, _ = xm.shape
    nrow = wif_rows.shape[0]
    nc = seq // MLSTM_CHUNK
    per_b = lambda i: (i, 0, 0)
    fixed = lambda i: (0, 0)
    return pl.pallas_call(
        _mlstm_pre_kernel,
        grid=(b,),
        in_specs=[
            pl.BlockSpec((None, seq, D_MLSTM), per_b),
            pl.BlockSpec((MLSTM_CONV, D_MLSTM), fixed),
            pl.BlockSpec((1, D_MLSTM), fixed),
            pl.BlockSpec((D_MLSTM, D_MLSTM), fixed),
            pl.BlockSpec((D_MLSTM, D_MLSTM), fixed),
            pl.BlockSpec((D_MLSTM, D_MLSTM), fixed),
            pl.BlockSpec((nrow, 3 * D_MLSTM), fixed),
            pl.BlockSpec((nrow, 1), fixed),
        ],
        out_specs=[
            pl.BlockSpec((None, seq, D_MLSTM), per_b),
            pl.BlockSpec((None, seq, D_MLSTM), per_b),
            pl.BlockSpec((None, nc, D_MLSTM, MLSTM_CHUNK), lambda i: (i, 0, 0, 0)),
            pl.BlockSpec((None, seq, D_MLSTM), per_b),
            pl.BlockSpec((None, nrow, seq), per_b),
        ],
        out_shape=[
            jax.ShapeDtypeStruct((b, seq, D_MLSTM), F32),
            jax.ShapeDtypeStruct((b, seq, D_MLSTM), BF16),
            jax.ShapeDtypeStruct((b, nc, D_MLSTM, MLSTM_CHUNK), BF16),
            jax.ShapeDtypeStruct((b, seq, D_MLSTM), BF16),
            jax.ShapeDtypeStruct((b, nrow, seq), F32),
        ],
        compiler_params=_params(("parallel",)),
        name="mlstm_pre",
    )(xm, conv_w, conv_b, wq, wkt, wv, wif_rows, bif_rows)


def _log_sigmoid(x):
    return jnp.minimum(x, 0.0) - jnp.log1p(jnp.exp(-jnp.abs(x)))


def _split3(x):
    hi = x.astype(BF16).astype(F32)
    mid = (x - hi).astype(BF16).astype(F32)
    lo = (x - hi - mid).astype(BF16).astype(F32)
    return hi, mid, lo


def _mlstm_kernel(q_ref, kt_ref, v_ref, g_ref, xc_ref, z_ref, ng_ref, sk_ref, o_ref,
                  w_ref, ml_ref, tot_ref, pm_ref, tb_ref, tr_ref, dc_ref, cs_ref, ms_ref):
    L = MLSTM_CHUNK
    dh = MLSTM_HEAD_DIM
    nc = kt_ref.shape[0]
    rows_all = g_ref.shape[1]
    lane = lax.broadcasted_iota(jnp.int32, (rows_all, L), 1)
    sub = lax.broadcasted_iota(jnp.int32, (rows_all, L), 0) % 8
    fwd_row = sub == 0
    row_i = lax.broadcasted_iota(jnp.int32, (L, L), 0)
    col_i = lax.broadcasted_iota(jnp.int32, (L, L), 1)
    ones_col = jnp.ones((L, dh), BF16)

    lf = _log_sigmoid(g_ref[0])
    pre, suf = lf, lf
    d = 1
    while d < L:
        pre = pre + jnp.where(lane >= d, pltpu.roll(pre, d, axis=1), 0.0)
        suf = suf + jnp.where(lane < L - d, pltpu.roll(suf, L - d, axis=1), 0.0)
        d *= 2
    cum = jnp.where(fwd_row, pre, suf)
    tot = jnp.where(fwd_row, cum[:, L - 1:L], cum[:, 0:1])
    a = tot - cum + g_ref[1]
    ml = jnp.max(a, axis=1, keepdims=True)
    w_ref[...] = jnp.exp(a - ml)
    ml_ref[...] = jnp.broadcast_to(ml, (rows_all, L))
    tot_ref[...] = tot
    r = g_ref[1] - cum
    pmax, smax = r, r
    d = 1
    while d < L:
        pmax = jnp.maximum(pmax, jnp.where(lane >= d, pltpu.roll(pmax, d, axis=1), NEG_INF))
        smax = jnp.maximum(smax, jnp.where(lane < L - d, pltpu.roll(smax, L - d, axis=1), NEG_INF))
        d *= 2
    pm_ref[...] = jnp.where(fwd_row, pmax, smax)

    def tile_bcast(x, src):
        y = jnp.where(sub == src, x, 0.0)
        if src:
            y = pltpu.roll(y, rows_all - src, axis=0)
        for s in (1, 2, 4):
            y = y + pltpu.roll(y, s, axis=0)
        return y

    for dr in range(2):
        b_hi, b_mid, b_lo = _split3(tile_bcast(cum, dr))
        r_hi, r_mid, r_lo = _split3(tile_bcast(r, dr))
        tb_ref[dr] = jnp.where(sub == 1, b_hi, jnp.where(sub == 2, b_mid, jnp.where(
            sub == 3, b_lo, jnp.where((sub >= 4) & (sub <= 6), 1.0, 0.0))))
        tr_ref[dr] = jnp.where(sub == 0, 1.0, jnp.where(sub == 4, r_hi, jnp.where(
            sub == 5, r_mid, jnp.where(sub == 6, r_lo, 0.0))))

    def chunk_rows(ref, c):
        return ref[pl.ds(pl.multiple_of(c * 8, 8), 8), :]

    def v_aug(c):
        return jnp.concatenate([v_ref[pl.ds(c * L, L), :], ones_col], axis=1)

    def phase_a(c, carry):
        kt = kt_ref[c].astype(F32)
        w = chunk_rows(w_ref, c)
        kw = jnp.concatenate([kt * w[0:1], kt * w[1:2]], axis=0)
        dc_ref[c] = _dot(kw.astype(BF16), v_aug(c))
        return carry

    lax.fori_loop(0, nc, phase_a, 0, unroll=4)

    def scan_dir(direction_row, reverse):
        off_rows = direction_row * dh
        off_cols = direction_row * 2 * dh

        def body(i, carry):
            c = (nc - 1 - i) if reverse else i
            state, m = carry
            cs_ref[c, :, off_cols:off_cols + 2 * dh] = state.astype(BF16)
            ms_ref[c, direction_row:direction_row + 1, :] = m
            ml_c = chunk_rows(ml_ref, c)[direction_row:direction_row + 1]
            g_c = chunk_rows(tot_ref, c)[direction_row:direction_row + 1]
            m_new = jnp.maximum(g_c + m, ml_c)
            alpha = jnp.exp(g_c + m - m_new)
            beta = jnp.exp(ml_c - m_new)
            alpha2 = jnp.concatenate([alpha, alpha], axis=1)
            beta2 = jnp.concatenate([beta, beta], axis=1)
            state = alpha2 * state + beta2 * dc_ref[c, off_rows:off_rows + dh, :]
            return state, m_new

        init = (jnp.zeros((dh, 2 * dh), F32), jnp.zeros((1, L), F32))
        lax.fori_loop(0, nc, body, init)

    scan_dir(0, False)
    scan_dir(1, True)

    ng = ng_ref[...]
    sk = sk_ref[...]

    sub8 = lax.broadcasted_iota(jnp.int32, (8, L), 0)
    floor_rows = jnp.where(sub8 == 0, 1.0, jnp.where(sub8 <= 3, -1.0, 0.0))
    no_rows = jnp.zeros((8, 3 * L), F32)

    def direction(s_qk, qc, vaug, ex, keep):
        w = jnp.exp(jnp.where(keep, ex[:, :L], NEG_INF)) * s_qk
        scale = jnp.exp(ex[:, L:2 * L])
        intra = _dot(w.astype(BF16), vaug)
        tot_c = intra + jnp.concatenate([scale, scale], axis=1) * qc
        return tot_c[:, :dh] / jnp.maximum(jnp.abs(tot_c[:, dh:]), jnp.exp(ex[:, 2 * L:]))

    def phase_c(c, carry):
        rows = pl.ds(c * L, L)
        q = q_ref[rows, :]
        s_qk = _dot(q, kt_ref[c])
        qc = _dot(q, cs_ref[c])
        vaug = v_aug(c)
        ms = ms_ref[c]
        e = -jnp.maximum(ms, chunk_rows(pm_ref, c))
        e = (e - jnp.abs(e) * (2.0 ** -7)).astype(BF16).astype(F32)
        lhs_tiles, rhs_tiles = [], []
        for dr in range(2):
            e_rows = jnp.broadcast_to(e[dr:dr + 1], (8, L))
            lhs_tiles.append(jnp.where(sub8 == 0, e_rows, chunk_rows(tb_ref.at[dr], c)))
            m_hi, m_mid, m_lo = _split3(jnp.broadcast_to(ms[dr:dr + 1], (8, L)))
            scale_rows = jnp.where(sub8 == 0, 1.0, jnp.where(sub8 == 4, m_hi, jnp.where(
                sub8 == 5, m_mid, jnp.where(sub8 == 6, m_lo, 0.0))))
            rhs_tiles.append(jnp.concatenate(
                [chunk_rows(tr_ref.at[dr], c), scale_rows, floor_rows], axis=1))
        lhs = jnp.concatenate(lhs_tiles, axis=0).astype(BF16)
        rhs = jnp.concatenate([jnp.concatenate([rhs_tiles[0], no_rows], axis=1),
                               jnp.concatenate([no_rows, rhs_tiles[1]], axis=1)],
                              axis=0).astype(BF16)
        ex = lax.dot_general(lhs, rhs, (((0,), (0,)), ((), ())),
                             preferred_element_type=F32)
        h = (direction(s_qk, qc[:, :2 * dh], vaug, ex[:, :3 * L], col_i <= row_i)
             + direction(s_qk, qc[:, 2 * dh:], vaug, ex[:, 3 * L:], col_i >= row_i))
        hn = _rms(h, ng)
        o_ref[rows, :] = (hn + sk * xc_ref[rows, :]) * _silu(z_ref[rows, :])
        return carry

    lax.fori_loop(0, nc, phase_c, 0, unroll=16)


def _mlstm(q, kt, v, gates, xc, z, norm_g, skip):
    b, seq, _ = q.shape
    nc = seq // MLSTM_CHUNK
    dh = MLSTM_HEAD_DIM
    head = lambda i, j: (i, 0, j)
    vec = lambda i, j: (0, j)
    blk = pl.BlockSpec((None, seq, dh), head)
    gate_rows = pltpu.VMEM((nc * 8, MLSTM_CHUNK), F32)
    gate_tiles = pltpu.VMEM((2, nc * 8, MLSTM_CHUNK), F32)
    return pl.pallas_call(
        _mlstm_kernel,
        grid=(b, MLSTM_HEADS),
        in_specs=[
            blk,
            pl.BlockSpec((None, nc, dh, MLSTM_CHUNK), lambda i, j: (i, 0, j, 0)),
            blk,
            pl.BlockSpec((None, None, 2, nc * 8, MLSTM_CHUNK), lambda i, j: (i, j, 0, 0, 0)),
            blk, blk,
            pl.BlockSpec((1, dh), vec),
            pl.BlockSpec((1, dh), vec),
        ],
        out_specs=blk,
        out_shape=jax.ShapeDtypeStruct((b, seq, D_MLSTM), F32),
        scratch_shapes=[
            gate_rows, gate_rows, gate_rows, gate_rows, gate_tiles, gate_tiles,
            pltpu.VMEM((nc, 2 * dh, 2 * dh), F32),
            pltpu.VMEM((nc, dh, 4 * dh), BF16),
            pltpu.VMEM((nc, 8, MLSTM_CHUNK), F32),
        ],
        compiler_params=_params(("parallel", "parallel")),
        name="mlstm",
    )(q, kt, v, gates, xc, z, norm_g, skip)


def _band_attn_kernel(q_ref, k_ref, v_ref, o_ref, lse_ref, *, half, dil):
    lsub = q_ref.shape[0]
    tq = ATTN_Q_TILE
    win = min(lsub, 2 * tq)
    first = lax.broadcasted_iota(jnp.int32, (1, LANES), 1) < ATTN_HEAD_DIM
    lane = lax.broadcasted_iota(jnp.int32, (tq, LANES), 1)
    rel = (lax.broadcasted_iota(jnp.int32, (2 * tq, win), 1)
           - lax.broadcasted_iota(jnp.int32, (2 * tq, win), 0) % tq)

    def tile(qs, ws, out_rows):
        keep = jnp.abs(rel + (ws - qs)) <= half
        for c in range(dil):
            lse_tile = jnp.zeros((tq, LANES), F32)
            for p in range(D_ATTN // LANES):
                lanes = slice(c * D_ATTN + p * LANES, c * D_ATTN + (p + 1) * LANES)
                q = q_ref[pl.ds(qs, tq), lanes]
                kw = k_ref[pl.ds(ws, win), lanes]
                vw = v_ref[pl.ds(ws, win), lanes]
                zero = jnp.zeros_like(q)
                q2 = jnp.concatenate([jnp.where(first, q, zero), jnp.where(first, zero, q)], axis=0)
                s = jnp.where(keep, _dot_nt(q2, kw), NEG_INF)
                m = jnp.max(s, axis=1, keepdims=True)
                e = jnp.exp(s - m)
                l = jnp.sum(e, axis=1, keepdims=True)
                o2 = _dot(e.astype(BF16), vw) / l
                o_ref[p, out_rows(c), :] = jnp.where(first, o2[:tq], o2[tq:])
                lse2 = m + jnp.log(l)
                lse_tile = jnp.where(lane == 2 * p, lse2[:tq],
                                     jnp.where(lane == 2 * p + 1, lse2[tq:], lse_tile))
            lse_ref[out_rows(c), :] = lse_tile

    if dil == 1:
        def body(t, carry):
            qs = pl.multiple_of(t * tq, tq)
            ws = pl.multiple_of(jnp.clip(qs - half, 0, lsub - win), half)
            tile(qs, ws, lambda c: pl.ds(qs, tq))
            return carry

        lax.fori_loop(0, lsub // tq, body, 0, unroll=4)
    else:
        for t in range(lsub // tq):
            qs = t * tq
            ws = min(max(qs - half, 0), lsub - win)
            tile(qs, ws, lambda c, qs=qs: pl.ds(qs * dil + c, tq, stride=dil))


def _band_attn(q, k, v, seq, win, dil):
    lsub = seq // dil
    b = q.shape[0] // lsub
    half = win // (2 * dil)
    pairs = D_ATTN // LANES
    assert lsub % ATTN_Q_TILE == 0 and half % 16 == 0 and ATTN_Q_TILE + 2 * half <= 2 * ATTN_Q_TILE
    blk = pl.BlockSpec((lsub, dil * D_ATTN), lambda i: (i, 0))
    return pl.pallas_call(
        functools.partial(_band_attn_kernel, half=half, dil=dil),
        grid=(b,),
        in_specs=[blk, blk, blk],
        out_specs=[pl.BlockSpec((None, pairs, seq, LANES), lambda i: (i, 0, 0, 0)),
                   pl.BlockSpec((seq, LANES), lambda i: (i, 0))],
        out_shape=[
            jax.ShapeDtypeStruct((b, pairs, seq, LANES), F32),
            jax.ShapeDtypeStruct((b * seq, LANES), F32),
        ],
        compiler_params=_params(("parallel",)),
        name=f"band_attn_d{dil}",
    )(q, k, v)


def _out_proj_kernel(ym_ref, o1_ref, o2_ref, o3_ref, l1_ref, l2_ref, l3_ref, sp_ref, x_ref, ag_ref,
                     w_ref, n2_ref, wr_ref, x2_ref, h2_ref, lg_ref):
    spread = sp_ref[...]
    wr = wr_ref[...]
    w_hi = wr.astype(BF16)
    w_lo = (wr - w_hi.astype(F32)).astype(BF16)
    wr3 = jnp.concatenate([w_hi, w_hi, w_lo], axis=0)

    def per_lane(w):
        hi = w.astype(BF16)
        lo = (w - hi.astype(F32)).astype(BF16)
        return _dot(jnp.concatenate([hi, lo], axis=1), spread)

    rows = x_ref.shape[0] // OUT_PROJ_SPLITS
    for part in range(OUT_PROJ_SPLITS):
        rs = slice(part * rows, (part + 1) * rows)
        lses = [r[rs, :] for r in (l1_ref, l2_ref, l3_ref)]
        top = jnp.maximum(jnp.maximum(lses[0], lses[1]), lses[2])
        wts = [jnp.exp(l - top) for l in lses]
        total = wts[0] + wts[1] + wts[2]

        def heads(o_ref):
            return jnp.concatenate([o_ref[p, rs, :] for p in range(o_ref.shape[0])], axis=1)

        ya = sum(per_lane(w / total) * heads(o) for w, o in zip(wts, (o1_ref, o2_ref, o3_ref)))
        ya = _rms(ya, ag_ref[...])
        mixed = jnp.concatenate([ym_ref[rs, :], ya], axis=1).astype(BF16)
        x2 = x_ref[rs, :] + _dot(mixed, w_ref[...])
        x2_ref[rs, :] = x2
        h2 = _rms(x2, n2_ref[...])
        hi = h2.astype(BF16)
        h2_ref[rs, :] = hi
        lo = (h2 - hi.astype(F32)).astype(BF16)
        lg_ref[rs, :] = _dot(jnp.concatenate([hi, lo, hi], axis=1), wr3)


def _out_proj(ym, branch_o, branch_lse, x2d, attn_g, w_bf, n2g, wr_pad):
    n = x2d.shape[0]
    spread = (jnp.arange(LANES)[:, None] == jnp.arange(D_ATTN)[None, :] // ATTN_HEAD_DIM)
    spread = jnp.tile(spread.astype(BF16), (2, 1))
    pairs, seq = branch_o[0].shape[1:3]
    tiles_per_seq = seq // ROW_TILE
    branch = pl.BlockSpec((None, pairs, ROW_TILE, LANES),
                          lambda i: (i // tiles_per_seq, 0, i % tiles_per_seq, 0))
    row = lambda i: (i, 0)
    fixed = lambda i: (0, 0)
    return pl.pallas_call(
        _out_proj_kernel,
        grid=(n // ROW_TILE,),
        in_specs=[
            pl.BlockSpec((ROW_TILE, D_MLSTM), row),
            branch, branch, branch,
            pl.BlockSpec((ROW_TILE, LANES), row),
            pl.BlockSpec((ROW_TILE, LANES), row),
            pl.BlockSpec((ROW_TILE, LANES), row),
            pl.BlockSpec((2 * LANES, D_ATTN), fixed),
            pl.BlockSpec((ROW_TILE, D_MODEL), row),
            pl.BlockSpec((1, D_ATTN), fixed),
            pl.BlockSpec((D_MODEL, D_MODEL), fixed),
            pl.BlockSpec((1, D_MODEL), fixed),
            pl.BlockSpec((D_MODEL, LANES), fixed),
        ],
        out_specs=[
            pl.BlockSpec((ROW_TILE, D_MODEL), row),
            pl.BlockSpec((ROW_TILE, D_MODEL), row),
            pl.BlockSpec((ROW_TILE, LANES), row),
        ],
        out_shape=[
            jax.ShapeDtypeStruct((n, D_MODEL), F32),
            jax.ShapeDtypeStruct((n, D_MODEL), BF16),
            jax.ShapeDtypeStruct((n, LANES), F32),
        ],
        compiler_params=_params(("parallel",)),
        name="out_proj",
    )(ym, *branch_o, *branch_lse, spread, x2d, attn_g, w_bf, n2g, wr_pad)


def _route_kernel(lg_ref, tri_ref, eye_ref, slot_ref, slot_t_ref, aff_ref, *, cap):
    for i in range(lg_ref.shape[0]):
        _route_one(lg_ref.at[i], tri_ref, eye_ref, slot_ref.at[i], slot_t_ref.at[i],
                   aff_ref.at[i], cap=cap)


def _route_one(lg_ref, tri_ref, eye_ref, slot_ref, slot_t_ref, aff_ref, *, cap):
    lg = lg_ref[...]
    valid = lax.broadcasted_iota(jnp.int32, (1, LANES), 1) < N_EXPERTS
    lg = jnp.where(valid, lg, NEG_INF)
    e = jnp.exp(lg - jnp.max(lg, axis=1, keepdims=True))
    aff = e / jnp.sum(e, axis=1, keepdims=True)
    aff_ref[...] = aff
    groups = LANES // N_EXPERTS
    rpg = aff.shape[0] // groups
    lane = lax.broadcasted_iota(jnp.int32, (1, LANES), 1)
    packed = aff[:rpg]
    for g in range(1, groups):
        packed = packed + pltpu.roll(aff[g * rpg:(g + 1) * rpg], g * N_EXPERTS, axis=1)

    def over_groups(x):
        shift = N_EXPERTS
        while shift < LANES:
            x = x + pltpu.roll(x, shift, axis=1)
            shift *= 2
        return x

    def enough(cand):
        part = jnp.sum(jnp.where(packed >= cand, 1.0, 0.0), axis=0, keepdims=True)
        return over_groups(part) >= cap

    def narrow(lo, hi, cands):
        new_lo, new_hi = lo, hi
        for cand in cands:
            ok = enough(cand)
            new_lo = jnp.maximum(new_lo, jnp.where(ok, cand, lo))
            new_hi = jnp.minimum(new_hi, jnp.where(ok, hi, cand))
        return new_lo, new_hi

    tiny = jnp.full((1, LANES), 2.0 ** -126, F32)
    normal = enough(tiny)
    p = tiny
    for span, count in ((16, 7), (1, 15)):
        p, _ = narrow(p, p, [p * (2.0 ** (span * j)) for j in range(1, count + 1)])
    lo = jnp.where(normal, p, 0.0)
    hi = jnp.where(normal, p * 2.0, tiny)
    width = jnp.where(normal, p, 0.0)
    for bits in THRESHOLD_RADIX_BITS:
        width = width * (0.5 ** bits)
        lo, hi = narrow(lo, hi, [lo + j * width for j in range(1, 2 ** bits)])
    gt = jnp.where(packed >= hi, 1.0, 0.0)
    eq = jnp.where(packed >= lo, 1.0, 0.0) - gt
    need = cap - over_groups(jnp.sum(gt, axis=0, keepdims=True))
    tri = tri_ref[...]

    def count_before(x):
        per_group = jnp.sum(x, axis=0, keepdims=True)
        upto = per_group
        shift = N_EXPERTS
        while shift < LANES:
            upto = upto + jnp.where(lane >= shift, pltpu.roll(upto, shift, axis=1), 0.0)
            shift *= 2
        return _dot(tri, x.astype(BF16)) + (upto - per_group)

    sel = gt + eq * jnp.where(count_before(eq) < need, 1.0, 0.0)
    pos = count_before(sel)
    slot_packed = jnp.where(sel > 0.0, pos, -1.0)
    slot = jnp.concatenate(
        [jnp.where(valid, slot_packed if g == 0 else
                   pltpu.roll(slot_packed, LANES - g * N_EXPERTS, axis=1), -1.0)
         for g in range(groups)], axis=0)
    slot_ref[...] = slot
    slot_t_ref[...] = _dot_nt(eye_ref[...], slot.astype(BF16))


def _route(logits, cap):
    b, seq, _ = logits.shape
    assert LANES % N_EXPERTS == 0 and seq % (LANES // N_EXPERTS) == 0
    rpg = seq // (LANES // N_EXPERTS)
    tri = (jnp.arange(rpg)[None, :] < jnp.arange(rpg)[:, None]).astype(BF16)
    eye = jnp.eye(LANES, dtype=BF16)
    per_b = lambda i: (i, 0, 0)
    fixed = lambda i: (0, 0)
    nb = ROUTE_BATCHES if b % ROUTE_BATCHES == 0 else 1
    return pl.pallas_call(
        functools.partial(_route_kernel, cap=cap),
        grid=(b // nb,),
        in_specs=[
            pl.BlockSpec((nb, seq, LANES), per_b),
            pl.BlockSpec((rpg, rpg), fixed),
            pl.BlockSpec((LANES, LANES), fixed),
        ],
        out_specs=[
            pl.BlockSpec((nb, seq, LANES), per_b),
            pl.BlockSpec((nb, LANES, seq), per_b),
            pl.BlockSpec((nb, seq, LANES), per_b),
        ],
        out_shape=[
            jax.ShapeDtypeStruct((b, seq, LANES), F32),
            jax.ShapeDtypeStruct((b, LANES, seq), F32),
            jax.ShapeDtypeStruct((b, seq, LANES), F32),
        ],
        compiler_params=_params(("parallel",)),
        name="route",
    )(logits, tri, eye)


def _moe_gather_kernel(slot_ref, h_ref, xs_ref):
    srow = slot_ref[...]
    cap, seq = xs_ref.shape[0], srow.shape[1]
    ci = lax.broadcasted_iota(jnp.int32, (cap, seq), 0).astype(F32)
    onehot = jnp.where(srow == ci, 1.0, 0.0).astype(BF16)
    xs_ref[...] = _dot(onehot, h_ref[...]).astype(BF16)


def _moe_gather(slot_t, h2, cap):
    b, seq, _ = h2.shape
    return pl.pallas_call(
        _moe_gather_kernel,
        grid=(b, N_EXPERTS),
        in_specs=[
            pl.BlockSpec((None, None, 1, seq), lambda i, e: (i, e, 0, 0)),
            pl.BlockSpec((None, seq, D_MODEL), lambda i, e: (i, 0, 0)),
        ],
        out_specs=pl.BlockSpec((None, None, cap, D_MODEL), lambda i, e: (i, e, 0, 0)),
        out_shape=jax.ShapeDtypeStruct((b, N_EXPERTS, cap, D_MODEL), BF16),
        compiler_params=_params(("parallel", "parallel")),
        name="moe_gather",
    )(slot_t, h2)


def _moe_ffn_kernel(xs_ref, w1_ref, w3_ref, w2_ref, y_ref, act_ref, w1b_ref, w3b_ref, w2b_ref):
    s = pl.program_id(1)
    nb, cap, _ = xs_ref.shape
    nf = act_ref.shape[0]
    per = FFN_ROW_TILE // cap
    row_tiles = nb // per

    @pl.when(s < nf)
    def _():
        w1b_ref[...] = w1_ref[...].astype(BF16)
        w3b_ref[...] = w3_ref[...].astype(BF16)
        for r in range(row_tiles):
            x = xs_ref[r * per:(r + 1) * per].reshape(FFN_ROW_TILE, D_MODEL)
            up = _dot(x, w1b_ref[...])
            gt = _dot(x, w3b_ref[...])
            act_ref[s, r * FFN_ROW_TILE:(r + 1) * FFN_ROW_TILE, :] = (_silu(up) * gt).astype(BF16)

    @pl.when(s >= nf)
    def _():
        w2b_ref[...] = w2_ref[...].astype(BF16)
        for r in range(row_tiles):
            rows = slice(r * FFN_ROW_TILE, (r + 1) * FFN_ROW_TILE)
            act = jnp.concatenate([act_ref[f, rows, :] for f in range(nf)], axis=1)
            y = _dot(act, w2b_ref[...])
            y_ref[r * per:(r + 1) * per] = y.astype(BF16).reshape(per, cap, y.shape[1])


def _moe_ffn(xs, w1, w3, w2):
    b, ne, cap, _ = xs.shape
    nf = D_EXPERT // FFN_F_TILE
    nn = D_MODEL // FFN_N_TILE
    hidden = lambda e, s: (e, 0, jnp.minimum(s, nf - 1))
    out_col = lambda e, s: jnp.maximum(s - nf, 0)
    return pl.pallas_call(
        _moe_ffn_kernel,
        grid=(ne, nf + nn),
        in_specs=[
            pl.BlockSpec((b, None, cap, D_MODEL), lambda e, s: (0, e, 0, 0)),
            pl.BlockSpec((None, D_MODEL, FFN_F_TILE), hidden),
            pl.BlockSpec((None, D_MODEL, FFN_F_TILE), hidden),
            pl.BlockSpec((None, D_EXPERT, FFN_N_TILE), lambda e, s: (e, 0, out_col(e, s))),
        ],
        out_specs=pl.BlockSpec((b, None, cap, FFN_N_TILE), lambda e, s: (0, e, 0, out_col(e, s))),
        out_shape=jax.ShapeDtypeStruct(xs.shape, BF16),
        scratch_shapes=[
            pltpu.VMEM((nf, b * cap, FFN_F_TILE), BF16),
            pltpu.VMEM((D_MODEL, FFN_F_TILE), BF16),
            pltpu.VMEM((D_MODEL, FFN_F_TILE), BF16),
            pltpu.VMEM((D_EXPERT, FFN_N_TILE), BF16),
        ],
        compiler_params=_params(("parallel", "arbitrary")),
        name="moe_ffn",
    )(xs, w1, w3, w2)


def _moe_scatter_kernel(slot_ref, aff_ref, y_ref, x2_ref, g_ref, o_ref):
    slot = slot_ref[...]
    aff = aff_ref[...]
    rows, cap = slot.shape[0], y_ref.shape[1]
    ci = lax.broadcasted_iota(jnp.int32, (rows, cap), 1).astype(F32)
    acc = x2_ref[...]
    for e in range(N_EXPERTS):
        onehot = jnp.where(slot[:, e:e + 1] == ci, 1.0, 0.0).astype(BF16)
        acc = acc + aff[:, e:e + 1] * _dot(onehot, y_ref[e])
    o_ref[...] = _rms(acc, g_ref[...])


def _moe_scatter(slot, aff, y, x2, norm_g):
    b, seq, _ = x2.shape
    cap = y.shape[2]
    tile = lambda i, r: (i, r, 0)
    return pl.pallas_call(
        _moe_scatter_kernel,
        grid=(b, seq // ROW_TILE),
        in_specs=[
            pl.BlockSpec((None, ROW_TILE, LANES), tile),
            pl.BlockSpec((None, ROW_TILE, LANES), tile),
            pl.BlockSpec((None, N_EXPERTS, cap, D_MODEL), lambda i, r: (i, 0, 0, 0)),
            pl.BlockSpec((None, ROW_TILE, D_MODEL), tile),
            pl.BlockSpec((1, D_MODEL), lambda i, r: (0, 0)),
        ],
        out_specs=pl.BlockSpec((None, ROW_TILE, D_MODEL), tile),
        out_shape=jax.ShapeDtypeStruct((b, seq, D_MODEL), F32),
        compiler_params=_params(("parallel", "parallel")),
        name="moe_scatter",
    )(slot, aff, y, x2, norm_g)


def kernel(x, norm1_g, w_in, conv_w, conv_b, wq_m, wk_m, wv_m, w_if_fwd, b_if_fwd,
           w_if_bwd, b_if_bwd, mlstm_norm_g, mlstm_skip, attn_norm_g, w_out, norm2_g,
           w_router, w1, w3, w2, norm_f_g):
    b, seq, _ = x.shape
    assert w_in.shape[0] == 1, "single-layer problem"
    assert seq % ROW_TILE == 0 and seq % MLSTM_CHUNK == 0 and seq % ATTN_Q_TILE == 0
    cap = EC_CAPACITY * seq // N_EXPERTS
    assert FFN_ROW_TILE % cap == 0 and (b * cap) % FFN_ROW_TILE == 0
    nc = seq // MLSTM_CHUNK
    l = 0
    x2d = x.reshape(b * seq, D_MODEL)
    xm, z, *qkv_views = _in_proj(x2d, norm1_g[l][None, :], w_in[l].astype(BF16), seq)
    shp = lambda t: t.reshape(b, seq, t.shape[-1])
    wif_rows, bif_rows = _gate_rows(w_if_fwd[l], b_if_fwd[l], w_if_bwd[l], b_if_bwd[l])
    xc, qm, ktm, vm, gates = _mlstm_pre(
        shp(xm), conv_w[l], conv_b[l][None, :],
        _block_diag(wq_m[l]).astype(BF16), _block_diag(wk_m[l]).T.astype(BF16),
        _block_diag(wv_m[l]).astype(BF16), wif_rows.astype(BF16), bif_rows)
    gates = gates.reshape(b, 2, MLSTM_HEADS, 8, nc, MLSTM_CHUNK).transpose(0, 2, 1, 4, 3, 5)
    gates = gates.reshape(b, MLSTM_HEADS, 2, nc * 8, MLSTM_CHUNK)
    ym = _mlstm(qm, ktm, vm, gates, xc, shp(z), mlstm_norm_g[l][None, :],
                mlstm_skip[l][None, :])
    branches = [_band_attn(*qkv_views[3 * d:3 * d + 3], seq, win, dil)
                for d, (win, dil) in enumerate(DILATED_PATTERNS)]
    wr_pad = jnp.pad(w_router[l], ((0, 0), (0, LANES - N_EXPERTS)))
    x2, h2, logits = _out_proj(
        ym.reshape(b * seq, D_MLSTM), [o for o, _ in branches], [s for _, s in branches], x2d,
        attn_norm_g[l][None, :], w_out[l].astype(BF16), norm2_g[l][None, :], wr_pad)
    slot, slot_t, aff = _route(logits.reshape(b, seq, LANES), cap)
    xs = _moe_gather(slot_t.reshape(b, LANES, 1, seq), h2.reshape(b, seq, D_MODEL), cap)
    y = _moe_ffn(xs, w1[l], w3[l], w2[l])
    return _moe_scatter(slot, aff, y, x2.reshape(b, seq, D_MODEL), norm_f_g[None, :])
```

```python
import functools

import jax
import jax.numpy as jnp
from jax import lax
from jax.experimental import pallas as pl
from jax.experimental.pallas import tpu as pltpu

F32 = jnp.float32
BF16 = jnp.bfloat16

D_MODEL = 1024
D_MLSTM = 512
D_ATTN = 512
D_IN_PROJ = 2 * D_MLSTM + 3 * D_ATTN
MLSTM_HEADS = 4
MLSTM_HEAD_DIM = 128
MLSTM_QKV_BLOCK = 4
MLSTM_CONV = 5
ATTN_HEAD_DIM = 64
ROPE_DIM = 16
ROPE_THETA = 500000.0
DILATED_PATTERNS = ((128, 1), (512, 4), (2048, 16))
N_EXPERTS = 16
EC_CAPACITY = 2
D_EXPERT = 2816
NORM_EPS = 1e-6
NEG_INF = -1e30

LANES = 128
MLSTM_CHUNK = 128
ROW_TILE = 512
OUT_PROJ_SPLITS = 2
ROUTE_BATCHES = 2
ATTN_Q_TILE = 128
FFN_F_TILE = 256
FFN_N_TILE = 256
FFN_ROW_TILE = 512
THRESHOLD_RADIX_BITS = (4, 4, 4, 4, 4, 3, 4, 4, 4, 4)
V7X_VMEM_BYTES = 64 * 1024 * 1024
VMEM_LIMIT = V7X_VMEM_BYTES * 7 // 8


def _params(sem):
    return pltpu.CompilerParams(dimension_semantics=sem, vmem_limit_bytes=VMEM_LIMIT)


def _rms(x, g):
    return x * lax.rsqrt(jnp.mean(x * x, axis=-1, keepdims=True) + NORM_EPS) * g


def _silu(x):
    return x * (1.0 / (1.0 + jnp.exp(-x)))


def _dot(a, b):
    return jnp.dot(a, b, preferred_element_type=F32)


def _dot_nt(a, b):
    return lax.dot_general(a, b, (((1,), (1,)), ((), ())), preferred_element_type=F32)


def _in_proj_kernel(x_ref, g_ref, w_ref, cos_ref, sa_ref, sb_ref, xm_ref, z_ref, *rest):
    qkv_refs, scr_ref = rest[:-1], rest[-1]
    h = _rms(x_ref[...], g_ref[...])
    p = _dot(h.astype(BF16), w_ref[...])
    xm_ref[...] = p[:, :D_MLSTM]
    z_ref[...] = p[:, D_MLSTM:2 * D_MLSTM]
    cos, sa, sb = cos_ref[...], sa_ref[...], sb_ref[...]
    half = ROPE_DIM // 2

    def rope(t):
        outs = []
        for j in range(D_ATTN // LANES):
            tj = t[:, j * LANES:(j + 1) * LANES]
            up = pltpu.roll(tj, LANES - half, axis=1)
            dn = pltpu.roll(tj, half, axis=1)
            outs.append(tj * cos + up * sa + dn * sb)
        return jnp.concatenate(outs, axis=1)

    o = 2 * D_MLSTM
    qkv = (rope(p[:, o:o + D_ATTN]) * (ATTN_HEAD_DIM ** -0.5),
           rope(p[:, o + D_ATTN:o + 2 * D_ATTN]),
           p[:, o + 2 * D_ATTN:])
    rows = p.shape[0]
    groups = D_ATTN // LANES
    for a, val in enumerate(qkv):
        for j in range(groups):
            scr_ref[j] = val[:, j * LANES:(j + 1) * LANES]
        for d, (_, dil) in enumerate(DILATED_PATTERNS):
            ref = qkv_refs[3 * d + a]
            if dil == 1:
                ref[...] = val.astype(BF16)
                continue
            for r in range(dil):
                for j in range(groups):
                    piece = scr_ref[j, pl.ds(r, rows // dil, stride=dil), :]
                    ref[:, r * D_ATTN + j * LANES:r * D_ATTN + (j + 1) * LANES] = piece.astype(BF16)


def _rope_tables(seq):
    half = ROPE_DIM // 2
    inv_freq = ROPE_THETA ** (-2.0 * jnp.arange(half, dtype=F32) / ROPE_DIM)
    ang = jnp.arange(seq).astype(F32)[:, None] * inv_freq[None, :]
    cos, sin = jnp.cos(ang), jnp.sin(ang)
    pad = jnp.zeros((seq, ATTN_HEAD_DIM - ROPE_DIM), F32)
    cos_h = jnp.concatenate([cos, cos, pad + 1.0], axis=1)
    sa_h = jnp.concatenate([-sin, jnp.zeros_like(sin), pad], axis=1)
    sb_h = jnp.concatenate([jnp.zeros_like(sin), sin, pad], axis=1)
    rep = LANES // ATTN_HEAD_DIM
    return tuple(jnp.tile(t, (1, rep)) for t in (cos_h, sa_h, sb_h))


def _in_proj(x2d, g, w_bf, seq):
    n = x2d.shape[0]
    tiles_per_seq = seq // ROW_TILE
    cos, sa, sb = _rope_tables(seq)
    row = lambda i: (i, 0)
    fixed = lambda i: (0, 0)
    pos = lambda i: (i % tiles_per_seq, 0)
    return pl.pallas_call(
        _in_proj_kernel,
        grid=(n // ROW_TILE,),
        in_specs=[
            pl.BlockSpec((ROW_TILE, D_MODEL), row),
            pl.BlockSpec((1, D_MODEL), fixed),
            pl.BlockSpec((D_MODEL, D_IN_PROJ), fixed),
            pl.BlockSpec((ROW_TILE, LANES), pos),
            pl.BlockSpec((ROW_TILE, LANES), pos),
            pl.BlockSpec((ROW_TILE, LANES), pos),
        ],
        out_specs=[
            pl.BlockSpec((ROW_TILE, D_MLSTM), row),
            pl.BlockSpec((ROW_TILE, D_MLSTM), row),
        ] + [pl.BlockSpec((ROW_TILE // dil, dil * D_ATTN), row)
             for _, dil in DILATED_PATTERNS for _ in range(3)],
        out_shape=[
            jax.ShapeDtypeStruct((n, D_MLSTM), F32),
            jax.ShapeDtypeStruct((n, D_MLSTM), F32),
        ] + [jax.ShapeDtypeStruct((n // dil, dil * D_ATTN), BF16)
             for _, dil in DILATED_PATTERNS for _ in range(3)],
        scratch_shapes=[pltpu.VMEM((D_ATTN // LANES, ROW_TILE, LANES), F32)],
        compiler_params=_params(("parallel",)),
        name="in_proj",
    )(x2d, g, w_bf, cos, sa, sb)


def _mlstm_pre_kernel(xm_ref, cw_ref, cb_ref, wq_ref, wkt_ref, wv_ref, wif_ref, bif_ref,
                      xc_ref, q_ref, kt_ref, v_ref, g_ref):
    x = xm_ref[...]
    seq = x.shape[0]
    t = lax.broadcasted_iota(jnp.int32, x.shape, 0)
    acc = jnp.zeros_like(x) + cb_ref[...]
    for j in range(MLSTM_CONV):
        d = j - MLSTM_CONV // 2
        if d == 0:
            tap = x
        else:
            tap = pltpu.roll(x, (-d) % seq, axis=0)
            tap = jnp.where((t + d >= 0) & (t + d < seq), tap, 0.0)
        acc = acc + tap * cw_ref[j:j + 1, :]
    xc = _silu(acc)
    xc_ref[...] = xc
    xcb = xc.astype(BF16)
    q = _dot(xcb, wq_ref[...]).astype(BF16)
    kt = (_dot_nt(wkt_ref[...], xcb) * (MLSTM_HEAD_DIM ** -0.5)).astype(BF16)
    v = _dot(x.astype(BF16), wv_ref[...]).astype(BF16)
    q_ref[...] = q
    v_ref[...] = v
    L = MLSTM_CHUNK
    for c in range(seq // L):
        kt_ref[c] = kt[:, c * L:(c + 1) * L]
    wif = wif_ref[...]
    g_ref[...] = (_dot_nt(wif[:, :D_MLSTM], q) + _dot(wif[:, D_MLSTM:2 * D_MLSTM], kt)
                  + _dot_nt(wif[:, 2 * D_MLSTM:], v) + bif_ref[...])


def _block_diag(w):
    nblk = w.shape[0]
    n = nblk * MLSTM_QKV_BLOCK
    tiled = jnp.tile(w.reshape(n, MLSTM_QKV_BLOCK), (1, nblk))
    blk = jnp.arange(n) // MLSTM_QKV_BLOCK
    return jnp.where(blk[:, None] == blk[None, :], tiled, 0.0)


def _gate_rows(w_f, b_f, w_b, b_b):
    h = MLSTM_HEADS

    def rows(f, b):
        pair = jnp.stack([f, b], axis=-1)
        pair = jnp.concatenate([pair[..., h:, :], pair[..., :h, :]], axis=-2)
        pair = jnp.pad(pair, [(0, 0)] * (pair.ndim - 1) + [(0, 6)])
        return pair.reshape(*pair.shape[:-2], 2 * h * 8)

    return rows(w_f, w_b).T, rows(b_f, b_b)[:, None]


def _mlstm_pre(xm, conv_w, conv_b, wq, wkt, wv, wif_rows, bif_rows):
    b, seq, _ = xm.shape
    nrow = wif_rows.shape[0]
    nc = seq // MLSTM_CHUNK
    per_b = lambda i: (i, 0, 0)
    fixed = lambda i: (0, 0)
    return pl.pallas_call(
        _mlstm_pre_kernel,
        grid=(b,),
        in_specs=[
            pl.BlockSpec((None, seq, D_MLSTM), per_b),
            pl.BlockSpec((MLSTM_CONV, D_MLSTM), fixed),
            pl.BlockSpec((1, D_MLSTM), fixed),
            pl.BlockSpec((D_MLSTM, D_MLSTM), fixed),
            pl.BlockSpec((D_MLSTM, D_MLSTM), fixed),
            pl.BlockSpec((D_MLSTM, D_MLSTM), fixed),
            pl.BlockSpec((nrow, 3 * D_MLSTM), fixed),
            pl.BlockSpec((nrow, 1), fixed),
        ],
        out_specs=[
            pl.BlockSpec((None, seq, D_MLSTM), per_b),
            pl.BlockSpec((None, seq, D_MLSTM), per_b),
            pl.BlockSpec((None, nc, D_MLSTM, MLSTM_CHUNK), lambda i: (i, 0, 0, 0)),
            pl.BlockSpec((None, seq, D_MLSTM), per_b),
            pl.BlockSpec((None, nrow, seq), per_b),
        ],
        out_shape=[
            jax.ShapeDtypeStruct((b, seq, D_MLSTM), F32),
            jax.ShapeDtypeStruct((b, seq, D_MLSTM), BF16),
            jax.ShapeDtypeStruct((b, nc, D_MLSTM, MLSTM_CHUNK), BF16),
            jax.ShapeDtypeStruct((b, seq, D_MLSTM), BF16),
            jax.ShapeDtypeStruct((b, nrow, seq), F32),
        ],
        compiler_params=_params(("parallel",)),
        name="mlstm_pre",
    )(xm, conv_w, conv_b, wq, wkt, wv, wif_rows, bif_rows)


def _log_sigmoid(x):
    return jnp.minimum(x, 0.0) - jnp.log1p(jnp.exp(-jnp.abs(x)))


def _split3(x):
    hi = x.astype(BF16).astype(F32)
    mid = (x - hi).astype(BF16).astype(F32)
    lo = (x - hi - mid).astype(BF16).astype(F32)
    return hi, mid, lo


def _mlstm_kernel(q_ref, kt_ref, v_ref, g_ref, xc_ref, z_ref, ng_ref, sk_ref, o_ref,
                  w_ref, ml_ref, tot_ref, pm_ref, tb_ref, tr_ref, dc_ref, cs_ref, ms_ref):
    L = MLSTM_CHUNK
    dh = MLSTM_HEAD_DIM
    nc = kt_ref.shape[0]
    rows_all = g_ref.shape[1]
    lane = lax.broadcasted_iota(jnp.int32, (rows_all, L), 1)
    sub = lax.broadcasted_iota(jnp.int32, (rows_all, L), 0) % 8
    fwd_row = sub == 0
    row_i = lax.broadcasted_iota(jnp.int32, (L, L), 0)
    col_i = lax.broadcasted_iota(jnp.int32, (L, L), 1)
    ones_col = jnp.ones((L, dh), BF16)

    lf = _log_sigmoid(g_ref[0])
    pre, suf = lf, lf
    d = 1
    while d < L:
        pre = pre + jnp.where(lane >= d, pltpu.roll(pre, d, axis=1), 0.0)
        suf = suf + jnp.where(lane < L - d, pltpu.roll(suf, L - d, axis=1), 0.0)
        d *= 2
    cum = jnp.where(fwd_row, pre, suf)
    tot = jnp.where(fwd_row, cum[:, L - 1:L], cum[:, 0:1])
    a = tot - cum + g_ref[1]
    ml = jnp.max(a, axis=1, keepdims=True)
    w_ref[...] = jnp.exp(a - ml)
    ml_ref[...] = jnp.broadcast_to(ml, (rows_all, L))
    tot_ref[...] = tot
    r = g_ref[1] - cum
    pmax, smax = r, r
    d = 1
    while d < L:
        pmax = jnp.maximum(pmax, jnp.where(lane >= d, pltpu.roll(pmax, d, axis=1), NEG_INF))
        smax = jnp.maximum(smax, jnp.where(lane < L - d, pltpu.roll(smax, L - d, axis=1), NEG_INF))
        d *= 2
    pm_ref[...] = jnp.where(fwd_row, pmax, smax)

    def tile_bcast(x, src):
        y = jnp.where(sub == src, x, 0.0)
        if src:
            y = pltpu.roll(y, rows_all - src, axis=0)
        for s in (1, 2, 4):
            y = y + pltpu.roll(y, s, axis=0)
        return y

    for dr in range(2):
        b_hi, b_mid, b_lo = _split3(tile_bcast(cum, dr))
        r_hi, r_mid, r_lo = _split3(tile_bcast(r, dr))
        tb_ref[dr] = jnp.where(sub == 1, b_hi, jnp.where(sub == 2, b_mid, jnp.where(
            sub == 3, b_lo, jnp.where((sub >= 4) & (sub <= 6), 1.0, 0.0))))
        tr_ref[dr] = jnp.where(sub == 0, 1.0, jnp.where(sub == 4, r_hi, jnp.where(
            sub == 5, r_mid, jnp.where(sub == 6, r_lo, 0.0))))

    def chunk_rows(ref, c):
        return ref[pl.ds(pl.multiple_of(c * 8, 8), 8), :]

    def v_aug(c):
        return jnp.concatenate([v_ref[pl.ds(c * L, L), :], ones_col], axis=1)

    def phase_a(c, carry):
        kt = kt_ref[c].astype(F32)
        w = chunk_rows(w_ref, c)
        kw = jnp.concatenate([kt * w[0:1], kt * w[1:2]], axis=0)
        dc_ref[c] = _dot(kw.astype(BF16), v_aug(c))
        return carry

    lax.fori_loop(0, nc, phase_a, 0, unroll=8)

    ms_ref[...] = jnp.zeros(ms_ref.shape, F32)
    def scan_dir(direction_row, reverse):
        off_rows = direction_row * dh
        off_cols = direction_row * 2 * dh

        def body(i, carry):
            c = (nc - 1 - i) if reverse else i
            state, m = carry
            cs_ref[c, :, off_cols:off_cols + 2 * dh] = state.astype(BF16)
            ms_ref[c, direction_row:direction_row + 1, :] = m
            ml_c = chunk_rows(ml_ref, c)[direction_row:direction_row + 1]
            g_c = chunk_rows(tot_ref, c)[direction_row:direction_row + 1]
            m_new = jnp.maximum(g_c + m, ml_c)
            alpha = jnp.exp(g_c + m - m_new)
            beta = jnp.exp(ml_c - m_new)
            alpha2 = jnp.concatenate([alpha, alpha], axis=1)
            beta2 = jnp.concatenate([beta, beta], axis=1)
            state = alpha2 * state + beta2 * dc_ref[c, off_rows:off_rows + dh, :]
            return state, m_new

        init = (jnp.zeros((dh, 2 * dh), F32), jnp.zeros((1, L), F32))
        lax.fori_loop(0, nc, body, init)

    scan_dir(0, False)
    scan_dir(1, True)

    ng = ng_ref[...]
    sk = sk_ref[...]

    sub8 = lax.broadcasted_iota(jnp.int32, (8, L), 0)
    floor_rows = jnp.where(sub8 == 0, 1.0, jnp.where(sub8 <= 3, -1.0, 0.0))
    no_rows = jnp.zeros((8, 3 * L), F32)

    def direction(s_qk, qc, vaug, ex, keep):
        w = jnp.exp(jnp.where(keep, ex[:, :L], NEG_INF)) * s_qk
        scale = jnp.exp(ex[:, L:2 * L])
        intra = _dot(w.astype(BF16), vaug)
        tot_c = intra + jnp.concatenate([scale, scale], axis=1) * qc
        return tot_c[:, :dh] / jnp.maximum(jnp.abs(tot_c[:, dh:]), jnp.exp(ex[:, 2 * L:]))

    def phase_c(c, carry):
        rows = pl.ds(c * L, L)
        q = q_ref[rows, :]
        s_qk = _dot(q, kt_ref[c])
        qc = _dot(q, cs_ref[c])
        vaug = v_aug(c)
        ms = ms_ref[c]
        e = -jnp.maximum(ms, chunk_rows(pm_ref, c))
        e = (e - jnp.abs(e) * (2.0 ** -7)).astype(BF16).astype(F32)
        lhs_tiles, rhs_tiles = [], []
        for dr in range(2):
            e_rows = jnp.broadcast_to(e[dr:dr + 1], (8, L))
            lhs_tiles.append(jnp.where(sub8 == 0, e_rows, chunk_rows(tb_ref.at[dr], c)))
            m_hi, m_mid, m_lo = _split3(jnp.broadcast_to(ms[dr:dr + 1], (8, L)))
            scale_rows = jnp.where(sub8 == 0, 1.0, jnp.where(sub8 == 4, m_hi, jnp.where(
                sub8 == 5, m_mid, jnp.where(sub8 == 6, m_lo, 0.0))))
            rhs_tiles.append(jnp.concatenate(
                [chunk_rows(tr_ref.at[dr], c), scale_rows, floor_rows], axis=1))
        lhs = jnp.concatenate(lhs_tiles, axis=0).astype(BF16)
        rhs = jnp.concatenate([jnp.concatenate([rhs_tiles[0], no_rows], axis=1),
                               jnp.concatenate([no_rows, rhs_tiles[1]], axis=1)],
                              axis=0).astype(BF16)
        ex = lax.dot_general(lhs, rhs, (((0,), (0,)), ((), ())),
                             preferred_element_type=F32)
        h = (direction(s_qk, qc[:, :2 * dh], vaug, ex[:, :3 * L], col_i <= row_i)
             + direction(s_qk, qc[:, 2 * dh:], vaug, ex[:, 3 * L:], col_i >= row_i))
        hn = _rms(h, ng)
        o_ref[rows, :] = (hn + sk * xc_ref[rows, :]) * _silu(z_ref[rows, :])
        return carry

    lax.fori_loop(0, nc, phase_c, 0, unroll=16)


def _mlstm(q, kt, v, gates, xc, z, norm_g, skip):
    b, seq, _ = q.shape
    nc = seq // MLSTM_CHUNK
    dh = MLSTM_HEAD_DIM
    head = lambda i, j: (i, 0, j)
    vec = lambda i, j: (0, j)
    blk = pl.BlockSpec((None, seq, dh), head)
    gate_rows = pltpu.VMEM((nc * 8, MLSTM_CHUNK), F32)
    gate_tiles = pltpu.VMEM((2, nc * 8, MLSTM_CHUNK), F32)
    return pl.pallas_call(
        _mlstm_kernel,
        grid=(b, MLSTM_HEADS),
        in_specs=[
            blk,
            pl.BlockSpec((None, nc, dh, MLSTM_CHUNK), lambda i, j: (i, 0, j, 0)),
            blk,
            pl.BlockSpec((None, None, 2, nc * 8, MLSTM_CHUNK), lambda i, j: (i, j, 0, 0, 0)),
            blk, blk,
            pl.BlockSpec((1, dh), vec),
            pl.BlockSpec((1, dh), vec),
        ],
        out_specs=blk,
        out_shape=jax.ShapeDtypeStruct((b, seq, D_MLSTM), F32),
        scratch_shapes=[
            gate_rows, gate_rows, gate_rows, gate_rows, gate_tiles, gate_tiles,
            pltpu.VMEM((nc, 2 * dh, 2 * dh), F32),
            pltpu.VMEM((nc, dh, 4 * dh), BF16),
            pltpu.VMEM((nc, 8, MLSTM_CHUNK), F32),
        ],
        compiler_params=_params(("parallel", "parallel")),
        name="mlstm",
    )(q, kt, v, gates, xc, z, norm_g, skip)


def _band_attn_kernel(q_ref, k_ref, v_ref, o_ref, lse_ref, *, half, dil):
    lsub = q_ref.shape[0]
    tq = ATTN_Q_TILE
    win = min(lsub, 2 * tq)
    first = lax.broadcasted_iota(jnp.int32, (1, LANES), 1) < ATTN_HEAD_DIM
    lane = lax.broadcasted_iota(jnp.int32, (tq, LANES), 1)
    rel = (lax.broadcasted_iota(jnp.int32, (2 * tq, win), 1)
           - lax.broadcasted_iota(jnp.int32, (2 * tq, win), 0) % tq)

    def tile(qs, ws, out_rows):
        keep = jnp.abs(rel + (ws - qs)) <= half
        for c in range(dil):
            lse_tile = jnp.zeros((tq, LANES), F32)
            for p in range(D_ATTN // LANES):
                lanes = slice(c * D_ATTN + p * LANES, c * D_ATTN + (p + 1) * LANES)
                q = q_ref[pl.ds(qs, tq), lanes]
                kw = k_ref[pl.ds(ws, win), lanes]
                vw = v_ref[pl.ds(ws, win), lanes]
                zero = jnp.zeros_like(q)
                q2 = jnp.concatenate([jnp.where(first, q, zero), jnp.where(first, zero, q)], axis=0)
                s = jnp.where(keep, _dot_nt(q2, kw), NEG_INF)
                m = jnp.max(s, axis=1, keepdims=True)
                e = jnp.exp(s - m)
                l = jnp.sum(e, axis=1, keepdims=True)
                o2 = _dot(e.astype(BF16), vw) / l
                o_ref[p, out_rows(c), :] = jnp.where(first, o2[:tq], o2[tq:])
                lse2 = m + jnp.log(l)
                lse_tile = jnp.where(lane == 2 * p, lse2[:tq],
                                     jnp.where(lane == 2 * p + 1, lse2[tq:], lse_tile))
            lse_ref[out_rows(c), :] = lse_tile

    if dil == 1:
        def body(t, carry):
            qs = pl.multiple_of(t * tq, tq)
            ws = pl.multiple_of(jnp.clip(qs - half, 0, lsub - win), half)
            tile(qs, ws, lambda c: pl.ds(qs, tq))
            return carry

        lax.fori_loop(0, lsub // tq, body, 0, unroll=8)
    else:
        for t in range(lsub // tq):
            qs = t * tq
            ws = min(max(qs - half, 0), lsub - win)
            tile(qs, ws, lambda c, qs=qs: pl.ds(qs * dil + c, tq, stride=dil))


def _band_attn(q, k, v, seq, win, dil):
    lsub = seq // dil
    b = q.shape[0] // lsub
    half = win // (2 * dil)
    pairs = D_ATTN // LANES
    assert lsub % ATTN_Q_TILE == 0 and half % 16 == 0 and ATTN_Q_TILE + 2 * half <= 2 * ATTN_Q_TILE
    blk = pl.BlockSpec((lsub, dil * D_ATTN), lambda i: (i, 0))
    return pl.pallas_call(
        functools.partial(_band_attn_kernel, half=half, dil=dil),
        grid=(b,),
        in_specs=[blk, blk, blk],
        out_specs=[pl.BlockSpec((None, pairs, seq, LANES), lambda i: (i, 0, 0, 0)),
                   pl.BlockSpec((seq, LANES), lambda i: (i, 0))],
        out_shape=[
            jax.ShapeDtypeStruct((b, pairs, seq, LANES), F32),
            jax.ShapeDtypeStruct((b * seq, LANES), F32),
        ],
        compiler_params=_params(("parallel",)),
        name=f"band_attn_d{dil}",
    )(q, k, v)


def _out_proj_kernel(ym_ref, o1_ref, o2_ref, o3_ref, l1_ref, l2_ref, l3_ref, sp_ref, x_ref, ag_ref,
                     w_ref, n2_ref, wr_ref, x2_ref, h2_ref, lg_ref):
    spread = sp_ref[...]
    wr = wr_ref[...]
    w_hi = wr.astype(BF16)
    w_lo = (wr - w_hi.astype(F32)).astype(BF16)
    wr3 = jnp.concatenate([w_hi, w_hi, w_lo], axis=0)

    def per_lane(w):
        hi = w.astype(BF16)
        lo = (w - hi.astype(F32)).astype(BF16)
        return _dot(jnp.concatenate([hi, lo], axis=1), spread)

    rows = x_ref.shape[0] // OUT_PROJ_SPLITS
    for part in range(OUT_PROJ_SPLITS):
        rs = slice(part * rows, (part + 1) * rows)
        lses = [r[rs, :] for r in (l1_ref, l2_ref, l3_ref)]
        top = jnp.maximum(jnp.maximum(lses[0], lses[1]), lses[2])
        wts = [jnp.exp(l - top) for l in lses]
        total = wts[0] + wts[1] + wts[2]

        def heads(o_ref):
            return jnp.concatenate([o_ref[p, rs, :] for p in range(o_ref.shape[0])], axis=1)

        ya = sum(per_lane(w / total) * heads(o) for w, o in zip(wts, (o1_ref, o2_ref, o3_ref)))
        ya = _rms(ya, ag_ref[...])
        mixed = jnp.concatenate([ym_ref[rs, :], ya], axis=1).astype(BF16)
        x2 = x_ref[rs, :] + _dot(mixed, w_ref[...])
        x2_ref[rs, :] = x2
        h2 = _rms(x2, n2_ref[...])
        hi = h2.astype(BF16)
        h2_ref[rs, :] = hi
        lo = (h2 - hi.astype(F32)).astype(BF16)
        lg_ref[rs, :] = _dot(jnp.concatenate([hi, lo, hi], axis=1), wr3)


def _out_proj(ym, branch_o, branch_lse, x2d, attn_g, w_bf, n2g, wr_pad):
    n = x2d.shape[0]
    spread = (jnp.arange(LANES)[:, None] == jnp.arange(D_ATTN)[None, :] // ATTN_HEAD_DIM)
    spread = jnp.tile(spread.astype(BF16), (2, 1))
    pairs, seq = branch_o[0].shape[1:3]
    tiles_per_seq = seq // ROW_TILE
    branch = pl.BlockSpec((None, pairs, ROW_TILE, LANES),
                          lambda i: (i // tiles_per_seq, 0, i % tiles_per_seq, 0))
    row = lambda i: (i, 0)
    fixed = lambda i: (0, 0)
    return pl.pallas_call(
        _out_proj_kernel,
        grid=(n // ROW_TILE,),
        in_specs=[
            pl.BlockSpec((ROW_TILE, D_MLSTM), row),
            branch, branch, branch,
            pl.BlockSpec((ROW_TILE, LANES), row),
            pl.BlockSpec((ROW_TILE, LANES), row),
            pl.BlockSpec((ROW_TILE, LANES), row),
            pl.BlockSpec((2 * LANES, D_ATTN), fixed),
            pl.BlockSpec((ROW_TILE, D_MODEL), row),
            pl.BlockSpec((1, D_ATTN), fixed),
            pl.BlockSpec((D_MODEL, D_MODEL), fixed),
            pl.BlockSpec((1, D_MODEL), fixed),
            pl.BlockSpec((D_MODEL, LANES), fixed),
        ],
        out_specs=[
            pl.BlockSpec((ROW_TILE, D_MODEL), row),
            pl.BlockSpec((ROW_TILE, D_MODEL), row),
            pl.BlockSpec((ROW_TILE, LANES), row),
        ],
        out_shape=[
            jax.ShapeDtypeStruct((n, D_MODEL), F32),
            jax.ShapeDtypeStruct((n, D_MODEL), BF16),
            jax.ShapeDtypeStruct((n, LANES), F32),
        ],
        compiler_params=_params(("parallel",)),
        name="out_proj",
    )(ym, *branch_o, *branch_lse, spread, x2d, attn_g, w_bf, n2g, wr_pad)


def _route_kernel(lg_ref, tri_ref, eye_ref, slot_ref, slot_t_ref, aff_ref, *, cap):
    for i in range(lg_ref.shape[0]):
        _route_one(lg_ref.at[i], tri_ref, eye_ref, slot_ref.at[i], slot_t_ref.at[i],
                   aff_ref.at[i], cap=cap)


def _route_one(lg_ref, tri_ref, eye_ref, slot_ref, slot_t_ref, aff_ref, *, cap):
    lg = lg_ref[...]
    valid = lax.broadcasted_iota(jnp.int32, (1, LANES), 1) < N_EXPERTS
    lg = jnp.where(valid, lg, NEG_INF)
    e = jnp.exp(lg - jnp.max(lg, axis=1, keepdims=True))
    aff = e / jnp.sum(e, axis=1, keepdims=True)
    aff_ref[...] = aff
    groups = LANES // N_EXPERTS
    rpg = aff.shape[0] // groups
    lane = lax.broadcasted_iota(jnp.int32, (1, LANES), 1)
    packed = aff[:rpg]
    for g in range(1, groups):
        packed = packed + pltpu.roll(aff[g * rpg:(g + 1) * rpg], g * N_EXPERTS, axis=1)

    def over_groups(x):
        shift = N_EXPERTS
        while shift < LANES:
            x = x + pltpu.roll(x, shift, axis=1)
            shift *= 2
        return x

    def enough(cand):
        part = jnp.sum(jnp.where(packed >= cand, 1.0, 0.0), axis=0, keepdims=True)
        return over_groups(part) >= cap

    def narrow(lo, hi, cands):
        new_lo, new_hi = lo, hi
        for cand in cands:
            ok = enough(cand)
            new_lo = jnp.maximum(new_lo, jnp.where(ok, cand, lo))
            new_hi = jnp.minimum(new_hi, jnp.where(ok, hi, cand))
        return new_lo, new_hi

    tiny = jnp.full((1, LANES), 2.0 ** -126, F32)
    normal = enough(tiny)
    p = tiny
    for span, count in ((16, 7), (1, 15)):
        p, _ = narrow(p, p, [p * (2.0 ** (span * j)) for j in range(1, count + 1)])
    lo = jnp.where(normal, p, 0.0)
    hi = jnp.where(normal, p * 2.0, tiny)
    width = jnp.where(normal, p, 0.0)
    for bits in THRESHOLD_RADIX_BITS:
        width = width * (0.5 ** bits)
        lo, hi = narrow(lo, hi, [lo + j * width for j in range(1, 2 ** bits)])
    gt = jnp.where(packed >= hi, 1.0, 0.0)
    eq = jnp.where(packed >= lo, 1.0, 0.0) - gt
    need = cap - over_groups(jnp.sum(gt, axis=0, keepdims=True))
    tri = tri_ref[...]

    def count_before(x):
        per_group = jnp.sum(x, axis=0, keepdims=True)
        upto = per_group
        shift = N_EXPERTS
        while shift < LANES:
            upto = upto + jnp.where(lane >= shift, pltpu.roll(upto, shift, axis=1), 0.0)
            shift *= 2
        return _dot(tri, x.astype(BF16)) + (upto - per_group)

    sel = gt + eq * jnp.where(count_before(eq) < need, 1.0, 0.0)
    pos = count_before(sel)
    slot_packed = jnp.where(sel > 0.0, pos, -1.0)
    slot = jnp.concatenate(
        [jnp.where(valid, slot_packed if g == 0 else
                   pltpu.roll(slot_packed, LANES - g * N_EXPERTS, axis=1), -1.0)
         for g in range(groups)], axis=0)
    slot_ref[...] = slot
    slot_t_ref[...] = _dot_nt(eye_ref[...], slot.astype(BF16))


def _route(logits, cap):
    b, seq, _ = logits.shape
    assert LANES % N_EXPERTS == 0 and seq % (LANES // N_EXPERTS) == 0
    rpg = seq // (LANES // N_EXPERTS)
    tri = (jnp.arange(rpg)[None, :] < jnp.arange(rpg)[:, None]).astype(BF16)
    eye = jnp.eye(LANES, dtype=BF16)
    per_b = lambda i: (i, 0, 0)
    fixed = lambda i: (0, 0)
    nb = ROUTE_BATCHES if b % ROUTE_BATCHES == 0 else 1
    return pl.pallas_call(
        functools.partial(_route_kernel, cap=cap),
        grid=(b // nb,),
        in_specs=[
            pl.BlockSpec((nb, seq, LANES), per_b),
            pl.BlockSpec((rpg, rpg), fixed),
            pl.BlockSpec((LANES, LANES), fixed),
        ],
        out_specs=[
            pl.BlockSpec((nb, seq, LANES), per_b),
            pl.BlockSpec((nb, LANES, seq), per_b),
            pl.BlockSpec((nb, seq, LANES), per_b),
        ],
        out_shape=[
            jax.ShapeDtypeStruct((b, seq, LANES), F32),
            jax.ShapeDtypeStruct((b, LANES, seq), F32),
            jax.ShapeDtypeStruct((b, seq, LANES), F32),
        ],
        compiler_params=_params(("parallel",)),
        name="route",
    )(logits, tri, eye)


def _moe_gather_kernel(slot_ref, h_ref, xs_ref):
    srow = slot_ref[...]
    cap, seq = xs_ref.shape[0], srow.shape[1]
    ci = lax.broadcasted_iota(jnp.int32, (cap, seq), 0).astype(F32)
    onehot = jnp.where(srow == ci, 1.0, 0.0).astype(BF16)
    xs_ref[...] = _dot(onehot, h_ref[...]).astype(BF16)


def _moe_gather(slot_t, h2, cap):
    b, seq, _ = h2.shape
    return pl.pallas_call(
        _moe_gather_kernel,
        grid=(b, N_EXPERTS),
        in_specs=[
            pl.BlockSpec((None, None, 1, seq), lambda i, e: (i, e, 0, 0)),
            pl.BlockSpec((None, seq, D_MODEL), lambda i, e: (i, 0, 0)),
        ],
        out_specs=pl.BlockSpec((None, None, cap, D_MODEL), lambda i, e: (i, e, 0, 0)),
        out_shape=jax.ShapeDtypeStruct((b, N_EXPERTS, cap, D_MODEL), BF16),
        compiler_params=_params(("parallel", "parallel")),
        name="moe_gather",
    )(slot_t, h2)


def _moe_ffn_kernel(xs_ref, w1_ref, w3_ref, w2_ref, y_ref, act_ref, w1b_ref, w3b_ref, w2b_ref):
    s = pl.program_id(1)
    nb, cap, _ = xs_ref.shape
    nf = act_ref.shape[0]
    per = FFN_ROW_TILE // cap
    row_tiles = nb // per

    @pl.when(s < nf)
    def _():
        w1b_ref[...] = w1_ref[...].astype(BF16)
        w3b_ref[...] = w3_ref[...].astype(BF16)
        for r in range(row_tiles):
            x = xs_ref[r * per:(r + 1) * per].reshape(FFN_ROW_TILE, D_MODEL)
            up = _dot(x, w1b_ref[...])
            gt = _dot(x, w3b_ref[...])
            act_ref[s, r * FFN_ROW_TILE:(r + 1) * FFN_ROW_TILE, :] = (_silu(up) * gt).astype(BF16)

    @pl.when(s >= nf)
    def _():
        w2b_ref[...] = w2_ref[...].astype(BF16)
        for r in range(row_tiles):
            rows = slice(r * FFN_ROW_TILE, (r + 1) * FFN_ROW_TILE)
            act = jnp.concatenate([act_ref[f, rows, :] for f in range(nf)], axis=1)
            y = _dot(act, w2b_ref[...])
            y_ref[r * per:(r + 1) * per] = y.astype(BF16).reshape(per, cap, y.shape[1])


def _moe_ffn(xs, w1, w3, w2):
    b, ne, cap, _ = xs.shape
    nf = D_EXPERT // FFN_F_TILE
    nn = D_MODEL // FFN_N_TILE
    hidden = lambda e, s: (e, 0, jnp.minimum(s, nf - 1))
    out_col = lambda e, s: jnp.maximum(s - nf, 0)
    return pl.pallas_call(
        _moe_ffn_kernel,
        grid=(ne, nf + nn),
        in_specs=[
            pl.BlockSpec((b, None, cap, D_MODEL), lambda e, s: (0, e, 0, 0)),
            pl.BlockSpec((None, D_MODEL, FFN_F_TILE), hidden),
            pl.BlockSpec((None, D_MODEL, FFN_F_TILE), hidden),
            pl.BlockSpec((None, D_EXPERT, FFN_N_TILE), lambda e, s: (e, 0, out_col(e, s))),
        ],
        out_specs=pl.BlockSpec((b, None, cap, FFN_N_TILE), lambda e, s: (0, e, 0, out_col(e, s))),
        out_shape=jax.ShapeDtypeStruct(xs.shape, BF16),
        scratch_shapes=[
            pltpu.VMEM((nf, b * cap, FFN_F_TILE), BF16),
            pltpu.VMEM((D_MODEL, FFN_F_TILE), BF16),
            pltpu.VMEM((D_MODEL, FFN_F_TILE), BF16),
            pltpu.VMEM((D_EXPERT, FFN_N_TILE), BF16),
        ],
        compiler_params=_params(("parallel", "arbitrary")),
        name="moe_ffn",
    )(xs, w1, w3, w2)


def _moe_scatter_kernel(slot_ref, aff_ref, y_ref, x2_ref, g_ref, o_ref):
    slot = slot_ref[...]
    aff = aff_ref[...]
    rows, cap = slot.shape[0], y_ref.shape[1]
    ci = lax.broadcasted_iota(jnp.int32, (rows, cap), 1).astype(F32)
    acc = x2_ref[...]
    for e in range(N_EXPERTS):
        onehot = jnp.where(slot[:, e:e + 1] == ci, 1.0, 0.0).astype(BF16)
        acc = acc + aff[:, e:e + 1] * _dot(onehot, y_ref[e])
    o_ref[...] = _rms(acc, g_ref[...])


def _moe_scatter(slot, aff, y, x2, norm_g):
    b, seq, _ = x2.shape
    cap = y.shape[2]
    tile = lambda i, r: (i, r, 0)
    return pl.pallas_call(
        _moe_scatter_kernel,
        grid=(b, seq // ROW_TILE),
        in_specs=[
            pl.BlockSpec((None, ROW_TILE, LANES), tile),
            pl.BlockSpec((None, ROW_TILE, LANES), tile),
            pl.BlockSpec((None, N_EXPERTS, cap, D_MODEL), lambda i, r: (i, 0, 0, 0)),
            pl.BlockSpec((None, ROW_TILE, D_MODEL), tile),
            pl.BlockSpec((1, D_MODEL), lambda i, r: (0, 0)),
        ],
        out_specs=pl.BlockSpec((None, ROW_TILE, D_MODEL), tile),
        out_shape=jax.ShapeDtypeStruct((b, seq, D_MODEL), F32),
        compiler_params=_params(("parallel", "parallel")),
        name="moe_scatter",
    )(slot, aff, y, x2, norm_g)


def kernel(x, norm1_g, w_in, conv_w, conv_b, wq_m, wk_m, wv_m, w_if_fwd, b_if_fwd,
           w_if_bwd, b_if_bwd, mlstm_norm_g, mlstm_skip, attn_norm_g, w_out, norm2_g,
           w_router, w1, w3, w2, norm_f_g):
    b, seq, _ = x.shape
    assert w_in.shape[0] == 1, "single-layer problem"
    assert seq % ROW_TILE == 0 and seq % MLSTM_CHUNK == 0 and seq % ATTN_Q_TILE == 0
    cap = EC_CAPACITY * seq // N_EXPERTS
    assert FFN_ROW_TILE % cap == 0 and (b * cap) % FFN_ROW_TILE == 0
    nc = seq // MLSTM_CHUNK
    l = 0
    x2d = x.reshape(b * seq, D_MODEL)
    xm, z, *qkv_views = _in_proj(x2d, norm1_g[l][None, :], w_in[l].astype(BF16), seq)
    shp = lambda t: t.reshape(b, seq, t.shape[-1])
    wif_rows, bif_rows = _gate_rows(w_if_fwd[l], b_if_fwd[l], w_if_bwd[l], b_if_bwd[l])
    xc, qm, ktm, vm, gates = _mlstm_pre(
        shp(xm), conv_w[l], conv_b[l][None, :],
        _block_diag(wq_m[l]).astype(BF16), _block_diag(wk_m[l]).T.astype(BF16),
        _block_diag(wv_m[l]).astype(BF16), wif_rows.astype(BF16), bif_rows)
    gates = gates.reshape(b, 2, MLSTM_HEADS, 8, nc, MLSTM_CHUNK).transpose(0, 2, 1, 4, 3, 5)
    gates = gates.reshape(b, MLSTM_HEADS, 2, nc * 8, MLSTM_CHUNK)
    ym = _mlstm(qm, ktm, vm, gates, xc, shp(z), mlstm_norm_g[l][None, :],
                mlstm_skip[l][None, :])
    branches = [_band_attn(*qkv_views[3 * d:3 * d + 3], seq, win, dil)
                for d, (win, dil) in enumerate(DILATED_PATTERNS)]
    wr_pad = jnp.pad(w_router[l], ((0, 0), (0, LANES - N_EXPERTS)))
    x2, h2, logits = _out_proj(
        ym.reshape(b * seq, D_MLSTM), [o for o, _ in branches], [s for _, s in branches], x2d,
        attn_norm_g[l][None, :], w_out[l].astype(BF16), norm2_g[l][None, :], wr_pad)
    slot, slot_t, aff = _route(logits.reshape(b, seq, LANES), cap)
    xs = _moe_gather(slot_t.reshape(b, LANES, 1, seq), h2.reshape(b, seq, D_MODEL), cap)
    y = _moe_ffn(xs, w1[l], w3[l], w2[l])
    return _moe_scatter(slot, aff, y, x2.reshape(b, seq, D_MODEL), norm_f_g[None, :])
```

```python
import functools

import jax
import jax.numpy as jnp
from jax import lax
from jax.experimental import pallas as pl
from jax.experimental.pallas import tpu as pltpu

F32 = jnp.float32
BF16 = jnp.bfloat16

D_MODEL = 1024
D_MLSTM = 512
D_ATTN = 512
D_IN_PROJ = 2 * D_MLSTM + 3 * D_ATTN
MLSTM_HEADS = 4
MLSTM_HEAD_DIM = 128
MLSTM_QKV_BLOCK = 4
MLSTM_CONV = 5
ATTN_HEAD_DIM = 64
ROPE_DIM = 16
ROPE_THETA = 500000.0
DILATED_PATTERNS = ((128, 1), (512, 4), (2048, 16))
N_EXPERTS = 16
EC_CAPACITY = 2
D_EXPERT = 2816
NORM_EPS = 1e-6
NEG_INF = -1e30

LANES = 128
MLSTM_CHUNK = 128
ROW_TILE = 512
OUT_PROJ_SPLITS = 2
ROUTE_BATCHES = 2
ATTN_Q_TILE = 128
FFN_F_TILE = 256
FFN_N_TILE = 256
FFN_ROW_TILE = 512
THRESHOLD_RADIX_BITS = (4, 4, 4, 4, 4, 3, 4, 4, 4, 4)
V7X_VMEM_BYTES = 64 * 1024 * 1024
VMEM_LIMIT = V7X_VMEM_BYTES * 7 // 8


def _params(sem):
    return pltpu.CompilerParams(dimension_semantics=sem, vmem_limit_bytes=VMEM_LIMIT)


def _rms(x, g):
    return x * lax.rsqrt(jnp.mean(x * x, axis=-1, keepdims=True) + NORM_EPS) * g


def _silu(x):
    return x * (1.0 / (1.0 + jnp.exp(-x)))


def _dot(a, b):
    return jnp.dot(a, b, preferred_element_type=F32)


def _dot_nt(a, b):
    return lax.dot_general(a, b, (((1,), (1,)), ((), ())), preferred_element_type=F32)


def _in_proj_kernel(x_ref, g_ref, w_ref, cos_ref, sa_ref, sb_ref, xm_ref, z_ref, *rest):
    qkv_refs, scr_ref = rest[:-1], rest[-1]
    h = _rms(x_ref[...], g_ref[...])
    p = _dot(h.astype(BF16), w_ref[...])
    xm_ref[...] = p[:, :D_MLSTM]
    z_ref[...] = p[:, D_MLSTM:2 * D_MLSTM]
    cos, sa, sb = cos_ref[...], sa_ref[...], sb_ref[...]
    half = ROPE_DIM // 2

    def rope(t):
        outs = []
        for j in range(D_ATTN // LANES):
            tj = t[:, j * LANES:(j + 1) * LANES]
            up = pltpu.roll(tj, LANES - half, axis=1)
            dn = pltpu.roll(tj, half, axis=1)
            outs.append(tj * cos + up * sa + dn * sb)
        return jnp.concatenate(outs, axis=1)

    o = 2 * D_MLSTM
    qkv = (rope(p[:, o:o + D_ATTN]) * (ATTN_HEAD_DIM ** -0.5),
           rope(p[:, o + D_ATTN:o + 2 * D_ATTN]),
           p[:, o + 2 * D_ATTN:])
    rows = p.shape[0]
    groups = D_ATTN // LANES
    for a, val in enumerate(qkv):
        for j in range(groups):
            scr_ref[j] = val[:, j * LANES:(j + 1) * LANES]
        for d, (_, dil) in enumerate(DILATED_PATTERNS):
            ref = qkv_refs[3 * d + a]
            if dil == 1:
                ref[...] = val.astype(BF16)
                continue
            for r in range(dil):
                for j in range(groups):
                    piece = scr_ref[j, pl.ds(r, rows // dil, stride=dil), :]
                    ref[:, r * D_ATTN + j * LANES:r * D_ATTN + (j + 1) * LANES] = piece.astype(BF16)


def _rope_tables(seq):
    half = ROPE_DIM // 2
    inv_freq = ROPE_THETA ** (-2.0 * jnp.arange(half, dtype=F32) / ROPE_DIM)
    ang = jnp.arange(seq).astype(F32)[:, None] * inv_freq[None, :]
    cos, sin = jnp.cos(ang), jnp.sin(ang)
    pad = jnp.zeros((seq, ATTN_HEAD_DIM - ROPE_DIM), F32)
    cos_h = jnp.concatenate([cos, cos, pad + 1.0], axis=1)
    sa_h = jnp.concatenate([-sin, jnp.zeros_like(sin), pad], axis=1)
    sb_h = jnp.concatenate([jnp.zeros_like(sin), sin, pad], axis=1)
    rep = LANES // ATTN_HEAD_DIM
    return tuple(jnp.tile(t, (1, rep)) for t in (cos_h, sa_h, sb_h))


def _in_proj(x2d, g, w_bf, seq):
    n = x2d.shape[0]
    tiles_per_seq = seq // ROW_TILE
    cos, sa, sb = _rope_tables(seq)
    row = lambda i: (i, 0)
    fixed = lambda i: (0, 0)
    pos = lambda i: (i % tiles_per_seq, 0)
    return pl.pallas_call(
        _in_proj_kernel,
        grid=(n // ROW_TILE,),
        in_specs=[
            pl.BlockSpec((ROW_TILE, D_MODEL), row),
            pl.BlockSpec((1, D_MODEL), fixed),
            pl.BlockSpec((D_MODEL, D_IN_PROJ), fixed),
            pl.BlockSpec((ROW_TILE, LANES), pos),
            pl.BlockSpec((ROW_TILE, LANES), pos),
            pl.BlockSpec((ROW_TILE, LANES), pos),
        ],
        out_specs=[
            pl.BlockSpec((ROW_TILE, D_MLSTM), row),
            pl.BlockSpec((ROW_TILE, D_MLSTM), row),
        ] + [pl.BlockSpec((ROW_TILE // dil, dil * D_ATTN), row)
             for _, dil in DILATED_PATTERNS for _ in range(3)],
        out_shape=[
            jax.ShapeDtypeStruct((n, D_MLSTM), F32),
            jax.ShapeDtypeStruct((n, D_MLSTM), F32),
        ] + [jax.ShapeDtypeStruct((n // dil, dil * D_ATTN), BF16)
             for _, dil in DILATED_PATTERNS for _ in range(3)],
        scratch_shapes=[pltpu.VMEM((D_ATTN // LANES, ROW_TILE, LANES), F32)],
        compiler_params=_params(("parallel",)),
        name="in_proj",
    )(x2d, g, w_bf, cos, sa, sb)


def _mlstm_pre_kernel(xm_ref, cw_ref, cb_ref, wq_ref, wkt_ref, wv_ref, wif_ref, bif_ref,
                      xc_ref, q_ref, kt_ref, v_ref, g_ref):
    x = xm_ref[...]
    seq = x.shape[0]
    t = lax.broadcasted_iota(jnp.int32, x.shape, 0)
    acc = jnp.zeros_like(x) + cb_ref[...]
    for j in range(MLSTM_CONV):
        d = j - MLSTM_CONV // 2
        if d == 0:
            tap = x
        else:
            tap = pltpu.roll(x, (-d) % seq, axis=0)
            tap = jnp.where((t + d >= 0) & (t + d < seq), tap, 0.0)
        acc = acc + tap * cw_ref[j:j + 1, :]
    xc = _silu(acc)
    xc_ref[...] = xc
    xcb = xc.astype(BF16)
    q = _dot(xcb, wq_ref[...]).astype(BF16)
    kt = (_dot_nt(wkt_ref[...], xcb) * (MLSTM_HEAD_DIM ** -0.5)).astype(BF16)
    v = _dot(x.astype(BF16), wv_ref[...]).astype(BF16)
    q_ref[...] = q
    v_ref[...] = v
    L = MLSTM_CHUNK
    for c in range(seq // L):
        kt_ref[c] = kt[:, c * L:(c + 1) * L]
    wif = wif_ref[...]
    g = (_dot_nt(wif[:, :D_MLSTM], q) + _dot(wif[:, D_MLSTM:2 * D_MLSTM], kt)
         + _dot_nt(wif[:, 2 * D_MLSTM:], v) + bif_ref[...])
    for kind in range(2):
        for hd in range(MLSTM_HEADS):
            r0 = (kind * MLSTM_HEADS + hd) * 8
            for c in range(seq // L):
                g_ref[hd, kind, c * 8:(c + 1) * 8, :] = g[r0:r0 + 8, c * L:(c + 1) * L]


def _block_diag(w):
    nblk = w.shape[0]
    n = nblk * MLSTM_QKV_BLOCK
    tiled = jnp.tile(w.reshape(n, MLSTM_QKV_BLOCK), (1, nblk))
    blk = jnp.arange(n) // MLSTM_QKV_BLOCK
    return jnp.where(blk[:, None] == blk[None, :], tiled, 0.0)


def _gate_rows(w_f, b_f, w_b, b_b):
    h = MLSTM_HEADS

    def rows(f, b):
        pair = jnp.stack([f, b], axis=-1)
        pair = jnp.concatenate([pair[..., h:, :], pair[..., :h, :]], axis=-2)
        pair = jnp.pad(pair, [(0, 0)] * (pair.ndim - 1) + [(0, 6)])
        return pair.reshape(*pair.shape[:-2], 2 * h * 8)

    return rows(w_f, w_b).T, rows(b_f, b_b)[:, None]


def _mlstm_pre(xm, conv_w, conv_b, wq, wkt, wv, wif_rows, bif_rows):
    b, seq, _ = xm.shape
    nrow = wif_rows.shape[0]
    nc = seq // MLSTM_CHUNK
    per_b = lambda i: (i, 0, 0)
    fixed = lambda i: (0, 0)
    return pl.pallas_call(
        _mlstm_pre_kernel,
        grid=(b,),
        in_specs=[
            pl.BlockSpec((None, seq, D_MLSTM), per_b),
            pl.BlockSpec((MLSTM_CONV, D_MLSTM), fixed),
            pl.BlockSpec((1, D_MLSTM), fixed),
            pl.BlockSpec((D_MLSTM, D_MLSTM), fixed),
            pl.BlockSpec((D_MLSTM, D_MLSTM), fixed),
            pl.BlockSpec((D_MLSTM, D_MLSTM), fixed),
            pl.BlockSpec((nrow, 3 * D_MLSTM), fixed),
            pl.BlockSpec((nrow, 1), fixed),
        ],
        out_specs=[
            pl.BlockSpec((None, seq, D_MLSTM), per_b),
            pl.BlockSpec((None, seq, D_MLSTM), per_b),
            pl.BlockSpec((None, nc, D_MLSTM, MLSTM_CHUNK), lambda i: (i, 0, 0, 0)),
            pl.BlockSpec((None, seq, D_MLSTM), per_b),
            pl.BlockSpec((None, MLSTM_HEADS, 2, nc * 8, MLSTM_CHUNK), lambda i: (i, 0, 0, 0, 0)),
        ],
        out_shape=[
            jax.ShapeDtypeStruct((b, seq, D_MLSTM), F32),
            jax.ShapeDtypeStruct((b, seq, D_MLSTM), BF16),
            jax.ShapeDtypeStruct((b, nc, D_MLSTM, MLSTM_CHUNK), BF16),
            jax.ShapeDtypeStruct((b, seq, D_MLSTM), BF16),
            jax.ShapeDtypeStruct((b, MLSTM_HEADS, 2, nc * 8, MLSTM_CHUNK), F32),
        ],
        compiler_params=_params(("parallel",)),
        name="mlstm_pre",
    )(xm, conv_w, conv_b, wq, wkt, wv, wif_rows, bif_rows)


def _log_sigmoid(x):
    return jnp.minimum(x, 0.0) - jnp.log1p(jnp.exp(-jnp.abs(x)))


def _split3(x):
    hi = x.astype(BF16).astype(F32)
    mid = (x - hi).astype(BF16).astype(F32)
    lo = (x - hi - mid).astype(BF16).astype(F32)
    return hi, mid, lo


def _mlstm_kernel(q_ref, kt_ref, v_ref, g_ref, xc_ref, z_ref, ng_ref, sk_ref, o_ref,
                  w_ref, ml_ref, tot_ref, pm_ref, tb_ref, tr_ref, dc_ref, cs_ref, ms_ref):
    L = MLSTM_CHUNK
    dh = MLSTM_HEAD_DIM
    nc = kt_ref.shape[0]
    rows_all = g_ref.shape[1]
    lane = lax.broadcasted_iota(jnp.int32, (rows_all, L), 1)
    sub = lax.broadcasted_iota(jnp.int32, (rows_all, L), 0) % 8
    fwd_row = sub == 0
    row_i = lax.broadcasted_iota(jnp.int32, (L, L), 0)
    col_i = lax.broadcasted_iota(jnp.int32, (L, L), 1)
    ones_col = jnp.ones((L, dh), BF16)

    lf = _log_sigmoid(g_ref[0])
    pre, suf = lf, lf
    d = 1
    while d < L:
        pre = pre + jnp.where(lane >= d, pltpu.roll(pre, d, axis=1), 0.0)
        suf = suf + jnp.where(lane < L - d, pltpu.roll(suf, L - d, axis=1), 0.0)
        d *= 2
    cum = jnp.where(fwd_row, pre, suf)
    tot = jnp.where(fwd_row, cum[:, L - 1:L], cum[:, 0:1])
    a = tot - cum + g_ref[1]
    ml = jnp.max(a, axis=1, keepdims=True)
    w_ref[...] = jnp.exp(a - ml)
    ml_ref[...] = jnp.broadcast_to(ml, (rows_all, L))
    tot_ref[...] = tot
    r = g_ref[1] - cum
    pmax, smax = r, r
    d = 1
    while d < L:
        pmax = jnp.maximum(pmax, jnp.where(lane >= d, pltpu.roll(pmax, d, axis=1), NEG_INF))
        smax = jnp.maximum(smax, jnp.where(lane < L - d, pltpu.roll(smax, L - d, axis=1), NEG_INF))
        d *= 2
    pm_ref[...] = jnp.where(fwd_row, pmax, smax)

    def tile_bcast(x, src):
        y = jnp.where(sub == src, x, 0.0)
        if src:
            y = pltpu.roll(y, rows_all - src, axis=0)
        for s in (1, 2, 4):
            y = y + pltpu.roll(y, s, axis=0)
        return y

    for dr in range(2):
        b_hi, b_mid, b_lo = _split3(tile_bcast(cum, dr))
        r_hi, r_mid, r_lo = _split3(tile_bcast(r, dr))
        tb_ref[dr] = jnp.where(sub == 1, b_hi, jnp.where(sub == 2, b_mid, jnp.where(
            sub == 3, b_lo, jnp.where((sub >= 4) & (sub <= 6), 1.0, 0.0))))
        tr_ref[dr] = jnp.where(sub == 0, 1.0, jnp.where(sub == 4, r_hi, jnp.where(
            sub == 5, r_mid, jnp.where(sub == 6, r_lo, 0.0))))

    def chunk_rows(ref, c):
        return ref[pl.ds(pl.multiple_of(c * 8, 8), 8), :]

    def v_aug(c):
        return jnp.concatenate([v_ref[pl.ds(c * L, L), :], ones_col], axis=1)

    def phase_a(c, carry):
        kt = kt_ref[c].astype(F32)
        w = chunk_rows(w_ref, c)
        kw = jnp.concatenate([kt * w[0:1], kt * w[1:2]], axis=0)
        dc_ref[c] = _dot(kw.astype(BF16), v_aug(c))
        return carry

    lax.fori_loop(0, nc, phase_a, 0, unroll=8)

    ms_ref[...] = jnp.zeros(ms_ref.shape, F32)
    def scan_dir(direction_row, reverse):
        off_rows = direction_row * dh
        off_cols = direction_row * 2 * dh

        def body(i, carry):
            c = (nc - 1 - i) if reverse else i
            state, m = carry
            cs_ref[c, :, off_cols:off_cols + 2 * dh] = state.astype(BF16)
            ms_ref[c, direction_row:direction_row + 1, :] = m
            ml_c = chunk_rows(ml_ref, c)[direction_row:direction_row + 1]
            g_c = chunk_rows(tot_ref, c)[direction_row:direction_row + 1]
            m_new = jnp.maximum(g_c + m, ml_c)
            alpha = jnp.exp(g_c + m - m_new)
            beta = jnp.exp(ml_c - m_new)
            alpha2 = jnp.concatenate([alpha, alpha], axis=1)
            beta2 = jnp.concatenate([beta, beta], axis=1)
            state = alpha2 * state + beta2 * dc_ref[c, off_rows:off_rows + dh, :]
            return state, m_new

        init = (jnp.zeros((dh, 2 * dh), F32), jnp.zeros((1, L), F32))
        lax.fori_loop(0, nc, body, init)

    scan_dir(0, False)
    scan_dir(1, True)

    ng = ng_ref[...]
    sk = sk_ref[...]

    sub8 = lax.broadcasted_iota(jnp.int32, (8, L), 0)
    floor_rows = jnp.where(sub8 == 0, 1.0, jnp.where(sub8 <= 3, -1.0, 0.0))
    no_rows = jnp.zeros((8, 3 * L), F32)

    def direction(s_qk, qc, vaug, ex, keep):
        w = jnp.exp(jnp.where(keep, ex[:, :L], NEG_INF)) * s_qk
        scale = jnp.exp(ex[:, L:2 * L])
        intra = _dot(w.astype(BF16), vaug)
        tot_c = intra + jnp.concatenate([scale, scale], axis=1) * qc
        return tot_c[:, :dh] / jnp.maximum(jnp.abs(tot_c[:, dh:]), jnp.exp(ex[:, 2 * L:]))

    def phase_c(c, carry):
        rows = pl.ds(c * L, L)
        q = q_ref[rows, :]
        s_qk = _dot(q, kt_ref[c])
        qc = _dot(q, cs_ref[c])
        vaug = v_aug(c)
        ms = ms_ref[c]
        e = -jnp.maximum(ms, chunk_rows(pm_ref, c))
        e = (e - jnp.abs(e) * (2.0 ** -7)).astype(BF16).astype(F32)
        lhs_tiles, rhs_tiles = [], []
        for dr in range(2):
            e_rows = jnp.broadcast_to(e[dr:dr + 1], (8, L))
            lhs_tiles.append(jnp.where(sub8 == 0, e_rows, chunk_rows(tb_ref.at[dr], c)))
            m_hi, m_mid, m_lo = _split3(jnp.broadcast_to(ms[dr:dr + 1], (8, L)))
            scale_rows = jnp.where(sub8 == 0, 1.0, jnp.where(sub8 == 4, m_hi, jnp.where(
                sub8 == 5, m_mid, jnp.where(sub8 == 6, m_lo, 0.0))))
            rhs_tiles.append(jnp.concatenate(
                [chunk_rows(tr_ref.at[dr], c), scale_rows, floor_rows], axis=1))
        lhs = jnp.concatenate(lhs_tiles, axis=0).astype(BF16)
        rhs = jnp.concatenate([jnp.concatenate([rhs_tiles[0], no_rows], axis=1),
                               jnp.concatenate([no_rows, rhs_tiles[1]], axis=1)],
                              axis=0).astype(BF16)
        ex = lax.dot_general(lhs, rhs, (((0,), (0,)), ((), ())),
                             preferred_element_type=F32)
        h = (direction(s_qk, qc[:, :2 * dh], vaug, ex[:, :3 * L], col_i <= row_i)
             + direction(s_qk, qc[:, 2 * dh:], vaug, ex[:, 3 * L:], col_i >= row_i))
        hn = _rms(h, ng)
        o_ref[rows, :] = (hn + sk * xc_ref[rows, :]) * _silu(z_ref[rows, :])
        return carry

    lax.fori_loop(0, nc, phase_c, 0, unroll=16)


def _mlstm(q, kt, v, gates, xc, z, norm_g, skip):
    b, seq, _ = q.shape
    nc = seq // MLSTM_CHUNK
    dh = MLSTM_HEAD_DIM
    head = lambda i, j: (i, 0, j)
    vec = lambda i, j: (0, j)
    blk = pl.BlockSpec((None, seq, dh), head)
    gate_rows = pltpu.VMEM((nc * 8, MLSTM_CHUNK), F32)
    gate_tiles = pltpu.VMEM((2, nc * 8, MLSTM_CHUNK), F32)
    return pl.pallas_call(
        _mlstm_kernel,
        grid=(b, MLSTM_HEADS),
        in_specs=[
            blk,
            pl.BlockSpec((None, nc, dh, MLSTM_CHUNK), lambda i, j: (i, 0, j, 0)),
            blk,
            pl.BlockSpec((None, None, 2, nc * 8, MLSTM_CHUNK), lambda i, j: (i, j, 0, 0, 0)),
            blk, blk,
            pl.BlockSpec((1, dh), vec),
            pl.BlockSpec((1, dh), vec),
        ],
        out_specs=blk,
        out_shape=jax.ShapeDtypeStruct((b, seq, D_MLSTM), F32),
        scratch_shapes=[
            gate_rows, gate_rows, gate_rows, gate_rows, gate_tiles, gate_tiles,
            pltpu.VMEM((nc, 2 * dh, 2 * dh), F32),
            pltpu.VMEM((nc, dh, 4 * dh), BF16),
            pltpu.VMEM((nc, 8, MLSTM_CHUNK), F32),
        ],
        compiler_params=_params(("parallel", "parallel")),
        name="mlstm",
    )(q, kt, v, gates, xc, z, norm_g, skip)


def _band_attn_kernel(q_ref, k_ref, v_ref, o_ref, lse_ref, *, half, dil):
    lsub = q_ref.shape[0]
    tq = ATTN_Q_TILE
    win = min(lsub, 2 * tq)
    first = lax.broadcasted_iota(jnp.int32, (1, LANES), 1) < ATTN_HEAD_DIM
    lane = lax.broadcasted_iota(jnp.int32, (tq, LANES), 1)
    rel = (lax.broadcasted_iota(jnp.int32, (2 * tq, win), 1)
           - lax.broadcasted_iota(jnp.int32, (2 * tq, win), 0) % tq)

    def tile(qs, ws, out_rows):
        keep = jnp.abs(rel + (ws - qs)) <= half
        for c in range(dil):
            lse_tile = jnp.zeros((tq, LANES), F32)
            for p in range(D_ATTN // LANES):
                lanes = slice(c * D_ATTN + p * LANES, c * D_ATTN + (p + 1) * LANES)
                q = q_ref[pl.ds(qs, tq), lanes]
                kw = k_ref[pl.ds(ws, win), lanes]
                vw = v_ref[pl.ds(ws, win), lanes]
                zero = jnp.zeros_like(q)
                q2 = jnp.concatenate([jnp.where(first, q, zero), jnp.where(first, zero, q)], axis=0)
                s = jnp.where(keep, _dot_nt(q2, kw), NEG_INF)
                m = jnp.max(s, axis=1, keepdims=True)
                e = jnp.exp(s - m)
                l = jnp.sum(e, axis=1, keepdims=True)
                o2 = _dot(e.astype(BF16), vw) / l
                o_ref[p, out_rows(c), :] = jnp.where(first, o2[:tq], o2[tq:])
                lse2 = m + jnp.log(l)
                lse_tile = jnp.where(lane == 2 * p, lse2[:tq],
                                     jnp.where(lane == 2 * p + 1, lse2[tq:], lse_tile))
            lse_ref[out_rows(c), :] = lse_tile

    if dil == 1:
        def body(t, carry):
            qs = pl.multiple_of(t * tq, tq)
            ws = pl.multiple_of(jnp.clip(qs - half, 0, lsub - win), half)
            tile(qs, ws, lambda c: pl.ds(qs, tq))
            return carry

        lax.fori_loop(0, lsub // tq, body, 0, unroll=8)
    else:
        for t in range(lsub // tq):
            qs = t * tq
            ws = min(max(qs - half, 0), lsub - win)
            tile(qs, ws, lambda c, qs=qs: pl.ds(qs * dil + c, tq, stride=dil))


def _band_attn(q, k, v, seq, win, dil):
    lsub = seq // dil
    b = q.shape[0] // lsub
    half = win // (2 * dil)
    pairs = D_ATTN // LANES
    assert lsub % ATTN_Q_TILE == 0 and half % 16 == 0 and ATTN_Q_TILE + 2 * half <= 2 * ATTN_Q_TILE
    blk = pl.BlockSpec((lsub, dil * D_ATTN), lambda i: (i, 0))
    return pl.pallas_call(
        functools.partial(_band_attn_kernel, half=half, dil=dil),
        grid=(b,),
        in_specs=[blk, blk, blk],
        out_specs=[pl.BlockSpec((None, pairs, seq, LANES), lambda i: (i, 0, 0, 0)),
                   pl.BlockSpec((seq, LANES), lambda i: (i, 0))],
        out_shape=[
            jax.ShapeDtypeStruct((b, pairs, seq, LANES), F32),
            jax.ShapeDtypeStruct((b * seq, LANES), F32),
        ],
        compiler_params=_params(("parallel",)),
        name=f"band_attn_d{dil}",
    )(q, k, v)


def _out_proj_kernel(ym_ref, o1_ref, o2_ref, o3_ref, l1_ref, l2_ref, l3_ref, sp_ref, x_ref, ag_ref,
                     w_ref, n2_ref, wr_ref, x2_ref, h2_ref, lg_ref):
    spread = sp_ref[...]
    wr = wr_ref[...]
    w_hi = wr.astype(BF16)
    w_lo = (wr - w_hi.astype(F32)).astype(BF16)
    wr3 = jnp.concatenate([w_hi, w_hi, w_lo], axis=0)

    def per_lane(w):
        hi = w.astype(BF16)
        lo = (w - hi.astype(F32)).astype(BF16)
        return _dot(jnp.concatenate([hi, lo], axis=1), spread)

    rows = x_ref.shape[0] // OUT_PROJ_SPLITS
    for part in range(OUT_PROJ_SPLITS):
        rs = slice(part * rows, (part + 1) * rows)
        lses = [r[rs, :] for r in (l1_ref, l2_ref, l3_ref)]
        top = jnp.maximum(jnp.maximum(lses[0], lses[1]), lses[2])
        wts = [jnp.exp(l - top) for l in lses]
        total = wts[0] + wts[1] + wts[2]

        def heads(o_ref):
            return jnp.concatenate([o_ref[p, rs, :] for p in range(o_ref.shape[0])], axis=1)

        ya = sum(per_lane(w / total) * heads(o) for w, o in zip(wts, (o1_ref, o2_ref, o3_ref)))
        ya = _rms(ya, ag_ref[...])
        mixed = jnp.concatenate([ym_ref[rs, :], ya], axis=1).astype(BF16)
        x2 = x_ref[rs, :] + _dot(mixed, w_ref[...])
        x2_ref[rs, :] = x2
        h2 = _rms(x2, n2_ref[...])
        hi = h2.astype(BF16)
        h2_ref[rs, :] = hi
        lo = (h2 - hi.astype(F32)).astype(BF16)
        lg_ref[rs, :] = _dot(jnp.concatenate([hi, lo, hi], axis=1), wr3)


def _out_proj(ym, branch_o, branch_lse, x2d, attn_g, w_bf, n2g, wr_pad):
    n = x2d.shape[0]
    spread = (jnp.arange(LANES)[:, None] == jnp.arange(D_ATTN)[None, :] // ATTN_HEAD_DIM)
    spread = jnp.tile(spread.astype(BF16), (2, 1))
    pairs, seq = branch_o[0].shape[1:3]
    tiles_per_seq = seq // ROW_TILE
    branch = pl.BlockSpec((None, pairs, ROW_TILE, LANES),
                          lambda i: (i // tiles_per_seq, 0, i % tiles_per_seq, 0))
    row = lambda i: (i, 0)
    fixed = lambda i: (0, 0)
    return pl.pallas_call(
        _out_proj_kernel,
        grid=(n // ROW_TILE,),
        in_specs=[
            pl.BlockSpec((ROW_TILE, D_MLSTM), row),
            branch, branch, branch,
            pl.BlockSpec((ROW_TILE, LANES), row),
            pl.BlockSpec((ROW_TILE, LANES), row),
            pl.BlockSpec((ROW_TILE, LANES), row),
            pl.BlockSpec((2 * LANES, D_ATTN), fixed),
            pl.BlockSpec((ROW_TILE, D_MODEL), row),
            pl.BlockSpec((1, D_ATTN), fixed),
            pl.BlockSpec((D_MODEL, D_MODEL), fixed),
            pl.BlockSpec((1, D_MODEL), fixed),
            pl.BlockSpec((D_MODEL, LANES), fixed),
        ],
        out_specs=[
            pl.BlockSpec((ROW_TILE, D_MODEL), row),
            pl.BlockSpec((ROW_TILE, D_MODEL), row),
            pl.BlockSpec((ROW_TILE, LANES), row),
        ],
        out_shape=[
            jax.ShapeDtypeStruct((n, D_MODEL), F32),
            jax.ShapeDtypeStruct((n, D_MODEL), BF16),
            jax.ShapeDtypeStruct((n, LANES), F32),
        ],
        compiler_params=_params(("parallel",)),
        name="out_proj",
    )(ym, *branch_o, *branch_lse, spread, x2d, attn_g, w_bf, n2g, wr_pad)


def _route_kernel(lg_ref, tri_ref, eye_ref, slot_ref, slot_t_ref, aff_ref, *, cap):
    for i in range(lg_ref.shape[0]):
        _route_one(lg_ref.at[i], tri_ref, eye_ref, slot_ref.at[i], slot_t_ref.at[i],
                   aff_ref.at[i], cap=cap)


def _route_one(lg_ref, tri_ref, eye_ref, slot_ref, slot_t_ref, aff_ref, *, cap):
    lg = lg_ref[...]
    valid = lax.broadcasted_iota(jnp.int32, (1, LANES), 1) < N_EXPERTS
    lg = jnp.where(valid, lg, NEG_INF)
    e = jnp.exp(lg - jnp.max(lg, axis=1, keepdims=True))
    aff = e / jnp.sum(e, axis=1, keepdims=True)
    aff_ref[...] = aff
    groups = LANES // N_EXPERTS
    rpg = aff.shape[0] // groups
    lane = lax.broadcasted_iota(jnp.int32, (1, LANES), 1)
    packed = aff[:rpg]
    for g in range(1, groups):
        packed = packed + pltpu.roll(aff[g * rpg:(g + 1) * rpg], g * N_EXPERTS, axis=1)

    def over_groups(x):
        shift = N_EXPERTS
        while shift < LANES:
            x = x + pltpu.roll(x, shift, axis=1)
            shift *= 2
        return x

    def enough(cand):
        part = jnp.sum(jnp.where(packed >= cand, 1.0, 0.0), axis=0, keepdims=True)
        return over_groups(part) >= cap

    def narrow(lo, hi, cands):
        new_lo, new_hi = lo, hi
        for cand in cands:
            ok = enough(cand)
            new_lo = jnp.maximum(new_lo, jnp.where(ok, cand, lo))
            new_hi = jnp.minimum(new_hi, jnp.where(ok, hi, cand))
        return new_lo, new_hi

    tiny = jnp.full((1, LANES), 2.0 ** -126, F32)
    normal = enough(tiny)
    p = tiny
    for span, count in ((16, 7), (1, 15)):
        p, _ = narrow(p, p, [p * (2.0 ** (span * j)) for j in range(1, count + 1)])
    lo = jnp.where(normal, p, 0.0)
    hi = jnp.where(normal, p * 2.0, tiny)
    width = jnp.where(normal, p, 0.0)
    for bits in THRESHOLD_RADIX_BITS:
        width = width * (0.5 ** bits)
        lo, hi = narrow(lo, hi, [lo + j * width for j in range(1, 2 ** bits)])
    gt = jnp.where(packed >= hi, 1.0, 0.0)
    eq = jnp.where(packed >= lo, 1.0, 0.0) - gt
    need = cap - over_groups(jnp.sum(gt, axis=0, keepdims=True))
    tri = tri_ref[...]

    def count_before(x):
        per_group = jnp.sum(x, axis=0, keepdims=True)
        upto = per_group
        shift = N_EXPERTS
        while shift < LANES:
            upto = upto + jnp.where(lane >= shift, pltpu.roll(upto, shift, axis=1), 0.0)
            shift *= 2
        return _dot(tri, x.astype(BF16)) + (upto - per_group)

    sel = gt + eq * jnp.where(count_before(eq) < need, 1.0, 0.0)
    pos = count_before(sel)
    slot_packed = jnp.where(sel > 0.0, pos, -1.0)
    slot = jnp.concatenate(
        [jnp.where(valid, slot_packed if g == 0 else
                   pltpu.roll(slot_packed, LANES - g * N_EXPERTS, axis=1), -1.0)
         for g in range(groups)], axis=0)
    slot_ref[...] = slot
    slot_t_ref[...] = _dot_nt(eye_ref[...], slot.astype(BF16))


def _route(logits, cap):
    b, seq, _ = logits.shape
    assert LANES % N_EXPERTS == 0 and seq % (LANES // N_EXPERTS) == 0
    rpg = seq // (LANES // N_EXPERTS)
    tri = (jnp.arange(rpg)[None, :] < jnp.arange(rpg)[:, None]).astype(BF16)
    eye = jnp.eye(LANES, dtype=BF16)
    per_b = lambda i: (i, 0, 0)
    fixed = lambda i: (0, 0)
    nb = ROUTE_BATCHES if b % ROUTE_BATCHES == 0 else 1
    return pl.pallas_call(
        functools.partial(_route_kernel, cap=cap),
        grid=(b // nb,),
        in_specs=[
            pl.BlockSpec((nb, seq, LANES), per_b),
            pl.BlockSpec((rpg, rpg), fixed),
            pl.BlockSpec((LANES, LANES), fixed),
        ],
        out_specs=[
            pl.BlockSpec((nb, seq, LANES), per_b),
            pl.BlockSpec((nb, LANES, seq), per_b),
            pl.BlockSpec((nb, seq, LANES), per_b),
        ],
        out_shape=[
            jax.ShapeDtypeStruct((b, seq, LANES), F32),
            jax.ShapeDtypeStruct((b, LANES, seq), F32),
            jax.ShapeDtypeStruct((b, seq, LANES), F32),
        ],
        compiler_params=_params(("parallel",)),
        name="route",
    )(logits, tri, eye)


def _moe_gather_kernel(slot_ref, h_ref, xs_ref):
    srow = slot_ref[...]
    cap, seq = xs_ref.shape[0], srow.shape[1]
    ci = lax.broadcasted_iota(jnp.int32, (cap, seq), 0).astype(F32)
    onehot = jnp.where(srow == ci, 1.0, 0.0).astype(BF16)
    xs_ref[...] = _dot(onehot, h_ref[...]).astype(BF16)


def _moe_gather(slot_t, h2, cap):
    b, seq, _ = h2.shape
    return pl.pallas_call(
        _moe_gather_kernel,
        grid=(b, N_EXPERTS),
        in_specs=[
            pl.BlockSpec((None, None, 1, seq), lambda i, e: (i, e, 0, 0)),
            pl.BlockSpec((None, seq, D_MODEL), lambda i, e: (i, 0, 0)),
        ],
        out_specs=pl.BlockSpec((None, None, cap, D_MODEL), lambda i, e: (i, e, 0, 0)),
        out_shape=jax.ShapeDtypeStruct((b, N_EXPERTS, cap, D_MODEL), BF16),
        compiler_params=_params(("parallel", "parallel")),
        name="moe_gather",
    )(slot_t, h2)


def _moe_ffn_kernel(xs_ref, w1_ref, w3_ref, w2_ref, y_ref, act_ref, w1b_ref, w3b_ref, w2b_ref):
    s = pl.program_id(1)
    nb, cap, _ = xs_ref.shape
    nf = act_ref.shape[0]
    per = FFN_ROW_TILE // cap
    row_tiles = nb // per

    @pl.when(s < nf)
    def _():
        w1b_ref[...] = w1_ref[...].astype(BF16)
        w3b_ref[...] = w3_ref[...].astype(BF16)
        for r in range(row_tiles):
            x = xs_ref[r * per:(r + 1) * per].reshape(FFN_ROW_TILE, D_MODEL)
            up = _dot(x, w1b_ref[...])
            gt = _dot(x, w3b_ref[...])
            act_ref[s, r * FFN_ROW_TILE:(r + 1) * FFN_ROW_TILE, :] = (_silu(up) * gt).astype(BF16)

    @pl.when(s >= nf)
    def _():
        w2b_ref[...] = w2_ref[...].astype(BF16)
        for r in range(row_tiles):
            rows = slice(r * FFN_ROW_TILE, (r + 1) * FFN_ROW_TILE)
            act = jnp.concatenate([act_ref[f, rows, :] for f in range(nf)], axis=1)
            y = _dot(act, w2b_ref[...])
            y_ref[r * per:(r + 1) * per] = y.astype(BF16).reshape(per, cap, y.shape[1])


def _moe_ffn(xs, w1, w3, w2):
    b, ne, cap, _ = xs.shape
    nf = D_EXPERT // FFN_F_TILE
    nn = D_MODEL // FFN_N_TILE
    hidden = lambda e, s: (e, 0, jnp.minimum(s, nf - 1))
    out_col = lambda e, s: jnp.maximum(s - nf, 0)
    return pl.pallas_call(
        _moe_ffn_kernel,
        grid=(ne, nf + nn),
        in_specs=[
            pl.BlockSpec((b, None, cap, D_MODEL), lambda e, s: (0, e, 0, 0)),
            pl.BlockSpec((None, D_MODEL, FFN_F_TILE), hidden),
            pl.BlockSpec((None, D_MODEL, FFN_F_TILE), hidden),
            pl.BlockSpec((None, D_EXPERT, FFN_N_TILE), lambda e, s: (e, 0, out_col(e, s))),
        ],
        out_specs=pl.BlockSpec((b, None, cap, FFN_N_TILE), lambda e, s: (0, e, 0, out_col(e, s))),
        out_shape=jax.ShapeDtypeStruct(xs.shape, BF16),
        scratch_shapes=[
            pltpu.VMEM((nf, b * cap, FFN_F_TILE), BF16),
            pltpu.VMEM((D_MODEL, FFN_F_TILE), BF16),
            pltpu.VMEM((D_MODEL, FFN_F_TILE), BF16),
            pltpu.VMEM((D_EXPERT, FFN_N_TILE), BF16),
        ],
        compiler_params=_params(("parallel", "arbitrary")),
        name="moe_ffn",
    )(xs, w1, w3, w2)


def _moe_scatter_kernel(slot_ref, aff_ref, y_ref, x2_ref, g_ref, o_ref):
    slot = slot_ref[...]
    aff = aff_ref[...]
    rows, cap = slot.shape[0], y_ref.shape[1]
    ci = lax.broadcasted_iota(jnp.int32, (rows, cap), 1).astype(F32)
    acc = x2_ref[...]
    for e in range(N_EXPERTS):
        onehot = jnp.where(slot[:, e:e + 1] == ci, 1.0, 0.0).astype(BF16)
        acc = acc + aff[:, e:e + 1] * _dot(onehot, y_ref[e])
    o_ref[...] = _rms(acc, g_ref[...])


def _moe_scatter(slot, aff, y, x2, norm_g):
    b, seq, _ = x2.shape
    cap = y.shape[2]
    tile = lambda i, r: (i, r, 0)
    return pl.pallas_call(
        _moe_scatter_kernel,
        grid=(b, seq // ROW_TILE),
        in_specs=[
            pl.BlockSpec((None, ROW_TILE, LANES), tile),
            pl.BlockSpec((None, ROW_TILE, LANES), tile),
            pl.BlockSpec((None, N_EXPERTS, cap, D_MODEL), lambda i, r: (i, 0, 0, 0)),
            pl.BlockSpec((None, ROW_TILE, D_MODEL), tile),
            pl.BlockSpec((1, D_MODEL), lambda i, r: (0, 0)),
        ],
        out_specs=pl.BlockSpec((None, ROW_TILE, D_MODEL), tile),
        out_shape=jax.ShapeDtypeStruct((b, seq, D_MODEL), F32),
        compiler_params=_params(("parallel", "parallel")),
        name="moe_scatter",
    )(slot, aff, y, x2, norm_g)


def kernel(x, norm1_g, w_in, conv_w, conv_b, wq_m, wk_m, wv_m, w_if_fwd, b_if_fwd,
           w_if_bwd, b_if_bwd, mlstm_norm_g, mlstm_skip, attn_norm_g, w_out, norm2_g,
           w_router, w1, w3, w2, norm_f_g):
    b, seq, _ = x.shape
    assert w_in.shape[0] == 1, "single-layer problem"
    assert seq % ROW_TILE == 0 and seq % MLSTM_CHUNK == 0 and seq % ATTN_Q_TILE == 0
    cap = EC_CAPACITY * seq // N_EXPERTS
    assert FFN_ROW_TILE % cap == 0 and (b * cap) % FFN_ROW_TILE == 0
    l = 0
    x2d = x.reshape(b * seq, D_MODEL)
    xm, z, *qkv_views = _in_proj(x2d, norm1_g[l][None, :], w_in[l].astype(BF16), seq)
    shp = lambda t: t.reshape(b, seq, t.shape[-1])
    wif_rows, bif_rows = _gate_rows(w_if_fwd[l], b_if_fwd[l], w_if_bwd[l], b_if_bwd[l])
    xc, qm, ktm, vm, gates = _mlstm_pre(
        shp(xm), conv_w[l], conv_b[l][None, :],
        _block_diag(wq_m[l]).astype(BF16), _block_diag(wk_m[l]).T.astype(BF16),
        _block_diag(wv_m[l]).astype(BF16), wif_rows.astype(BF16), bif_rows)
    ym = _mlstm(qm, ktm, vm, gates, xc, shp(z), mlstm_norm_g[l][None, :],
                mlstm_skip[l][None, :])
    branches = [_band_attn(*qkv_views[3 * d:3 * d + 3], seq, win, dil)
                for d, (win, dil) in enumerate(DILATED_PATTERNS)]
    wr_pad = jnp.pad(w_router[l], ((0, 0), (0, LANES - N_EXPERTS)))
    x2, h2, logits = _out_proj(
        ym.reshape(b * seq, D_MLSTM), [o for o, _ in branches], [s for _, s in branches], x2d,
        attn_norm_g[l][None, :], w_out[l].astype(BF16), norm2_g[l][None, :], wr_pad)
    slot, slot_t, aff = _route(logits.reshape(b, seq, LANES), cap)
    xs = _moe_gather(slot_t.reshape(b, LANES, 1, seq), h2.reshape(b, seq, D_MODEL), cap)
    y = _moe_ffn(xs, w1[l], w3[l], w2[l])
    return _moe_scatter(slot, aff, y, x2.reshape(b, seq, D_MODEL), norm_f_g[None, :])
```

```python
import functools

import jax
import jax.numpy as jnp
from jax import lax
from jax.experimental import pallas as pl
from jax.experimental.pallas import tpu as pltpu

F32 = jnp.float32
BF16 = jnp.bfloat16

D_MODEL = 1024
D_MLSTM = 512
D_ATTN = 512
D_IN_PROJ = 2 * D_MLSTM + 3 * D_ATTN
MLSTM_HEADS = 4
MLSTM_HEAD_DIM = 128
MLSTM_QKV_BLOCK = 4
MLSTM_CONV = 5
ATTN_HEAD_DIM = 64
ROPE_DIM = 16
ROPE_THETA = 500000.0
DILATED_PATTERNS = ((128, 1), (512, 4), (2048, 16))
N_EXPERTS = 16
EC_CAPACITY = 2
D_EXPERT = 2816
NORM_EPS = 1e-6
NEG_INF = -1e30

LANES = 128
MLSTM_CHUNK = 128
ROW_TILE = 512
OUT_PROJ_SPLITS = 2
ROUTE_BATCHES = 2
ATTN_Q_TILE = 128
FFN_F_TILE = 256
FFN_N_TILE = 256
FFN_ROW_TILE = 512
THRESHOLD_RADIX_BITS = (4, 4, 4, 4, 4, 3, 4, 4, 4, 4)
V7X_VMEM_BYTES = 64 * 1024 * 1024
VMEM_LIMIT = V7X_VMEM_BYTES * 7 // 8


def _params(sem):
    return pltpu.CompilerParams(dimension_semantics=sem, vmem_limit_bytes=VMEM_LIMIT)


def _rms(x, g):
    return x * lax.rsqrt(jnp.mean(x * x, axis=-1, keepdims=True) + NORM_EPS) * g


def _silu(x):
    return x * (1.0 / (1.0 + jnp.exp(-x)))


def _dot(a, b):
    return jnp.dot(a, b, preferred_element_type=F32)


def _dot_nt(a, b):
    return lax.dot_general(a, b, (((1,), (1,)), ((), ())), preferred_element_type=F32)


def _in_proj_kernel(x_ref, g_ref, w_ref, cos_ref, sa_ref, sb_ref, xm_ref, z_ref, *rest):
    qkv_refs, scr_ref = rest[:-1], rest[-1]
    h = _rms(x_ref[...], g_ref[...])
    p = _dot(h.astype(BF16), w_ref[...])
    xm_ref[...] = p[:, :D_MLSTM]
    z_ref[...] = p[:, D_MLSTM:2 * D_MLSTM]
    cos, sa, sb = cos_ref[...], sa_ref[...], sb_ref[...]
    half = ROPE_DIM // 2

    def rope(t):
        outs = []
        for j in range(D_ATTN // LANES):
            tj = t[:, j * LANES:(j + 1) * LANES]
            up = pltpu.roll(tj, LANES - half, axis=1)
            dn = pltpu.roll(tj, half, axis=1)
            outs.append(tj * cos + up * sa + dn * sb)
        return jnp.concatenate(outs, axis=1)

    o = 2 * D_MLSTM
    qkv = (rope(p[:, o:o + D_ATTN]) * (ATTN_HEAD_DIM ** -0.5),
           rope(p[:, o + D_ATTN:o + 2 * D_ATTN]),
           p[:, o + 2 * D_ATTN:])
    rows = p.shape[0]
    groups = D_ATTN // LANES
    for a, val in enumerate(qkv):
        for j in range(groups):
            scr_ref[j] = val[:, j * LANES:(j + 1) * LANES]
        for d, (_, dil) in enumerate(DILATED_PATTERNS):
            ref = qkv_refs[3 * d + a]
            if dil == 1:
                ref[...] = val.astype(BF16)
                continue
            for r in range(dil):
                for j in range(groups):
                    piece = scr_ref[j, pl.ds(r, rows // dil, stride=dil), :]
                    ref[:, r * D_ATTN + j * LANES:r * D_ATTN + (j + 1) * LANES] = piece.astype(BF16)


def _rope_tables(seq):
    half = ROPE_DIM // 2
    inv_freq = ROPE_THETA ** (-2.0 * jnp.arange(half, dtype=F32) / ROPE_DIM)
    ang = jnp.arange(seq).astype(F32)[:, None] * inv_freq[None, :]
    cos, sin = jnp.cos(ang), jnp.sin(ang)
    pad = jnp.zeros((seq, ATTN_HEAD_DIM - ROPE_DIM), F32)
    cos_h = jnp.concatenate([cos, cos, pad + 1.0], axis=1)
    sa_h = jnp.concatenate([-sin, jnp.zeros_like(sin), pad], axis=1)
    sb_h = jnp.concatenate([jnp.zeros_like(sin), sin, pad], axis=1)
    rep = LANES // ATTN_HEAD_DIM
    return tuple(jnp.tile(t, (1, rep)) for t in (cos_h, sa_h, sb_h))


def _in_proj(x2d, g, w_bf, seq):
    n = x2d.shape[0]
    tiles_per_seq = seq // ROW_TILE
    cos, sa, sb = _rope_tables(seq)
    row = lambda i: (i, 0)
    fixed = lambda i: (0, 0)
    pos = lambda i: (i % tiles_per_seq, 0)
    return pl.pallas_call(
        _in_proj_kernel,
        grid=(n // ROW_TILE,),
        in_specs=[
            pl.BlockSpec((ROW_TILE, D_MODEL), row),
            pl.BlockSpec((1, D_MODEL), fixed),
            pl.BlockSpec((D_MODEL, D_IN_PROJ), fixed),
            pl.BlockSpec((ROW_TILE, LANES), pos),
            pl.BlockSpec((ROW_TILE, LANES), pos),
            pl.BlockSpec((ROW_TILE, LANES), pos),
        ],
        out_specs=[
            pl.BlockSpec((ROW_TILE, D_MLSTM), row),
            pl.BlockSpec((ROW_TILE, D_MLSTM), row),
        ] + [pl.BlockSpec((ROW_TILE // dil, dil * D_ATTN), row)
             for _, dil in DILATED_PATTERNS for _ in range(3)],
        out_shape=[
            jax.ShapeDtypeStruct((n, D_MLSTM), F32),
            jax.ShapeDtypeStruct((n, D_MLSTM), F32),
        ] + [jax.ShapeDtypeStruct((n // dil, dil * D_ATTN), BF16)
             for _, dil in DILATED_PATTERNS for _ in range(3)],
        scratch_shapes=[pltpu.VMEM((D_ATTN // LANES, ROW_TILE, LANES), F32)],
        compiler_params=_params(("parallel",)),
        name="in_proj",
    )(x2d, g, w_bf, cos, sa, sb)


def _mlstm_pre_kernel(xm_ref, cw_ref, cb_ref, wq_ref, wkt_ref, wv_ref, wif_ref, bif_ref,
                      xc_ref, q_ref, kt_ref, v_ref, g_ref):
    x = xm_ref[...]
    seq = x.shape[0]
    t = lax.broadcasted_iota(jnp.int32, x.shape, 0)
    acc = jnp.zeros_like(x) + cb_ref[...]
    for j in range(MLSTM_CONV):
        d = j - MLSTM_CONV // 2
        if d == 0:
            tap = x
        else:
            tap = pltpu.roll(x, (-d) % seq, axis=0)
            tap = jnp.where((t + d >= 0) & (t + d < seq), tap, 0.0)
        acc = acc + tap * cw_ref[j:j + 1, :]
    xc = _silu(acc)
    xc_ref[...] = xc
    xcb = xc.astype(BF16)
    q = _dot(xcb, wq_ref[...]).astype(BF16)
    kt = (_dot_nt(wkt_ref[...], xcb) * (MLSTM_HEAD_DIM ** -0.5)).astype(BF16)
    v = _dot(x.astype(BF16), wv_ref[...]).astype(BF16)
    q_ref[...] = q
    v_ref[...] = v
    L = MLSTM_CHUNK
    for c in range(seq // L):
        kt_ref[c] = kt[:, c * L:(c + 1) * L]
    wif = wif_ref[...]
    g = (_dot_nt(wif[:, :D_MLSTM], q) + _dot(wif[:, D_MLSTM:2 * D_MLSTM], kt)
         + _dot_nt(wif[:, 2 * D_MLSTM:], v) + bif_ref[...])
    for kind in range(2):
        for hd in range(MLSTM_HEADS):
            r0 = (kind * MLSTM_HEADS + hd) * 8
            for c in range(seq // L):
                g_ref[hd, kind, c * 8:(c + 1) * 8, :] = g[r0:r0 + 8, c * L:(c + 1) * L]


def _block_diag(w):
    nblk = w.shape[0]
    n = nblk * MLSTM_QKV_BLOCK
    tiled = jnp.tile(w.reshape(n, MLSTM_QKV_BLOCK), (1, nblk))
    blk = jnp.arange(n) // MLSTM_QKV_BLOCK
    return jnp.where(blk[:, None] == blk[None, :], tiled, 0.0)


def _gate_rows(w_f, b_f, w_b, b_b):
    h = MLSTM_HEADS

    def rows(f, b):
        pair = jnp.stack([f, b], axis=-1)
        pair = jnp.concatenate([pair[..., h:, :], pair[..., :h, :]], axis=-2)
        pair = jnp.pad(pair, [(0, 0)] * (pair.ndim - 1) + [(0, 6)])
        return pair.reshape(*pair.shape[:-2], 2 * h * 8)

    return rows(w_f, w_b).T, rows(b_f, b_b)[:, None]


def _mlstm_pre(xm, conv_w, conv_b, wq, wkt, wv, wif_rows, bif_rows):
    b, seq, _ = xm.shape
    nrow = wif_rows.shape[0]
    nc = seq // MLSTM_CHUNK
    per_b = lambda i: (i, 0, 0)
    fixed = lambda i: (0, 0)
    return pl.pallas_call(
        _mlstm_pre_kernel,
        grid=(b,),
        in_specs=[
            pl.BlockSpec((None, seq, D_MLSTM), per_b),
            pl.BlockSpec((MLSTM_CONV, D_MLSTM), fixed),
            pl.BlockSpec((1, D_MLSTM), fixed),
            pl.BlockSpec((D_MLSTM, D_MLSTM), fixed),
            pl.BlockSpec((D_MLSTM, D_MLSTM), fixed),
            pl.BlockSpec((D_MLSTM, D_MLSTM), fixed),
            pl.BlockSpec((nrow, 3 * D_MLSTM), fixed),
            pl.BlockSpec((nrow, 1), fixed),
        ],
        out_specs=[
            pl.BlockSpec((None, seq, D_MLSTM), per_b),
            pl.BlockSpec((None, seq, D_MLSTM), per_b),
            pl.BlockSpec((None, nc, D_MLSTM, MLSTM_CHUNK), lambda i: (i, 0, 0, 0)),
            pl.BlockSpec((None, seq, D_MLSTM), per_b),
            pl.BlockSpec((None, MLSTM_HEADS, 2, nc * 8, MLSTM_CHUNK), lambda i: (i, 0, 0, 0, 0)),
        ],
        out_shape=[
            jax.ShapeDtypeStruct((b, seq, D_MLSTM), F32),
            jax.ShapeDtypeStruct((b, seq, D_MLSTM), BF16),
            jax.ShapeDtypeStruct((b, nc, D_MLSTM, MLSTM_CHUNK), BF16),
            jax.ShapeDtypeStruct((b, seq, D_MLSTM), BF16),
            jax.ShapeDtypeStruct((b, MLSTM_HEADS, 2, nc * 8, MLSTM_CHUNK), F32),
        ],
        compiler_params=_params(("parallel",)),
        name="mlstm_pre",
    )(xm, conv_w, conv_b, wq, wkt, wv, wif_rows, bif_rows)


def _log_sigmoid(x):
    return jnp.minimum(x, 0.0) - jnp.log1p(jnp.exp(-jnp.abs(x)))


def _split3(x):
    hi = x.astype(BF16).astype(F32)
    mid = (x - hi).astype(BF16).astype(F32)
    lo = (x - hi - mid).astype(BF16).astype(F32)
    return hi, mid, lo


def _mlstm_kernel(q_ref, kt_ref, v_ref, g_ref, xc_ref, z_ref, ng_ref, sk_ref, o_ref,
                  w_ref, ml_ref, tot_ref, pm_ref, tb_ref, tr_ref, dc_ref, cs_ref, ms_ref):
    L = MLSTM_CHUNK
    dh = MLSTM_HEAD_DIM
    nc = kt_ref.shape[0]
    rows_all = g_ref.shape[1]
    lane = lax.broadcasted_iota(jnp.int32, (rows_all, L), 1)
    sub = lax.broadcasted_iota(jnp.int32, (rows_all, L), 0) % 8
    fwd_row = sub == 0
    row_i = lax.broadcasted_iota(jnp.int32, (L, L), 0)
    col_i = lax.broadcasted_iota(jnp.int32, (L, L), 1)
    ones_col = jnp.ones((L, dh), BF16)

    lf = _log_sigmoid(g_ref[0])
    pre, suf = lf, lf
    d = 1
    while d < L:
        pre = pre + jnp.where(lane >= d, pltpu.roll(pre, d, axis=1), 0.0)
        suf = suf + jnp.where(lane < L - d, pltpu.roll(suf, L - d, axis=1), 0.0)
        d *= 2
    cum = jnp.where(fwd_row, pre, suf)
    tot = jnp.where(fwd_row, cum[:, L - 1:L], cum[:, 0:1])
    a = tot - cum + g_ref[1]
    ml = jnp.max(a, axis=1, keepdims=True)
    w_ref[...] = jnp.exp(a - ml)
    ml_ref[...] = jnp.broadcast_to(ml, (rows_all, L))
    tot_ref[...] = tot
    r = g_ref[1] - cum
    pmax, smax = r, r
    d = 1
    while d < L:
        pmax = jnp.maximum(pmax, jnp.where(lane >= d, pltpu.roll(pmax, d, axis=1), NEG_INF))
        smax = jnp.maximum(smax, jnp.where(lane < L - d, pltpu.roll(smax, L - d, axis=1), NEG_INF))
        d *= 2
    pm_ref[...] = jnp.where(fwd_row, pmax, smax)

    def tile_bcast(x, src):
        y = jnp.where(sub == src, x, 0.0)
        if src:
            y = pltpu.roll(y, rows_all - src, axis=0)
        for s in (1, 2, 4):
            y = y + pltpu.roll(y, s, axis=0)
        return y

    for dr in range(2):
        b_hi, b_mid, b_lo = _split3(tile_bcast(cum, dr))
        r_hi, r_mid, r_lo = _split3(tile_bcast(r, dr))
        tb_ref[dr] = jnp.where(sub == 1, b_hi, jnp.where(sub == 2, b_mid, jnp.where(
            sub == 3, b_lo, jnp.where((sub >= 4) & (sub <= 6), 1.0, 0.0))))
        tr_ref[dr] = jnp.where(sub == 0, 1.0, jnp.where(sub == 4, r_hi, jnp.where(
            sub == 5, r_mid, jnp.where(sub == 6, r_lo, 0.0))))

    def chunk_rows(ref, c):
        return ref[pl.ds(pl.multiple_of(c * 8, 8), 8), :]

    def v_aug(c):
        return jnp.concatenate([v_ref[pl.ds(c * L, L), :], ones_col], axis=1)

    def phase_a(c, carry):
        kt = kt_ref[c].astype(F32)
        w = chunk_rows(w_ref, c)
        kw = jnp.concatenate([kt * w[0:1], kt * w[1:2]], axis=0)
        dc_ref[c] = _dot(kw.astype(BF16), v_aug(c))
        return carry

    lax.fori_loop(0, nc, phase_a, 0, unroll=8)

    ms_ref[...] = jnp.zeros(ms_ref.shape, F32)
    def scan_dir(direction_row, reverse):
        off_rows = direction_row * dh
        off_cols = direction_row * 2 * dh

        def body(i, carry):
            c = (nc - 1 - i) if reverse else i
            state, m = carry
            cs_ref[c, :, off_cols:off_cols + 2 * dh] = state.astype(BF16)
            ms_ref[c, direction_row:direction_row + 1, :] = m
            ml_c = chunk_rows(ml_ref, c)[direction_row:direction_row + 1]
            g_c = chunk_rows(tot_ref, c)[direction_row:direction_row + 1]
            m_new = jnp.maximum(g_c + m, ml_c)
            alpha = jnp.exp(g_c + m - m_new)
            beta = jnp.exp(ml_c - m_new)
            alpha2 = jnp.concatenate([alpha, alpha], axis=1)
            beta2 = jnp.concatenate([beta, beta], axis=1)
            state = alpha2 * state + beta2 * dc_ref[c, off_rows:off_rows + dh, :]
            return state, m_new

        init = (jnp.zeros((dh, 2 * dh), F32), jnp.zeros((1, L), F32))
        lax.fori_loop(0, nc, body, init, unroll=4)

    scan_dir(0, False)
    scan_dir(1, True)

    ng = ng_ref[...]
    sk = sk_ref[...]

    sub8 = lax.broadcasted_iota(jnp.int32, (8, L), 0)
    floor_rows = jnp.where(sub8 == 0, 1.0, jnp.where(sub8 <= 3, -1.0, 0.0))
    no_rows = jnp.zeros((8, 3 * L), F32)

    def direction(s_qk, qc, vaug, ex, keep):
        w = jnp.exp(jnp.where(keep, ex[:, :L], NEG_INF)) * s_qk
        scale = jnp.exp(ex[:, L:2 * L])
        intra = _dot(w.astype(BF16), vaug)
        tot_c = intra + jnp.concatenate([scale, scale], axis=1) * qc
        return tot_c[:, :dh] / jnp.maximum(jnp.abs(tot_c[:, dh:]), jnp.exp(ex[:, 2 * L:]))

    def phase_c(c, carry):
        rows = pl.ds(c * L, L)
        q = q_ref[rows, :]
        s_qk = _dot(q, kt_ref[c])
        qc = _dot(q, cs_ref[c])
        vaug = v_aug(c)
        ms = ms_ref[c]
        e = -jnp.maximum(ms, chunk_rows(pm_ref, c))
        e = (e - jnp.abs(e) * (2.0 ** -7)).astype(BF16).astype(F32)
        lhs_tiles, rhs_tiles = [], []
        for dr in range(2):
            e_rows = jnp.broadcast_to(e[dr:dr + 1], (8, L))
            lhs_tiles.append(jnp.where(sub8 == 0, e_rows, chunk_rows(tb_ref.at[dr], c)))
            m_hi, m_mid, m_lo = _split3(jnp.broadcast_to(ms[dr:dr + 1], (8, L)))
            scale_rows = jnp.where(sub8 == 0, 1.0, jnp.where(sub8 == 4, m_hi, jnp.where(
                sub8 == 5, m_mid, jnp.where(sub8 == 6, m_lo, 0.0))))
            rhs_tiles.append(jnp.concatenate(
                [chunk_rows(tr_ref.at[dr], c), scale_rows, floor_rows], axis=1))
        lhs = jnp.concatenate(lhs_tiles, axis=0).astype(BF16)
        rhs = jnp.concatenate([jnp.concatenate([rhs_tiles[0], no_rows], axis=1),
                               jnp.concatenate([no_rows, rhs_tiles[1]], axis=1)],
                              axis=0).astype(BF16)
        ex = lax.dot_general(lhs, rhs, (((0,), (0,)), ((), ())),
                             preferred_element_type=F32)
        h = (direction(s_qk, qc[:, :2 * dh], vaug, ex[:, :3 * L], col_i <= row_i)
             + direction(s_qk, qc[:, 2 * dh:], vaug, ex[:, 3 * L:], col_i >= row_i))
        hn = _rms(h, ng)
        o_ref[rows, :] = (hn + sk * xc_ref[rows, :]) * _silu(z_ref[rows, :])
        return carry

    lax.fori_loop(0, nc, phase_c, 0, unroll=16)


def _mlstm(q, kt, v, gates, xc, z, norm_g, skip):
    b, seq, _ = q.shape
    nc = seq // MLSTM_CHUNK
    dh = MLSTM_HEAD_DIM
    head = lambda i, j: (i, 0, j)
    vec = lambda i, j: (0, j)
    blk = pl.BlockSpec((None, seq, dh), head)
    gate_rows = pltpu.VMEM((nc * 8, MLSTM_CHUNK), F32)
    gate_tiles = pltpu.VMEM((2, nc * 8, MLSTM_CHUNK), F32)
    return pl.pallas_call(
        _mlstm_kernel,
        grid=(b, MLSTM_HEADS),
        in_specs=[
            blk,
            pl.BlockSpec((None, nc, dh, MLSTM_CHUNK), lambda i, j: (i, 0, j, 0)),
            blk,
            pl.BlockSpec((None, None, 2, nc * 8, MLSTM_CHUNK), lambda i, j: (i, j, 0, 0, 0)),
            blk, blk,
            pl.BlockSpec((1, dh), vec),
            pl.BlockSpec((1, dh), vec),
        ],
        out_specs=blk,
        out_shape=jax.ShapeDtypeStruct((b, seq, D_MLSTM), F32),
        scratch_shapes=[
            gate_rows, gate_rows, gate_rows, gate_rows, gate_tiles, gate_tiles,
            pltpu.VMEM((nc, 2 * dh, 2 * dh), F32),
            pltpu.VMEM((nc, dh, 4 * dh), BF16),
            pltpu.VMEM((nc, 8, MLSTM_CHUNK), F32),
        ],
        compiler_params=_params(("parallel", "parallel")),
        name="mlstm",
    )(q, kt, v, gates, xc, z, norm_g, skip)


def _band_attn_kernel(q_ref, k_ref, v_ref, o_ref, lse_ref, *, half, dil):
    lsub = q_ref.shape[0]
    tq = ATTN_Q_TILE
    win = min(lsub, 2 * tq)
    first = lax.broadcasted_iota(jnp.int32, (1, LANES), 1) < ATTN_HEAD_DIM
    lane = lax.broadcasted_iota(jnp.int32, (tq, LANES), 1)
    rel = (lax.broadcasted_iota(jnp.int32, (2 * tq, win), 1)
           - lax.broadcasted_iota(jnp.int32, (2 * tq, win), 0) % tq)

    def tile(qs, ws, out_rows):
        keep = jnp.abs(rel + (ws - qs)) <= half
        for c in range(dil):
            lse_tile = jnp.zeros((tq, LANES), F32)
            for p in range(D_ATTN // LANES):
                lanes = slice(c * D_ATTN + p * LANES, c * D_ATTN + (p + 1) * LANES)
                q = q_ref[pl.ds(qs, tq), lanes]
                kw = k_ref[pl.ds(ws, win), lanes]
                vw = v_ref[pl.ds(ws, win), lanes]
                zero = jnp.zeros_like(q)
                q2 = jnp.concatenate([jnp.where(first, q, zero), jnp.where(first, zero, q)], axis=0)
                s = jnp.where(keep, _dot_nt(q2, kw), NEG_INF)
                m = jnp.max(s, axis=1, keepdims=True)
                e = jnp.exp(s - m)
                l = jnp.sum(e, axis=1, keepdims=True)
                o2 = _dot(e.astype(BF16), vw) / l
                o_ref[p, out_rows(c), :] = jnp.where(first, o2[:tq], o2[tq:])
                lse2 = m + jnp.log(l)
                lse_tile = jnp.where(lane == 2 * p, lse2[:tq],
                                     jnp.where(lane == 2 * p + 1, lse2[tq:], lse_tile))
            lse_ref[out_rows(c), :] = lse_tile

    if dil == 1:
        def body(t, carry):
            qs = pl.multiple_of(t * tq, tq)
            ws = pl.multiple_of(jnp.clip(qs - half, 0, lsub - win), half)
            tile(qs, ws, lambda c: pl.ds(qs, tq))
            return carry

        lax.fori_loop(0, lsub // tq, body, 0, unroll=8)
    else:
        for t in range(lsub // tq):
            qs = t * tq
            ws = min(max(qs - half, 0), lsub - win)
            tile(qs, ws, lambda c, qs=qs: pl.ds(qs * dil + c, tq, stride=dil))


def _band_attn(q, k, v, seq, win, dil):
    lsub = seq // dil
    b = q.shape[0] // lsub
    half = win // (2 * dil)
    pairs = D_ATTN // LANES
    assert lsub % ATTN_Q_TILE == 0 and half % 16 == 0 and ATTN_Q_TILE + 2 * half <= 2 * ATTN_Q_TILE
    blk = pl.BlockSpec((lsub, dil * D_ATTN), lambda i: (i, 0))
    return pl.pallas_call(
        functools.partial(_band_attn_kernel, half=half, dil=dil),
        grid=(b,),
        in_specs=[blk, blk, blk],
        out_specs=[pl.BlockSpec((None, pairs, seq, LANES), lambda i: (i, 0, 0, 0)),
                   pl.BlockSpec((seq, LANES), lambda i: (i, 0))],
        out_shape=[
            jax.ShapeDtypeStruct((b, pairs, seq, LANES), F32),
            jax.ShapeDtypeStruct((b * seq, LANES), F32),
        ],
        compiler_params=_params(("parallel",)),
        name=f"band_attn_d{dil}",
    )(q, k, v)


def _out_proj_kernel(ym_ref, o1_ref, o2_ref, o3_ref, l1_ref, l2_ref, l3_ref, sp_ref, x_ref, ag_ref,
                     w_ref, n2_ref, wr_ref, x2_ref, h2_ref, lg_ref):
    spread = sp_ref[...]
    wr = wr_ref[...]
    w_hi = wr.astype(BF16)
    w_lo = (wr - w_hi.astype(F32)).astype(BF16)
    wr3 = jnp.concatenate([w_hi, w_hi, w_lo], axis=0)

    def per_lane(w):
        hi = w.astype(BF16)
        lo = (w - hi.astype(F32)).astype(BF16)
        return _dot(jnp.concatenate([hi, lo], axis=1), spread)

    rows = x_ref.shape[0] // OUT_PROJ_SPLITS
    for part in range(OUT_PROJ_SPLITS):
        rs = slice(part * rows, (part + 1) * rows)
        lses = [r[rs, :] for r in (l1_ref, l2_ref, l3_ref)]
        top = jnp.maximum(jnp.maximum(lses[0], lses[1]), lses[2])
        wts = [jnp.exp(l - top) for l in lses]
        total = wts[0] + wts[1] + wts[2]

        def heads(o_ref):
            return jnp.concatenate([o_ref[p, rs, :] for p in range(o_ref.shape[0])], axis=1)

        ya = sum(per_lane(w / total) * heads(o) for w, o in zip(wts, (o1_ref, o2_ref, o3_ref)))
        ya = _rms(ya, ag_ref[...])
        mixed = jnp.concatenate([ym_ref[rs, :], ya], axis=1).astype(BF16)
        x2 = x_ref[rs, :] + _dot(mixed, w_ref[...])
        x2_ref[rs, :] = x2
        h2 = _rms(x2, n2_ref[...])
        hi = h2.astype(BF16)
        h2_ref[rs, :] = hi
        lo = (h2 - hi.astype(F32)).astype(BF16)
        lg_ref[rs, :] = _dot(jnp.concatenate([hi, lo, hi], axis=1), wr3)


def _out_proj(ym, branch_o, branch_lse, x2d, attn_g, w_bf, n2g, wr_pad):
    n = x2d.shape[0]
    spread = (jnp.arange(LANES)[:, None] == jnp.arange(D_ATTN)[None, :] // ATTN_HEAD_DIM)
    spread = jnp.tile(spread.astype(BF16), (2, 1))
    pairs, seq = branch_o[0].shape[1:3]
    tiles_per_seq = seq // ROW_TILE
    branch = pl.BlockSpec((None, pairs, ROW_TILE, LANES),
                          lambda i: (i // tiles_per_seq, 0, i % tiles_per_seq, 0))
    row = lambda i: (i, 0)
    fixed = lambda i: (0, 0)
    return pl.pallas_call(
        _out_proj_kernel,
        grid=(n // ROW_TILE,),
        in_specs=[
            pl.BlockSpec((ROW_TILE, D_MLSTM), row),
            branch, branch, branch,
            pl.BlockSpec((ROW_TILE, LANES), row),
            pl.BlockSpec((ROW_TILE, LANES), row),
            pl.BlockSpec((ROW_TILE, LANES), row),
            pl.BlockSpec((2 * LANES, D_ATTN), fixed),
            pl.BlockSpec((ROW_TILE, D_MODEL), row),
            pl.BlockSpec((1, D_ATTN), fixed),
            pl.BlockSpec((D_MODEL, D_MODEL), fixed),
            pl.BlockSpec((1, D_MODEL), fixed),
            pl.BlockSpec((D_MODEL, LANES), fixed),
        ],
        out_specs=[
            pl.BlockSpec((ROW_TILE, D_MODEL), row),
            pl.BlockSpec((ROW_TILE, D_MODEL), row),
            pl.BlockSpec((ROW_TILE, LANES), row),
        ],
        out_shape=[
            jax.ShapeDtypeStruct((n, D_MODEL), F32),
            jax.ShapeDtypeStruct((n, D_MODEL), BF16),
            jax.ShapeDtypeStruct((n, LANES), F32),
        ],
        compiler_params=_params(("parallel",)),
        name="out_proj",
    )(ym, *branch_o, *branch_lse, spread, x2d, attn_g, w_bf, n2g, wr_pad)


def _route_kernel(lg_ref, tri_ref, eye_ref, slot_ref, slot_t_ref, aff_ref, *, cap):
    for i in range(lg_ref.shape[0]):
        _route_one(lg_ref.at[i], tri_ref, eye_ref, slot_ref.at[i], slot_t_ref.at[i],
                   aff_ref.at[i], cap=cap)


def _route_one(lg_ref, tri_ref, eye_ref, slot_ref, slot_t_ref, aff_ref, *, cap):
    lg = lg_ref[...]
    valid = lax.broadcasted_iota(jnp.int32, (1, LANES), 1) < N_EXPERTS
    lg = jnp.where(valid, lg, NEG_INF)
    e = jnp.exp(lg - jnp.max(lg, axis=1, keepdims=True))
    aff = e / jnp.sum(e, axis=1, keepdims=True)
    aff_ref[...] = aff
    groups = LANES // N_EXPERTS
    rpg = aff.shape[0] // groups
    lane = lax.broadcasted_iota(jnp.int32, (1, LANES), 1)
    packed = aff[:rpg]
    for g in range(1, groups):
        packed = packed + pltpu.roll(aff[g * rpg:(g + 1) * rpg], g * N_EXPERTS, axis=1)

    def over_groups(x):
        shift = N_EXPERTS
        while shift < LANES:
            x = x + pltpu.roll(x, shift, axis=1)
            shift *= 2
        return x

    def enough(cand):
        part = jnp.sum(jnp.where(packed >= cand, 1.0, 0.0), axis=0, keepdims=True)
        return over_groups(part) >= cap

    def narrow(lo, hi, cands):
        new_lo, new_hi = lo, hi
        for cand in cands:
            ok = enough(cand)
            new_lo = jnp.maximum(new_lo, jnp.where(ok, cand, lo))
            new_hi = jnp.minimum(new_hi, jnp.where(ok, hi, cand))
        return new_lo, new_hi

    tiny = jnp.full((1, LANES), 2.0 ** -126, F32)
    normal = enough(tiny)
    p = tiny
    for span, count in ((16, 7), (1, 15)):
        p, _ = narrow(p, p, [p * (2.0 ** (span * j)) for j in range(1, count + 1)])
    lo = jnp.where(normal, p, 0.0)
    hi = jnp.where(normal, p * 2.0, tiny)
    width = jnp.where(normal, p, 0.0)
    for bits in THRESHOLD_RADIX_BITS:
        width = width * (0.5 ** bits)
        lo, hi = narrow(lo, hi, [lo + j * width for j in range(1, 2 ** bits)])
    gt = jnp.where(packed >= hi, 1.0, 0.0)
    eq = jnp.where(packed >= lo, 1.0, 0.0) - gt
    need = cap - over_groups(jnp.sum(gt, axis=0, keepdims=True))
    tri = tri_ref[...]

    def count_before(x):
        per_group = jnp.sum(x, axis=0, keepdims=True)
        upto = per_group
        shift = N_EXPERTS
        while shift < LANES:
            upto = upto + jnp.where(lane >= shift, pltpu.roll(upto, shift, axis=1), 0.0)
            shift *= 2
        return _dot(tri, x.astype(BF16)) + (upto - per_group)

    sel = gt + eq * jnp.where(count_before(eq) < need, 1.0, 0.0)
    pos = count_before(sel)
    slot_packed = jnp.where(sel > 0.0, pos, -1.0)
    slot = jnp.concatenate(
        [jnp.where(valid, slot_packed if g == 0 else
                   pltpu.roll(slot_packed, LANES - g * N_EXPERTS, axis=1), -1.0)
         for g in range(groups)], axis=0)
    slot_ref[...] = slot
    slot_t_ref[...] = _dot_nt(eye_ref[...], slot.astype(BF16))


def _route(logits, cap):
    b, seq, _ = logits.shape
    assert LANES % N_EXPERTS == 0 and seq % (LANES // N_EXPERTS) == 0
    rpg = seq // (LANES // N_EXPERTS)
    tri = (jnp.arange(rpg)[None, :] < jnp.arange(rpg)[:, None]).astype(BF16)
    eye = jnp.eye(LANES, dtype=BF16)
    per_b = lambda i: (i, 0, 0)
    fixed = lambda i: (0, 0)
    nb = ROUTE_BATCHES if b % ROUTE_BATCHES == 0 else 1
    return pl.pallas_call(
        functools.partial(_route_kernel, cap=cap),
        grid=(b // nb,),
        in_specs=[
            pl.BlockSpec((nb, seq, LANES), per_b),
            pl.BlockSpec((rpg, rpg), fixed),
            pl.BlockSpec((LANES, LANES), fixed),
        ],
        out_specs=[
            pl.BlockSpec((nb, seq, LANES), per_b),
            pl.BlockSpec((nb, LANES, seq), per_b),
            pl.BlockSpec((nb, seq, LANES), per_b),
        ],
        out_shape=[
            jax.ShapeDtypeStruct((b, seq, LANES), F32),
            jax.ShapeDtypeStruct((b, LANES, seq), F32),
            jax.ShapeDtypeStruct((b, seq, LANES), F32),
        ],
        compiler_params=_params(("parallel",)),
        name="route",
    )(logits, tri, eye)


def _moe_gather_kernel(slot_ref, h_ref, xs_ref):
    srow = slot_ref[pl.ds(pl.program_id(1) % 8, 1), :]
    cap, seq = xs_ref.shape[0], srow.shape[1]
    ci = lax.broadcasted_iota(jnp.int32, (cap, seq), 0).astype(F32)
    onehot = jnp.where(srow == ci, 1.0, 0.0).astype(BF16)
    xs_ref[...] = _dot(onehot, h_ref[...]).astype(BF16)


def _moe_gather(slot_t, h2, cap):
    b, seq, _ = h2.shape
    return pl.pallas_call(
        _moe_gather_kernel,
        grid=(b, N_EXPERTS),
        in_specs=[
            pl.BlockSpec((None, 8, seq), lambda i, e: (i, e // 8, 0)),
            pl.BlockSpec((None, seq, D_MODEL), lambda i, e: (i, 0, 0)),
        ],
        out_specs=pl.BlockSpec((None, None, cap, D_MODEL), lambda i, e: (i, e, 0, 0)),
        out_shape=jax.ShapeDtypeStruct((b, N_EXPERTS, cap, D_MODEL), BF16),
        compiler_params=_params(("parallel", "parallel")),
        name="moe_gather",
    )(slot_t, h2)


def _moe_ffn_kernel(xs_ref, w1_ref, w3_ref, w2_ref, y_ref, act_ref, w1b_ref, w3b_ref, w2b_ref):
    s = pl.program_id(1)
    nb, cap, _ = xs_ref.shape
    nf = act_ref.shape[0]
    per = FFN_ROW_TILE // cap
    row_tiles = nb // per

    @pl.when(s < nf)
    def _():
        w1b_ref[...] = w1_ref[...].astype(BF16)
        w3b_ref[...] = w3_ref[...].astype(BF16)
        for r in range(row_tiles):
            x = xs_ref[r * per:(r + 1) * per].reshape(FFN_ROW_TILE, D_MODEL)
            up = _dot(x, w1b_ref[...])
            gt = _dot(x, w3b_ref[...])
            act_ref[s, r * FFN_ROW_TILE:(r + 1) * FFN_ROW_TILE, :] = (_silu(up) * gt).astype(BF16)

    @pl.when(s >= nf)
    def _():
        w2b_ref[...] = w2_ref[...].astype(BF16)
        for r in range(row_tiles):
            rows = slice(r * FFN_ROW_TILE, (r + 1) * FFN_ROW_TILE)
            act = jnp.concatenate([act_ref[f, rows, :] for f in range(nf)], axis=1)
            y = _dot(act, w2b_ref[...])
            y_ref[r * per:(r + 1) * per] = y.astype(BF16).reshape(per, cap, y.shape[1])


def _moe_ffn(xs, w1, w3, w2):
    b, ne, cap, _ = xs.shape
    nf = D_EXPERT // FFN_F_TILE
    nn = D_MODEL // FFN_N_TILE
    hidden = lambda e, s: (e, 0, jnp.minimum(s, nf - 1))
    out_col = lambda e, s: jnp.maximum(s - nf, 0)
    return pl.pallas_call(
        _moe_ffn_kernel,
        grid=(ne, nf + nn),
        in_specs=[
            pl.BlockSpec((b, None, cap, D_MODEL), lambda e, s: (0, e, 0, 0)),
            pl.BlockSpec((None, D_MODEL, FFN_F_TILE), hidden),
            pl.BlockSpec((None, D_MODEL, FFN_F_TILE), hidden),
            pl.BlockSpec((None, D_EXPERT, FFN_N_TILE), lambda e, s: (e, 0, out_col(e, s))),
        ],
        out_specs=pl.BlockSpec((b, None, cap, FFN_N_TILE), lambda e, s: (0, e, 0, out_col(e, s))),
        out_shape=jax.ShapeDtypeStruct(xs.shape, BF16),
        scratch_shapes=[
            pltpu.VMEM((nf, b * cap, FFN_F_TILE), BF16),
            pltpu.VMEM((D_MODEL, FFN_F_TILE), BF16),
            pltpu.VMEM((D_MODEL, FFN_F_TILE), BF16),
            pltpu.VMEM((D_EXPERT, FFN_N_TILE), BF16),
        ],
        compiler_params=_params(("parallel", "arbitrary")),
        name="moe_ffn",
    )(xs, w1, w3, w2)


def _moe_scatter_kernel(slot_ref, aff_ref, y_ref, x2_ref, g_ref, o_ref):
    slot = slot_ref[...]
    aff = aff_ref[...]
    rows, cap = slot.shape[0], y_ref.shape[1]
    ci = lax.broadcasted_iota(jnp.int32, (rows, cap), 1).astype(F32)
    acc = x2_ref[...]
    for e in range(N_EXPERTS):
        onehot = jnp.where(slot[:, e:e + 1] == ci, 1.0, 0.0).astype(BF16)
        acc = acc + aff[:, e:e + 1] * _dot(onehot, y_ref[e])
    o_ref[...] = _rms(acc, g_ref[...])


def _moe_scatter(slot, aff, y, x2, norm_g):
    b, seq, _ = x2.shape
    cap = y.shape[2]
    tile = lambda i, r: (i, r, 0)
    return pl.pallas_call(
        _moe_scatter_kernel,
        grid=(b, seq // ROW_TILE),
        in_specs=[
            pl.BlockSpec((None, ROW_TILE, LANES), tile),
            pl.BlockSpec((None, ROW_TILE, LANES), tile),
            pl.BlockSpec((None, N_EXPERTS, cap, D_MODEL), lambda i, r: (i, 0, 0, 0)),
            pl.BlockSpec((None, ROW_TILE, D_MODEL), tile),
            pl.BlockSpec((1, D_MODEL), lambda i, r: (0, 0)),
        ],
        out_specs=pl.BlockSpec((None, ROW_TILE, D_MODEL), tile),
        out_shape=jax.ShapeDtypeStruct((b, seq, D_MODEL), F32),
        compiler_params=_params(("parallel", "parallel")),
        name="moe_scatter",
    )(slot, aff, y, x2, norm_g)


def kernel(x, norm1_g, w_in, conv_w, conv_b, wq_m, wk_m, wv_m, w_if_fwd, b_if_fwd,
           w_if_bwd, b_if_bwd, mlstm_norm_g, mlstm_skip, attn_norm_g, w_out, norm2_g,
           w_router, w1, w3, w2, norm_f_g):
    b, seq, _ = x.shape
    assert w_in.shape[0] == 1, "single-layer problem"
    assert seq % ROW_TILE == 0 and seq % MLSTM_CHUNK == 0 and seq % ATTN_Q_TILE == 0
    cap = EC_CAPACITY * seq // N_EXPERTS
    assert FFN_ROW_TILE % cap == 0 and (b * cap) % FFN_ROW_TILE == 0
    l = 0
    x2d = x.reshape(b * seq, D_MODEL)
    xm, z, *qkv_views = _in_proj(x2d, norm1_g[l][None, :], w_in[l].astype(BF16), seq)
    shp = lambda t: t.reshape(b, seq, t.shape[-1])
    wif_rows, bif_rows = _gate_rows(w_if_fwd[l], b_if_fwd[l], w_if_bwd[l], b_if_bwd[l])
    xc, qm, ktm, vm, gates = _mlstm_pre(
        shp(xm), conv_w[l], conv_b[l][None, :],
        _block_diag(wq_m[l]).astype(BF16), _block_diag(wk_m[l]).T.astype(BF16),
        _block_diag(wv_m[l]).astype(BF16), wif_rows.astype(BF16), bif_rows)
    ym = _mlstm(qm, ktm, vm, gates, xc, shp(z), mlstm_norm_g[l][None, :],
                mlstm_skip[l][None, :])
    branches = [_band_attn(*qkv_views[3 * d:3 * d + 3], seq, win, dil)
                for d, (win, dil) in enumerate(DILATED_PATTERNS)]
    wr_pad = jnp.pad(w_router[l], ((0, 0), (0, LANES - N_EXPERTS)))
    x2, h2, logits = _out_proj(
        ym.reshape(b * seq, D_MLSTM), [o for o, _ in branches], [s for _, s in branches], x2d,
        attn_norm_g[l][None, :], w_out[l].astype(BF16), norm2_g[l][None, :], wr_pad)
    slot, slot_t, aff = _route(logits.reshape(b, seq, LANES), cap)
    xs = _moe_gather(slot_t, h2.reshape(b, seq, D_MODEL), cap)
    y = _moe_ffn(xs, w1[l], w3[l], w2[l])
    return _moe_scatter(slot, aff, y, x2.reshape(b, seq, D_MODEL), norm_f_g[None, :])
```

```python
import functools

import jax
import jax.numpy as jnp
from jax import lax
from jax.experimental import pallas as pl
from jax.experimental.pallas import tpu as pltpu

F32 = jnp.float32
BF16 = jnp.bfloat16

D_MODEL = 1024
D_MLSTM = 512
D_ATTN = 512
D_IN_PROJ = 2 * D_MLSTM + 3 * D_ATTN
MLSTM_HEADS = 4
MLSTM_HEAD_DIM = 128
MLSTM_QKV_BLOCK = 4
MLSTM_CONV = 5
ATTN_HEAD_DIM = 64
ROPE_DIM = 16
ROPE_THETA = 500000.0
DILATED_PATTERNS = ((128, 1), (512, 4), (2048, 16))
N_EXPERTS = 16
EC_CAPACITY = 2
D_EXPERT = 2816
NORM_EPS = 1e-6
NEG_INF = -1e30

LANES = 128
MLSTM_CHUNK = 128
ROW_TILE = 512
OUT_PROJ_SPLITS = 2
ROUTE_BATCHES = 2
ATTN_Q_TILE = 128
FFN_F_TILE = 256
FFN_N_TILE = 256
FFN_ROW_TILE = 512
THRESHOLD_RADIX_BITS = (4, 4, 4, 4, 4, 3, 4, 4, 4, 4)
V7X_VMEM_BYTES = 64 * 1024 * 1024
VMEM_LIMIT = V7X_VMEM_BYTES * 7 // 8


def _params(sem):
    return pltpu.CompilerParams(dimension_semantics=sem, vmem_limit_bytes=VMEM_LIMIT)


def _rms(x, g):
    return x * lax.rsqrt(jnp.mean(x * x, axis=-1, keepdims=True) + NORM_EPS) * g


def _silu(x):
    return x * (1.0 / (1.0 + jnp.exp(-x)))


def _dot(a, b):
    return jnp.dot(a, b, preferred_element_type=F32)


def _dot_nt(a, b):
    return lax.dot_general(a, b, (((1,), (1,)), ((), ())), preferred_element_type=F32)


def _in_proj_kernel(x_ref, g_ref, w_ref, cos_ref, sa_ref, sb_ref, xm_ref, z_ref, *rest):
    qkv_refs, scr_ref = rest[:-1], rest[-1]
    h = _rms(x_ref[...], g_ref[...])
    p = _dot(h.astype(BF16), w_ref[...])
    xm_ref[...] = p[:, :D_MLSTM]
    z_ref[...] = p[:, D_MLSTM:2 * D_MLSTM]
    cos, sa, sb = cos_ref[...], sa_ref[...], sb_ref[...]
    half = ROPE_DIM // 2

    def rope(t):
        outs = []
        for j in range(D_ATTN // LANES):
            tj = t[:, j * LANES:(j + 1) * LANES]
            up = pltpu.roll(tj, LANES - half, axis=1)
            dn = pltpu.roll(tj, half, axis=1)
            outs.append(tj * cos + up * sa + dn * sb)
        return jnp.concatenate(outs, axis=1)

    o = 2 * D_MLSTM
    qkv = (rope(p[:, o:o + D_ATTN]) * (ATTN_HEAD_DIM ** -0.5),
           rope(p[:, o + D_ATTN:o + 2 * D_ATTN]),
           p[:, o + 2 * D_ATTN:])
    rows = p.shape[0]
    groups = D_ATTN // LANES
    for a, val in enumerate(qkv):
        for j in range(groups):
            scr_ref[j] = val[:, j * LANES:(j + 1) * LANES]
        for d, (_, dil) in enumerate(DILATED_PATTERNS):
            ref = qkv_refs[3 * d + a]
            if dil == 1:
                ref[...] = val.astype(BF16)
                continue
            for r in range(dil):
                for j in range(groups):
                    piece = scr_ref[j, pl.ds(r, rows // dil, stride=dil), :]
                    ref[:, r * D_ATTN + j * LANES:r * D_ATTN + (j + 1) * LANES] = piece.astype(BF16)


def _rope_tables(seq):
    half = ROPE_DIM // 2
    inv_freq = ROPE_THETA ** (-2.0 * jnp.arange(half, dtype=F32) / ROPE_DIM)
    ang = jnp.arange(seq).astype(F32)[:, None] * inv_freq[None, :]
    cos, sin = jnp.cos(ang), jnp.sin(ang)
    pad = jnp.zeros((seq, ATTN_HEAD_DIM - ROPE_DIM), F32)
    cos_h = jnp.concatenate([cos, cos, pad + 1.0], axis=1)
    sa_h = jnp.concatenate([-sin, jnp.zeros_like(sin), pad], axis=1)
    sb_h = jnp.concatenate([jnp.zeros_like(sin), sin, pad], axis=1)
    rep = LANES // ATTN_HEAD_DIM
    return tuple(jnp.tile(t, (1, rep)) for t in (cos_h, sa_h, sb_h))


def _in_proj(x2d, g, w_bf, seq):
    n = x2d.shape[0]
    tiles_per_seq = seq // ROW_TILE
    cos, sa, sb = _rope_tables(seq)
    row = lambda i: (i, 0)
    fixed = lambda i: (0, 0)
    pos = lambda i: (i % tiles_per_seq, 0)
    return pl.pallas_call(
        _in_proj_kernel,
        grid=(n // ROW_TILE,),
        in_specs=[
            pl.BlockSpec((ROW_TILE, D_MODEL), row),
            pl.BlockSpec((1, D_MODEL), fixed),
            pl.BlockSpec((D_MODEL, D_IN_PROJ), fixed),
            pl.BlockSpec((ROW_TILE, LANES), pos),
            pl.BlockSpec((ROW_TILE, LANES), pos),
            pl.BlockSpec((ROW_TILE, LANES), pos),
        ],
        out_specs=[
            pl.BlockSpec((ROW_TILE, D_MLSTM), row),
            pl.BlockSpec((ROW_TILE, D_MLSTM), row),
        ] + [pl.BlockSpec((ROW_TILE // dil, dil * D_ATTN), row)
             for _, dil in DILATED_PATTERNS for _ in range(3)],
        out_shape=[
            jax.ShapeDtypeStruct((n, D_MLSTM), F32),
            jax.ShapeDtypeStruct((n, D_MLSTM), F32),
        ] + [jax.ShapeDtypeStruct((n // dil, dil * D_ATTN), BF16)
             for _, dil in DILATED_PATTERNS for _ in range(3)],
        scratch_shapes=[pltpu.VMEM((D_ATTN // LANES, ROW_TILE, LANES), F32)],
        compiler_params=_params(("parallel",)),
        name="in_proj",
    )(x2d, g, w_bf, cos, sa, sb)


def _mlstm_pre_kernel(xm_ref, cw_ref, cb_ref, wq_ref, wkt_ref, wv_ref, wif_ref, bif_ref,
                      xc_ref, q_ref, kt_ref, v_ref, g_ref):
    x = xm_ref[...]
    seq = x.shape[0]
    t = lax.broadcasted_iota(jnp.int32, x.shape, 0)
    acc = jnp.zeros_like(x) + cb_ref[...]
    for j in range(MLSTM_CONV):
        d = j - MLSTM_CONV // 2
        if d == 0:
            tap = x
        else:
            tap = pltpu.roll(x, (-d) % seq, axis=0)
            tap = jnp.where((t + d >= 0) & (t + d < seq), tap, 0.0)
        acc = acc + tap * cw_ref[j:j + 1, :]
    xc = _silu(acc)
    xc_ref[...] = xc
    xcb = xc.astype(BF16)
    q = _dot(xcb, wq_ref[...]).astype(BF16)
    kt = (_dot_nt(wkt_ref[...], xcb) * (MLSTM_HEAD_DIM ** -0.5)).astype(BF16)
    v = _dot(x.astype(BF16), wv_ref[...]).astype(BF16)
    q_ref[...] = q
    v_ref[...] = v
    L = MLSTM_CHUNK
    for c in range(seq // L):
        kt_ref[c] = kt[:, c * L:(c + 1) * L]
    wif = wif_ref[...]
    g = (_dot_nt(wif[:, :D_MLSTM], q) + _dot(wif[:, D_MLSTM:2 * D_MLSTM], kt)
         + _dot_nt(wif[:, 2 * D_MLSTM:], v) + bif_ref[...])
    for kind in range(2):
        for hd in range(MLSTM_HEADS):
            r0 = (kind * MLSTM_HEADS + hd) * 8
            for c in range(seq // L):
                g_ref[hd, kind, c * 8:(c + 1) * 8, :] = g[r0:r0 + 8, c * L:(c + 1) * L]


def _block_diag(w):
    nblk = w.shape[0]
    n = nblk * MLSTM_QKV_BLOCK
    tiled = jnp.tile(w.reshape(n, MLSTM_QKV_BLOCK), (1, nblk))
    blk = jnp.arange(n) // MLSTM_QKV_BLOCK
    return jnp.where(blk[:, None] == blk[None, :], tiled, 0.0)


def _gate_rows(w_f, b_f, w_b, b_b):
    h = MLSTM_HEADS

    def rows(f, b):
        pair = jnp.stack([f, b], axis=-1)
        pair = jnp.concatenate([pair[..., h:, :], pair[..., :h, :]], axis=-2)
        pair = jnp.pad(pair, [(0, 0)] * (pair.ndim - 1) + [(0, 6)])
        return pair.reshape(*pair.shape[:-2], 2 * h * 8)

    return rows(w_f, w_b).T, rows(b_f, b_b)[:, None]


def _mlstm_pre(xm, conv_w, conv_b, wq, wkt, wv, wif_rows, bif_rows):
    b, seq, _ = xm.shape
    nrow = wif_rows.shape[0]
    nc = seq // MLSTM_CHUNK
    per_b = lambda i: (i, 0, 0)
    fixed = lambda i: (0, 0)
    return pl.pallas_call(
        _mlstm_pre_kernel,
        grid=(b,),
        in_specs=[
            pl.BlockSpec((None, seq, D_MLSTM), per_b),
            pl.BlockSpec((MLSTM_CONV, D_MLSTM), fixed),
            pl.BlockSpec((1, D_MLSTM), fixed),
            pl.BlockSpec((D_MLSTM, D_MLSTM), fixed),
            pl.BlockSpec((D_MLSTM, D_MLSTM), fixed),
            pl.BlockSpec((D_MLSTM, D_MLSTM), fixed),
            pl.BlockSpec((nrow, 3 * D_MLSTM), fixed),
            pl.BlockSpec((nrow, 1), fixed),
        ],
        out_specs=[
            pl.BlockSpec((None, seq, D_MLSTM), per_b),
            pl.BlockSpec((None, seq, D_MLSTM), per_b),
            pl.BlockSpec((None, nc, D_MLSTM, MLSTM_CHUNK), lambda i: (i, 0, 0, 0)),
            pl.BlockSpec((None, seq, D_MLSTM), per_b),
            pl.BlockSpec((None, MLSTM_HEADS, 2, nc * 8, MLSTM_CHUNK), lambda i: (i, 0, 0, 0, 0)),
        ],
        out_shape=[
            jax.ShapeDtypeStruct((b, seq, D_MLSTM), F32),
            jax.ShapeDtypeStruct((b, seq, D_MLSTM), BF16),
            jax.ShapeDtypeStruct((b, nc, D_MLSTM, MLSTM_CHUNK), BF16),
            jax.ShapeDtypeStruct((b, seq, D_MLSTM), BF16),
            jax.ShapeDtypeStruct((b, MLSTM_HEADS, 2, nc * 8, MLSTM_CHUNK), F32),
        ],
        compiler_params=_params(("parallel",)),
        name="mlstm_pre",
    )(xm, conv_w, conv_b, wq, wkt, wv, wif_rows, bif_rows)


def _log_sigmoid(x):
    return jnp.minimum(x, 0.0) - jnp.log1p(jnp.exp(-jnp.abs(x)))


def _split3(x):
    hi = x.astype(BF16).astype(F32)
    mid = (x - hi).astype(BF16).astype(F32)
    lo = (x - hi - mid).astype(BF16).astype(F32)
    return hi, mid, lo


def _mlstm_kernel(q_ref, kt_ref, v_ref, g_ref, xc_ref, z_ref, ng_ref, sk_ref, o_ref,
                  w_ref, ml_ref, tot_ref, pm_ref, tb_ref, tr_ref, dc_ref, cs_ref, ms_ref):
    L = MLSTM_CHUNK
    dh = MLSTM_HEAD_DIM
    nc = kt_ref.shape[0]
    rows_all = g_ref.shape[1]
    lane = lax.broadcasted_iota(jnp.int32, (rows_all, L), 1)
    sub = lax.broadcasted_iota(jnp.int32, (rows_all, L), 0) % 8
    fwd_row = sub == 0
    row_i = lax.broadcasted_iota(jnp.int32, (L, L), 0)
    col_i = lax.broadcasted_iota(jnp.int32, (L, L), 1)
    ones_col = jnp.ones((L, dh), BF16)

    lf = _log_sigmoid(g_ref[0])
    pre, suf = lf, lf
    d = 1
    while d < L:
        pre = pre + jnp.where(lane >= d, pltpu.roll(pre, d, axis=1), 0.0)
        suf = suf + jnp.where(lane < L - d, pltpu.roll(suf, L - d, axis=1), 0.0)
        d *= 2
    cum = jnp.where(fwd_row, pre, suf)
    tot = jnp.where(fwd_row, cum[:, L - 1:L], cum[:, 0:1])
    a = tot - cum + g_ref[1]
    ml = jnp.max(a, axis=1, keepdims=True)
    w_ref[...] = jnp.exp(a - ml)
    ml_ref[...] = jnp.broadcast_to(ml, (rows_all, L))
    tot_ref[...] = tot
    r = g_ref[1] - cum
    pmax, smax = r, r
    d = 1
    while d < L:
        pmax = jnp.maximum(pmax, jnp.where(lane >= d, pltpu.roll(pmax, d, axis=1), NEG_INF))
        smax = jnp.maximum(smax, jnp.where(lane < L - d, pltpu.roll(smax, L - d, axis=1), NEG_INF))
        d *= 2
    pm_ref[...] = jnp.where(fwd_row, pmax, smax)

    def tile_bcast(x, src):
        y = jnp.where(sub == src, x, 0.0)
        if src:
            y = pltpu.roll(y, rows_all - src, axis=0)
        for s in (1, 2, 4):
            y = y + pltpu.roll(y, s, axis=0)
        return y

    for dr in range(2):
        b_hi, b_mid, b_lo = _split3(tile_bcast(cum, dr))
        r_hi, r_mid, r_lo = _split3(tile_bcast(r, dr))
        tb_ref[dr] = jnp.where(sub == 1, b_hi, jnp.where(sub == 2, b_mid, jnp.where(
            sub == 3, b_lo, jnp.where((sub >= 4) & (sub <= 6), 1.0, 0.0))))
        tr_ref[dr] = jnp.where(sub == 0, 1.0, jnp.where(sub == 4, r_hi, jnp.where(
            sub == 5, r_mid, jnp.where(sub == 6, r_lo, 0.0))))

    def chunk_rows(ref, c):
        return ref[pl.ds(pl.multiple_of(c * 8, 8), 8), :]

    def v_aug(c):
        return jnp.concatenate([v_ref[pl.ds(c * L, L), :], ones_col], axis=1)

    def phase_a(c, carry):
        kt = kt_ref[c].astype(F32)
        w = chunk_rows(w_ref, c)
        kw = jnp.concatenate([kt * w[0:1], kt * w[1:2]], axis=0)
        dc_ref[c] = _dot(kw.astype(BF16), v_aug(c))
        return carry

    lax.fori_loop(0, nc, phase_a, 0, unroll=8)

    ms_ref[...] = jnp.zeros(ms_ref.shape, F32)
    def scan_dir(direction_row, reverse):
        off_rows = direction_row * dh
        off_cols = direction_row * 2 * dh

        def body(i, carry):
            c = (nc - 1 - i) if reverse else i
            state, m = carry
            cs_ref[c, :, off_cols:off_cols + 2 * dh] = state.astype(BF16)
            ms_ref[c, direction_row:direction_row + 1, :] = m
            ml_c = chunk_rows(ml_ref, c)[direction_row:direction_row + 1]
            g_c = chunk_rows(tot_ref, c)[direction_row:direction_row + 1]
            m_new = jnp.maximum(g_c + m, ml_c)
            alpha = jnp.exp(g_c + m - m_new)
            beta = jnp.exp(ml_c - m_new)
            alpha2 = jnp.concatenate([alpha, alpha], axis=1)
            beta2 = jnp.concatenate([beta, beta], axis=1)
            state = alpha2 * state + beta2 * dc_ref[c, off_rows:off_rows + dh, :]
            return state, m_new

        init = (jnp.zeros((dh, 2 * dh), F32), jnp.zeros((1, L), F32))
        lax.fori_loop(0, nc, body, init, unroll=16)

    scan_dir(0, False)
    scan_dir(1, True)

    ng = ng_ref[...]
    sk = sk_ref[...]

    sub8 = lax.broadcasted_iota(jnp.int32, (8, L), 0)
    floor_rows = jnp.where(sub8 == 0, 1.0, jnp.where(sub8 <= 3, -1.0, 0.0))
    no_rows = jnp.zeros((8, 3 * L), F32)

    def direction(s_qk, qc, vaug, ex, keep):
        w = jnp.exp(jnp.where(keep, ex[:, :L], NEG_INF)) * s_qk
        scale = jnp.exp(ex[:, L:2 * L])
        intra = _dot(w.astype(BF16), vaug)
        tot_c = intra + jnp.concatenate([scale, scale], axis=1) * qc
        return tot_c[:, :dh] / jnp.maximum(jnp.abs(tot_c[:, dh:]), jnp.exp(ex[:, 2 * L:]))

    def phase_c(c, carry):
        rows = pl.ds(c * L, L)
        q = q_ref[rows, :]
        s_qk = _dot(q, kt_ref[c])
        qc = _dot(q, cs_ref[c])
        vaug = v_aug(c)
        ms = ms_ref[c]
        e = -jnp.maximum(ms, chunk_rows(pm_ref, c))
        e = (e - jnp.abs(e) * (2.0 ** -7)).astype(BF16).astype(F32)
        lhs_tiles, rhs_tiles = [], []
        for dr in range(2):
            e_rows = jnp.broadcast_to(e[dr:dr + 1], (8, L))
            lhs_tiles.append(jnp.where(sub8 == 0, e_rows, chunk_rows(tb_ref.at[dr], c)))
            m_hi, m_mid, m_lo = _split3(jnp.broadcast_to(ms[dr:dr + 1], (8, L)))
            scale_rows = jnp.where(sub8 == 0, 1.0, jnp.where(sub8 == 4, m_hi, jnp.where(
                sub8 == 5, m_mid, jnp.where(sub8 == 6, m_lo, 0.0))))
            rhs_tiles.append(jnp.concatenate(
                [chunk_rows(tr_ref.at[dr], c), scale_rows, floor_rows], axis=1))
        lhs = jnp.concatenate(lhs_tiles, axis=0).astype(BF16)
        rhs = jnp.concatenate([jnp.concatenate([rhs_tiles[0], no_rows], axis=1),
                               jnp.concatenate([no_rows, rhs_tiles[1]], axis=1)],
                              axis=0).astype(BF16)
        ex = lax.dot_general(lhs, rhs, (((0,), (0,)), ((), ())),
                             preferred_element_type=F32)
        h = (direction(s_qk, qc[:, :2 * dh], vaug, ex[:, :3 * L], col_i <= row_i)
             + direction(s_qk, qc[:, 2 * dh:], vaug, ex[:, 3 * L:], col_i >= row_i))
        hn = _rms(h, ng)
        o_ref[rows, :] = (hn + sk * xc_ref[rows, :]) * _silu(z_ref[rows, :])
        return carry

    lax.fori_loop(0, nc, phase_c, 0, unroll=16)


def _mlstm(q, kt, v, gates, xc, z, norm_g, skip):
    b, seq, _ = q.shape
    nc = seq // MLSTM_CHUNK
    dh = MLSTM_HEAD_DIM
    head = lambda i, j: (i, 0, j)
    vec = lambda i, j: (0, j)
    blk = pl.BlockSpec((None, seq, dh), head)
    gate_rows = pltpu.VMEM((nc * 8, MLSTM_CHUNK), F32)
    gate_tiles = pltpu.VMEM((2, nc * 8, MLSTM_CHUNK), F32)
    return pl.pallas_call(
        _mlstm_kernel,
        grid=(b, MLSTM_HEADS),
        in_specs=[
            blk,
            pl.BlockSpec((None, nc, dh, MLSTM_CHUNK), lambda i, j: (i, 0, j, 0)),
            blk,
            pl.BlockSpec((None, None, 2, nc * 8, MLSTM_CHUNK), lambda i, j: (i, j, 0, 0, 0)),
            blk, blk,
            pl.BlockSpec((1, dh), vec),
            pl.BlockSpec((1, dh), vec),
        ],
        out_specs=blk,
        out_shape=jax.ShapeDtypeStruct((b, seq, D_MLSTM), F32),
        scratch_shapes=[
            gate_rows, gate_rows, gate_rows, gate_rows, gate_tiles, gate_tiles,
            pltpu.VMEM((nc, 2 * dh, 2 * dh), F32),
            pltpu.VMEM((nc, dh, 4 * dh), BF16),
            pltpu.VMEM((nc, 8, MLSTM_CHUNK), F32),
        ],
        compiler_params=_params(("parallel", "parallel")),
        name="mlstm",
    )(q, kt, v, gates, xc, z, norm_g, skip)


def _band_attn_kernel(q_ref, k_ref, v_ref, o_ref, lse_ref, *, half, dil):
    lsub = q_ref.shape[0]
    tq = ATTN_Q_TILE
    win = min(lsub, 2 * tq)
    first = lax.broadcasted_iota(jnp.int32, (1, LANES), 1) < ATTN_HEAD_DIM
    lane = lax.broadcasted_iota(jnp.int32, (tq, LANES), 1)
    rel = (lax.broadcasted_iota(jnp.int32, (2 * tq, win), 1)
           - lax.broadcasted_iota(jnp.int32, (2 * tq, win), 0) % tq)

    def tile(qs, ws, out_rows):
        keep = jnp.abs(rel + (ws - qs)) <= half
        for c in range(dil):
            lse_tile = jnp.zeros((tq, LANES), F32)
            for p in range(D_ATTN // LANES):
                lanes = slice(c * D_ATTN + p * LANES, c * D_ATTN + (p + 1) * LANES)
                q = q_ref[pl.ds(qs, tq), lanes]
                kw = k_ref[pl.ds(ws, win), lanes]
                vw = v_ref[pl.ds(ws, win), lanes]
                zero = jnp.zeros_like(q)
                q2 = jnp.concatenate([jnp.where(first, q, zero), jnp.where(first, zero, q)], axis=0)
                s = jnp.where(keep, _dot_nt(q2, kw), NEG_INF)
                m = jnp.max(s, axis=1, keepdims=True)
                e = jnp.exp(s - m)
                l = jnp.sum(e, axis=1, keepdims=True)
                o2 = _dot(e.astype(BF16), vw) / l
                o_ref[p, out_rows(c), :] = jnp.where(first, o2[:tq], o2[tq:])
                lse2 = m + jnp.log(l)
                lse_tile = jnp.where(lane == 2 * p, lse2[:tq],
                                     jnp.where(lane == 2 * p + 1, lse2[tq:], lse_tile))
            lse_ref[out_rows(c), :] = lse_tile

    if dil == 1:
        def body(t, carry):
            qs = pl.multiple_of(t * tq, tq)
            ws = pl.multiple_of(jnp.clip(qs - half, 0, lsub - win), half)
            tile(qs, ws, lambda c: pl.ds(qs, tq))
            return carry

        lax.fori_loop(0, lsub // tq, body, 0, unroll=8)
    else:
        for t in range(lsub // tq):
            qs = t * tq
            ws = min(max(qs - half, 0), lsub - win)
            tile(qs, ws, lambda c, qs=qs: pl.ds(qs * dil + c, tq, stride=dil))


def _band_attn(q, k, v, seq, win, dil):
    lsub = seq // dil
    b = q.shape[0] // lsub
    half = win // (2 * dil)
    pairs = D_ATTN // LANES
    assert lsub % ATTN_Q_TILE == 0 and half % 16 == 0 and ATTN_Q_TILE + 2 * half <= 2 * ATTN_Q_TILE
    blk = pl.BlockSpec((lsub, dil * D_ATTN), lambda i: (i, 0))
    return pl.pallas_call(
        functools.partial(_band_attn_kernel, half=half, dil=dil),
        grid=(b,),
        in_specs=[blk, blk, blk],
        out_specs=[pl.BlockSpec((None, pairs, seq, LANES), lambda i: (i, 0, 0, 0)),
                   pl.BlockSpec((seq, LANES), lambda i: (i, 0))],
        out_shape=[
            jax.ShapeDtypeStruct((b, pairs, seq, LANES), F32),
            jax.ShapeDtypeStruct((b * seq, LANES), F32),
        ],
        compiler_params=_params(("parallel",)),
        name=f"band_attn_d{dil}",
    )(q, k, v)


def _out_proj_kernel(ym_ref, o1_ref, o2_ref, o3_ref, l1_ref, l2_ref, l3_ref, sp_ref, x_ref, ag_ref,
                     w_ref, n2_ref, wr_ref, x2_ref, h2_ref, lg_ref):
    spread = sp_ref[...]
    wr = wr_ref[...]
    w_hi = wr.astype(BF16)
    w_lo = (wr - w_hi.astype(F32)).astype(BF16)
    wr3 = jnp.concatenate([w_hi, w_hi, w_lo], axis=0)

    def per_lane(w):
        hi = w.astype(BF16)
        lo = (w - hi.astype(F32)).astype(BF16)
        return _dot(jnp.concatenate([hi, lo], axis=1), spread)

    rows = x_ref.shape[0] // OUT_PROJ_SPLITS
    for part in range(OUT_PROJ_SPLITS):
        rs = slice(part * rows, (part + 1) * rows)
        lses = [r[rs, :] for r in (l1_ref, l2_ref, l3_ref)]
        top = jnp.maximum(jnp.maximum(lses[0], lses[1]), lses[2])
        wts = [jnp.exp(l - top) for l in lses]
        total = wts[0] + wts[1] + wts[2]

        def heads(o_ref):
            return jnp.concatenate([o_ref[p, rs, :] for p in range(o_ref.shape[0])], axis=1)

        ya = sum(per_lane(w / total) * heads(o) for w, o in zip(wts, (o1_ref, o2_ref, o3_ref)))
        ya = _rms(ya, ag_ref[...])
        mixed = jnp.concatenate([ym_ref[rs, :], ya], axis=1).astype(BF16)
        x2 = x_ref[rs, :] + _dot(mixed, w_ref[...])
        x2_ref[rs, :] = x2
        h2 = _rms(x2, n2_ref[...])
        hi = h2.astype(BF16)
        h2_ref[rs, :] = hi
        lo = (h2 - hi.astype(F32)).astype(BF16)
        lg_ref[rs, :] = _dot(jnp.concatenate([hi, lo, hi], axis=1), wr3)


def _out_proj(ym, branch_o, branch_lse, x2d, attn_g, w_bf, n2g, wr_pad):
    n = x2d.shape[0]
    spread = (jnp.arange(LANES)[:, None] == jnp.arange(D_ATTN)[None, :] // ATTN_HEAD_DIM)
    spread = jnp.tile(spread.astype(BF16), (2, 1))
    pairs, seq = branch_o[0].shape[1:3]
    tiles_per_seq = seq // ROW_TILE
    branch = pl.BlockSpec((None, pairs, ROW_TILE, LANES),
                          lambda i: (i // tiles_per_seq, 0, i % tiles_per_seq, 0))
    row = lambda i: (i, 0)
    fixed = lambda i: (0, 0)
    return pl.pallas_call(
        _out_proj_kernel,
        grid=(n // ROW_TILE,),
        in_specs=[
            pl.BlockSpec((ROW_TILE, D_MLSTM), row),
            branch, branch, branch,
            pl.BlockSpec((ROW_TILE, LANES), row),
            pl.BlockSpec((ROW_TILE, LANES), row),
            pl.BlockSpec((ROW_TILE, LANES), row),
            pl.BlockSpec((2 * LANES, D_ATTN), fixed),
            pl.BlockSpec((ROW_TILE, D_MODEL), row),
            pl.BlockSpec((1, D_ATTN), fixed),
            pl.BlockSpec((D_MODEL, D_MODEL), fixed),
            pl.BlockSpec((1, D_MODEL), fixed),
            pl.BlockSpec((D_MODEL, LANES), fixed),
        ],
        out_specs=[
            pl.BlockSpec((ROW_TILE, D_MODEL), row),
            pl.BlockSpec((ROW_TILE, D_MODEL), row),
            pl.BlockSpec((ROW_TILE, LANES), row),
        ],
        out_shape=[
            jax.ShapeDtypeStruct((n, D_MODEL), F32),
            jax.ShapeDtypeStruct((n, D_MODEL), BF16),
            jax.ShapeDtypeStruct((n, LANES), F32),
        ],
        compiler_params=_params(("parallel",)),
        name="out_proj",
    )(ym, *branch_o, *branch_lse, spread, x2d, attn_g, w_bf, n2g, wr_pad)


def _route_kernel(lg_ref, tri_ref, eye_ref, slot_ref, slot_t_ref, aff_ref, *, cap):
    for i in range(lg_ref.shape[0]):
        _route_one(lg_ref.at[i], tri_ref, eye_ref, slot_ref.at[i], slot_t_ref.at[i],
                   aff_ref.at[i], cap=cap)


def _route_one(lg_ref, tri_ref, eye_ref, slot_ref, slot_t_ref, aff_ref, *, cap):
    lg = lg_ref[...]
    valid = lax.broadcasted_iota(jnp.int32, (1, LANES), 1) < N_EXPERTS
    lg = jnp.where(valid, lg, NEG_INF)
    e = jnp.exp(lg - jnp.max(lg, axis=1, keepdims=True))
    aff = e / jnp.sum(e, axis=1, keepdims=True)
    aff_ref[...] = aff
    groups = LANES // N_EXPERTS
    rpg = aff.shape[0] // groups
    lane = lax.broadcasted_iota(jnp.int32, (1, LANES), 1)
    packed = aff[:rpg]
    for g in range(1, groups):
        packed = packed + pltpu.roll(aff[g * rpg:(g + 1) * rpg], g * N_EXPERTS, axis=1)

    def over_groups(x):
        shift = N_EXPERTS
        while shift < LANES:
            x = x + pltpu.roll(x, shift, axis=1)
            shift *= 2
        return x

    def enough(cand):
        part = jnp.sum(jnp.where(packed >= cand, 1.0, 0.0), axis=0, keepdims=True)
        return over_groups(part) >= cap

    def narrow(lo, hi, cands):
        new_lo, new_hi = lo, hi
        for cand in cands:
            ok = enough(cand)
            new_lo = jnp.maximum(new_lo, jnp.where(ok, cand, lo))
            new_hi = jnp.minimum(new_hi, jnp.where(ok, hi, cand))
        return new_lo, new_hi

    tiny = jnp.full((1, LANES), 2.0 ** -126, F32)
    normal = enough(tiny)
    p = tiny
    for span, count in ((16, 7), (1, 15)):
        p, _ = narrow(p, p, [p * (2.0 ** (span * j)) for j in range(1, count + 1)])
    lo = jnp.where(normal, p, 0.0)
    hi = jnp.where(normal, p * 2.0, tiny)
    width = jnp.where(normal, p, 0.0)
    for bits in THRESHOLD_RADIX_BITS:
        width = width * (0.5 ** bits)
        lo, hi = narrow(lo, hi, [lo + j * width for j in range(1, 2 ** bits)])
    gt = jnp.where(packed >= hi, 1.0, 0.0)
    eq = jnp.where(packed >= lo, 1.0, 0.0) - gt
    need = cap - over_groups(jnp.sum(gt, axis=0, keepdims=True))
    tri = tri_ref[...]

    def count_before(x):
        per_group = jnp.sum(x, axis=0, keepdims=True)
        upto = per_group
        shift = N_EXPERTS
        while shift < LANES:
            upto = upto + jnp.where(lane >= shift, pltpu.roll(upto, shift, axis=1), 0.0)
            shift *= 2
        return _dot(tri, x.astype(BF16)) + (upto - per_group)

    sel = gt + eq * jnp.where(count_before(eq) < need, 1.0, 0.0)
    pos = count_before(sel)
    slot_packed = jnp.where(sel > 0.0, pos, -1.0)
    slot = jnp.concatenate(
        [jnp.where(valid, slot_packed if g == 0 else
                   pltpu.roll(slot_packed, LANES - g * N_EXPERTS, axis=1), -1.0)
         for g in range(groups)], axis=0)
    slot_ref[...] = slot
    slot_t_ref[...] = _dot_nt(eye_ref[...], slot.astype(BF16))


def _route(logits, cap):
    b, seq, _ = logits.shape
    assert LANES % N_EXPERTS == 0 and seq % (LANES // N_EXPERTS) == 0
    rpg = seq // (LANES // N_EXPERTS)
    tri = (jnp.arange(rpg)[None, :] < jnp.arange(rpg)[:, None]).astype(BF16)
    eye = jnp.eye(LANES, dtype=BF16)
    per_b = lambda i: (i, 0, 0)
    fixed = lambda i: (0, 0)
    nb = ROUTE_BATCHES if b % ROUTE_BATCHES == 0 else 1
    return pl.pallas_call(
        functools.partial(_route_kernel, cap=cap),
        grid=(b // nb,),
        in_specs=[
            pl.BlockSpec((nb, seq, LANES), per_b),
            pl.BlockSpec((rpg, rpg), fixed),
            pl.BlockSpec((LANES, LANES), fixed),
        ],
        out_specs=[
            pl.BlockSpec((nb, seq, LANES), per_b),
            pl.BlockSpec((nb, LANES, seq), per_b),
            pl.BlockSpec((nb, seq, LANES), per_b),
        ],
        out_shape=[
            jax.ShapeDtypeStruct((b, seq, LANES), F32),
            jax.ShapeDtypeStruct((b, LANES, seq), F32),
            jax.ShapeDtypeStruct((b, seq, LANES), F32),
        ],
        compiler_params=_params(("parallel",)),
        name="route",
    )(logits, tri, eye)


def _moe_gather_kernel(slot_ref, h_ref, xs_ref):
    srow = slot_ref[pl.ds(pl.program_id(1) % 8, 1), :]
    cap, seq = xs_ref.shape[0], srow.shape[1]
    ci = lax.broadcasted_iota(jnp.int32, (cap, seq), 0).astype(F32)
    onehot = jnp.where(srow == ci, 1.0, 0.0).astype(BF16)
    xs_ref[...] = _dot(onehot, h_ref[...]).astype(BF16)


def _moe_gather(slot_t, h2, cap):
    b, seq, _ = h2.shape
    return pl.pallas_call(
        _moe_gather_kernel,
        grid=(b, N_EXPERTS),
        in_specs=[
            pl.BlockSpec((None, 8, seq), lambda i, e: (i, e // 8, 0)),
            pl.BlockSpec((None, seq, D_MODEL), lambda i, e: (i, 0, 0)),
        ],
        out_specs=pl.BlockSpec((None, None, cap, D_MODEL), lambda i, e: (i, e, 0, 0)),
        out_shape=jax.ShapeDtypeStruct((b, N_EXPERTS, cap, D_MODEL), BF16),
        compiler_params=_params(("parallel", "parallel")),
        name="moe_gather",
    )(slot_t, h2)


def _moe_ffn_kernel(xs_ref, w1_ref, w3_ref, w2_ref, y_ref, act_ref, w1b_ref, w3b_ref, w2b_ref):
    s = pl.program_id(1)
    nb, cap, _ = xs_ref.shape
    nf = act_ref.shape[0]
    per = FFN_ROW_TILE // cap
    row_tiles = nb // per

    @pl.when(s < nf)
    def _():
        w1b_ref[...] = w1_ref[...].astype(BF16)
        w3b_ref[...] = w3_ref[...].astype(BF16)
        for r in range(row_tiles):
            x = xs_ref[r * per:(r + 1) * per].reshape(FFN_ROW_TILE, D_MODEL)
            up = _dot(x, w1b_ref[...])
            gt = _dot(x, w3b_ref[...])
            act_ref[s, r * FFN_ROW_TILE:(r + 1) * FFN_ROW_TILE, :] = (_silu(up) * gt).astype(BF16)

    @pl.when(s >= nf)
    def _():
        w2b_ref[...] = w2_ref[...].astype(BF16)
        for r in range(row_tiles):
            rows = slice(r * FFN_ROW_TILE, (r + 1) * FFN_ROW_TILE)
            act = jnp.concatenate([act_ref[f, rows, :] for f in range(nf)], axis=1)
            y = _dot(act, w2b_ref[...])
            y_ref[r * per:(r + 1) * per] = y.astype(BF16).reshape(per, cap, y.shape[1])


def _moe_ffn(xs, w1, w3, w2):
    b, ne, cap, _ = xs.shape
    nf = D_EXPERT // FFN_F_TILE
    nn = D_MODEL // FFN_N_TILE
    hidden = lambda e, s: (e, 0, jnp.minimum(s, nf - 1))
    out_col = lambda e, s: jnp.maximum(s - nf, 0)
    return pl.pallas_call(
        _moe_ffn_kernel,
        grid=(ne, nf + nn),
        in_specs=[
            pl.BlockSpec((b, None, cap, D_MODEL), lambda e, s: (0, e, 0, 0)),
            pl.BlockSpec((None, D_MODEL, FFN_F_TILE), hidden),
            pl.BlockSpec((None, D_MODEL, FFN_F_TILE), hidden),
            pl.BlockSpec((None, D_EXPERT, FFN_N_TILE), lambda e, s: (e, 0, out_col(e, s))),
        ],
        out_specs=pl.BlockSpec((b, None, cap, FFN_N_TILE), lambda e, s: (0, e, 0, out_col(e, s))),
        out_shape=jax.ShapeDtypeStruct(xs.shape, BF16),
        scratch_shapes=[
            pltpu.VMEM((nf, b * cap, FFN_F_TILE), BF16),
            pltpu.VMEM((D_MODEL, FFN_F_TILE), BF16),
            pltpu.VMEM((D_MODEL, FFN_F_TILE), BF16),
            pltpu.VMEM((D_EXPERT, FFN_N_TILE), BF16),
        ],
        compiler_params=_params(("parallel", "arbitrary")),
        name="moe_ffn",
    )(xs, w1, w3, w2)


def _moe_scatter_kernel(slot_ref, aff_ref, y_ref, x2_ref, g_ref, o_ref):
    slot = slot_ref[...]
    aff = aff_ref[...]
    rows, cap = slot.shape[0], y_ref.shape[1]
    ci = lax.broadcasted_iota(jnp.int32, (rows, cap), 1).astype(F32)
    acc = x2_ref[...]
    for e in range(N_EXPERTS):
        onehot = jnp.where(slot[:, e:e + 1] == ci, 1.0, 0.0).astype(BF16)
        acc = acc + aff[:, e:e + 1] * _dot(onehot, y_ref[e])
    o_ref[...] = _rms(acc, g_ref[...])


def _moe_scatter(slot, aff, y, x2, norm_g):
    b, seq, _ = x2.shape
    cap = y.shape[2]
    tile = lambda i, r: (i, r, 0)
    return pl.pallas_call(
        _moe_scatter_kernel,
        grid=(b, seq // ROW_TILE),
        in_specs=[
            pl.BlockSpec((None, ROW_TILE, LANES), tile),
            pl.BlockSpec((None, ROW_TILE, LANES), tile),
            pl.BlockSpec((None, N_EXPERTS, cap, D_MODEL), lambda i, r: (i, 0, 0, 0)),
            pl.BlockSpec((None, ROW_TILE, D_MODEL), tile),
            pl.BlockSpec((1, D_MODEL), lambda i, r: (0, 0)),
        ],
        out_specs=pl.BlockSpec((None, ROW_TILE, D_MODEL), tile),
        out_shape=jax.ShapeDtypeStruct((b, seq, D_MODEL), F32),
        compiler_params=_params(("parallel", "parallel")),
        name="moe_scatter",
    )(slot, aff, y, x2, norm_g)


def kernel(x, norm1_g, w_in, conv_w, conv_b, wq_m, wk_m, wv_m, w_if_fwd, b_if_fwd,
           w_if_bwd, b_if_bwd, mlstm_norm_g, mlstm_skip, attn_norm_g, w_out, norm2_g,
           w_router, w1, w3, w2, norm_f_g):
    b, seq, _ = x.shape
    assert w_in.shape[0] == 1, "single-layer problem"
    assert seq % ROW_TILE == 0 and seq % MLSTM_CHUNK == 0 and seq % ATTN_Q_TILE == 0
    cap = EC_CAPACITY * seq // N_EXPERTS
    assert FFN_ROW_TILE % cap == 0 and (b * cap) % FFN_ROW_TILE == 0
    l = 0
    x2d = x.reshape(b * seq, D_MODEL)
    xm, z, *qkv_views = _in_proj(x2d, norm1_g[l][None, :], w_in[l].astype(BF16), seq)
    shp = lambda t: t.reshape(b, seq, t.shape[-1])
    wif_rows, bif_rows = _gate_rows(w_if_fwd[l], b_if_fwd[l], w_if_bwd[l], b_if_bwd[l])
    xc, qm, ktm, vm, gates = _mlstm_pre(
        shp(xm), conv_w[l], conv_b[l][None, :],
        _block_diag(wq_m[l]).astype(BF16), _block_diag(wk_m[l]).T.astype(BF16),
        _block_diag(wv_m[l]).astype(BF16), wif_rows.astype(BF16), bif_rows)
    ym = _mlstm(qm, ktm, vm, gates, xc, shp(z), mlstm_norm_g[l][None, :],
                mlstm_skip[l][None, :])
    branches = [_band_attn(*qkv_views[3 * d:3 * d + 3], seq, win, dil)
                for d, (win, dil) in enumerate(DILATED_PATTERNS)]
    wr_pad = jnp.pad(w_router[l], ((0, 0), (0, LANES - N_EXPERTS)))
    x2, h2, logits = _out_proj(
        ym.reshape(b * seq, D_MLSTM), [o for o, _ in branches], [s for _, s in branches], x2d,
        attn_norm_g[l][None, :], w_out[l].astype(BF16), norm2_g[l][None, :], wr_pad)
    slot, slot_t, aff = _route(logits.reshape(b, seq, LANES), cap)
    xs = _moe_gather(slot_t, h2.reshape(b, seq, D_MODEL), cap)
    y = _moe_ffn(xs, w1[l], w3[l], w2[l])
    return _moe_scatter(slot, aff, y, x2.reshape(b, seq, D_MODEL), norm_f_g[None, :])
```

```python
import functools

import jax
import jax.numpy as jnp
from jax import lax
from jax.experimental import pallas as pl
from jax.experimental.pallas import tpu as pltpu

F32 = jnp.float32
BF16 = jnp.bfloat16

D_MODEL = 1024
D_MLSTM = 512
D_ATTN = 512
D_IN_PROJ = 2 * D_MLSTM + 3 * D_ATTN
MLSTM_HEADS = 4
MLSTM_HEAD_DIM = 128
MLSTM_QKV_BLOCK = 4
MLSTM_CONV = 5
ATTN_HEAD_DIM = 64
ROPE_DIM = 16
ROPE_THETA = 500000.0
DILATED_PATTERNS = ((128, 1), (512, 4), (2048, 16))
N_EXPERTS = 16
EC_CAPACITY = 2
D_EXPERT = 2816
NORM_EPS = 1e-6
NEG_INF = -1e30

LANES = 128
MLSTM_CHUNK = 128
ROW_TILE = 512
OUT_PROJ_SPLITS = 2
ROUTE_BATCHES = 2
ATTN_Q_TILE = 128
FFN_F_TILE = 256
FFN_N_TILE = 256
FFN_ROW_TILE = 512
THRESHOLD_RADIX_BITS = (4, 4, 4, 4, 4, 3, 4, 4, 4, 4)
V7X_VMEM_BYTES = 64 * 1024 * 1024
VMEM_LIMIT = V7X_VMEM_BYTES * 7 // 8


def _params(sem):
    return pltpu.CompilerParams(dimension_semantics=sem, vmem_limit_bytes=VMEM_LIMIT)


def _rms(x, g):
    return x * lax.rsqrt(jnp.mean(x * x, axis=-1, keepdims=True) + NORM_EPS) * g


def _silu(x):
    return x * (1.0 / (1.0 + jnp.exp(-x)))


def _dot(a, b):
    return jnp.dot(a, b, preferred_element_type=F32)


def _dot_nt(a, b):
    return lax.dot_general(a, b, (((1,), (1,)), ((), ())), preferred_element_type=F32)


def _in_proj_kernel(x_ref, g_ref, w_ref, cos_ref, sa_ref, sb_ref, xm_ref, z_ref, *rest):
    qkv_refs, scr_ref, cls_ref = rest[:-2], rest[-2], rest[-1]
    h = _rms(x_ref[...], g_ref[...])
    p = _dot(h.astype(BF16), w_ref[...])
    xm_ref[...] = p[:, :D_MLSTM]
    z_ref[...] = p[:, D_MLSTM:2 * D_MLSTM]
    cos, sa, sb = cos_ref[...], sa_ref[...], sb_ref[...]
    half = ROPE_DIM // 2

    def rope(t):
        outs = []
        for j in range(D_ATTN // LANES):
            tj = t[:, j * LANES:(j + 1) * LANES]
            up = pltpu.roll(tj, LANES - half, axis=1)
            dn = pltpu.roll(tj, half, axis=1)
            outs.append(tj * cos + up * sa + dn * sb)
        return jnp.concatenate(outs, axis=1)

    o = 2 * D_MLSTM
    qkv = (rope(p[:, o:o + D_ATTN]) * (ATTN_HEAD_DIM ** -0.5),
           rope(p[:, o + D_ATTN:o + 2 * D_ATTN]),
           p[:, o + 2 * D_ATTN:])
    rows = p.shape[0]
    groups = D_ATTN // LANES
    for a, val in enumerate(qkv):
        for j in range(groups):
            scr_ref[j] = val[:, j * LANES:(j + 1) * LANES]
        prev = 1
        for d, (_, dil) in enumerate(DILATED_PATTERNS):
            ref = qkv_refs[3 * d + a]
            if dil == 1:
                ref[...] = val.astype(BF16)
                continue
            step = dil // prev
            keep = d + 1 < len(DILATED_PATTERNS)
            for r in range(dil):
                for j in range(groups):
                    src = scr_ref.at[j] if prev == 1 else cls_ref.at[r % prev, j]
                    piece = src[pl.ds(r // prev, rows // dil, stride=step), :]
                    ref[:, r * D_ATTN + j * LANES:r * D_ATTN + (j + 1) * LANES] = piece.astype(BF16)
                    if keep:
                        cls_ref[r, j] = piece
            prev = dil


def _rope_tables(seq):
    half = ROPE_DIM // 2
    inv_freq = ROPE_THETA ** (-2.0 * jnp.arange(half, dtype=F32) / ROPE_DIM)
    ang = jnp.arange(seq).astype(F32)[:, None] * inv_freq[None, :]
    cos, sin = jnp.cos(ang), jnp.sin(ang)
    pad = jnp.zeros((seq, ATTN_HEAD_DIM - ROPE_DIM), F32)
    cos_h = jnp.concatenate([cos, cos, pad + 1.0], axis=1)
    sa_h = jnp.concatenate([-sin, jnp.zeros_like(sin), pad], axis=1)
    sb_h = jnp.concatenate([jnp.zeros_like(sin), sin, pad], axis=1)
    rep = LANES // ATTN_HEAD_DIM
    return tuple(jnp.tile(t, (1, rep)) for t in (cos_h, sa_h, sb_h))


def _in_proj(x2d, g, w_bf, seq):
    n = x2d.shape[0]
    tiles_per_seq = seq // ROW_TILE
    cos, sa, sb = _rope_tables(seq)
    mid_dil = sorted(dil for _, dil in DILATED_PATTERNS)[-2]
    row = lambda i: (i, 0)
    fixed = lambda i: (0, 0)
    pos = lambda i: (i % tiles_per_seq, 0)
    return pl.pallas_call(
        _in_proj_kernel,
        grid=(n // ROW_TILE,),
        in_specs=[
            pl.BlockSpec((ROW_TILE, D_MODEL), row),
            pl.BlockSpec((1, D_MODEL), fixed),
            pl.BlockSpec((D_MODEL, D_IN_PROJ), fixed),
            pl.BlockSpec((ROW_TILE, LANES), pos),
            pl.BlockSpec((ROW_TILE, LANES), pos),
            pl.BlockSpec((ROW_TILE, LANES), pos),
        ],
        out_specs=[
            pl.BlockSpec((ROW_TILE, D_MLSTM), row),
            pl.BlockSpec((ROW_TILE, D_MLSTM), row),
        ] + [pl.BlockSpec((ROW_TILE // dil, dil * D_ATTN), row)
             for _, dil in DILATED_PATTERNS for _ in range(3)],
        out_shape=[
            jax.ShapeDtypeStruct((n, D_MLSTM), F32),
            jax.ShapeDtypeStruct((n, D_MLSTM), F32),
        ] + [jax.ShapeDtypeStruct((n // dil, dil * D_ATTN), BF16)
             for _, dil in DILATED_PATTERNS for _ in range(3)],
        scratch_shapes=[pltpu.VMEM((D_ATTN // LANES, ROW_TILE, LANES), F32),
                        pltpu.VMEM((mid_dil, D_ATTN // LANES, ROW_TILE // mid_dil, LANES), F32)],
        compiler_params=_params(("parallel",)),
        name="in_proj",
    )(x2d, g, w_bf, cos, sa, sb)


def _mlstm_pre_kernel(xm_ref, cw_ref, cb_ref, wq_ref, wkt_ref, wv_ref, wif_ref, bif_ref,
                      xc_ref, q_ref, kt_ref, v_ref, g_ref):
    x = xm_ref[...]
    seq = x.shape[0]
    t = lax.broadcasted_iota(jnp.int32, x.shape, 0)
    acc = jnp.zeros_like(x) + cb_ref[...]
    for j in range(MLSTM_CONV):
        d = j - MLSTM_CONV // 2
        if d == 0:
            tap = x
        else:
            tap = pltpu.roll(x, (-d) % seq, axis=0)
            tap = jnp.where((t + d >= 0) & (t + d < seq), tap, 0.0)
        acc = acc + tap * cw_ref[j:j + 1, :]
    xc = _silu(acc)
    xc_ref[...] = xc
    xcb = xc.astype(BF16)
    q = _dot(xcb, wq_ref[...]).astype(BF16)
    kt = (_dot_nt(wkt_ref[...], xcb) * (MLSTM_HEAD_DIM ** -0.5)).astype(BF16)
    v = _dot(x.astype(BF16), wv_ref[...]).astype(BF16)
    q_ref[...] = q
    v_ref[...] = v
    L = MLSTM_CHUNK
    for c in range(seq // L):
        kt_ref[c] = kt[:, c * L:(c + 1) * L]
    wif = wif_ref[...]
    g = (_dot_nt(wif[:, :D_MLSTM], q) + _dot(wif[:, D_MLSTM:2 * D_MLSTM], kt)
         + _dot_nt(wif[:, 2 * D_MLSTM:], v) + bif_ref[...])
    for kind in range(2):
        for hd in range(MLSTM_HEADS):
            r0 = (kind * MLSTM_HEADS + hd) * 8
            for c in range(seq // L):
                g_ref[hd, kind, c * 8:(c + 1) * 8, :] = g[r0:r0 + 8, c * L:(c + 1) * L]


def _block_diag(w):
    nblk = w.shape[0]
    n = nblk * MLSTM_QKV_BLOCK
    tiled = jnp.tile(w.reshape(n, MLSTM_QKV_BLOCK), (1, nblk))
    blk = jnp.arange(n) // MLSTM_QKV_BLOCK
    return jnp.where(blk[:, None] == blk[None, :], tiled, 0.0)


def _gate_rows(w_f, b_f, w_b, b_b):
    h = MLSTM_HEADS

    def rows(f, b):
        pair = jnp.stack([f, b], axis=-1)
        pair = jnp.concatenate([pair[..., h:, :], pair[..., :h, :]], axis=-2)
        pair = jnp.pad(pair, [(0, 0)] * (pair.ndim - 1) + [(0, 6)])
        return pair.reshape(*pair.shape[:-2], 2 * h * 8)

    return rows(w_f, w_b).T, rows(b_f, b_b)[:, None]


def _mlstm_pre(xm, conv_w, conv_b, wq, wkt, wv, wif_rows, bif_rows):
    b, seq, _ = xm.shape
    nrow = wif_rows.shape[0]
    nc = seq // MLSTM_CHUNK
    per_b = lambda i: (i, 0, 0)
    fixed = lambda i: (0, 0)
    return pl.pallas_call(
        _mlstm_pre_kernel,
        grid=(b,),
        in_specs=[
            pl.BlockSpec((None, seq, D_MLSTM), per_b),
            pl.BlockSpec((MLSTM_CONV, D_MLSTM), fixed),
            pl.BlockSpec((1, D_MLSTM), fixed),
            pl.BlockSpec((D_MLSTM, D_MLSTM), fixed),
            pl.BlockSpec((D_MLSTM, D_MLSTM), fixed),
            pl.BlockSpec((D_MLSTM, D_MLSTM), fixed),
            pl.BlockSpec((nrow, 3 * D_MLSTM), fixed),
            pl.BlockSpec((nrow, 1), fixed),
        ],
        out_specs=[
            pl.BlockSpec((None, seq, D_MLSTM), per_b),
            pl.BlockSpec((None, seq, D_MLSTM), per_b),
            pl.BlockSpec((None, nc, D_MLSTM, MLSTM_CHUNK), lambda i: (i, 0, 0, 0)),
            pl.BlockSpec((None, seq, D_MLSTM), per_b),
            pl.BlockSpec((None, MLSTM_HEADS, 2, nc * 8, MLSTM_CHUNK), lambda i: (i, 0, 0, 0, 0)),
        ],
        out_shape=[
            jax.ShapeDtypeStruct((b, seq, D_MLSTM), F32),
            jax.ShapeDtypeStruct((b, seq, D_MLSTM), BF16),
            jax.ShapeDtypeStruct((b, nc, D_MLSTM, MLSTM_CHUNK), BF16),
            jax.ShapeDtypeStruct((b, seq, D_MLSTM), BF16),
            jax.ShapeDtypeStruct((b, MLSTM_HEADS, 2, nc * 8, MLSTM_CHUNK), F32),
        ],
        compiler_params=_params(("parallel",)),
        name="mlstm_pre",
    )(xm, conv_w, conv_b, wq, wkt, wv, wif_rows, bif_rows)


def _log_sigmoid(x):
    return jnp.minimum(x, 0.0) - jnp.log1p(jnp.exp(-jnp.abs(x)))


def _split3(x):
    hi = x.astype(BF16).astype(F32)
    mid = (x - hi).astype(BF16).astype(F32)
    lo = (x - hi - mid).astype(BF16).astype(F32)
    return hi, mid, lo


def _mlstm_kernel(q_ref, kt_ref, v_ref, g_ref, xc_ref, z_ref, ng_ref, sk_ref, o_ref,
                  w_ref, ml_ref, tot_ref, pm_ref, tb_ref, tr_ref, dc_ref, cs_ref, ms_ref):
    L = MLSTM_CHUNK
    dh = MLSTM_HEAD_DIM
    nc = kt_ref.shape[0]
    rows_all = g_ref.shape[1]
    lane = lax.broadcasted_iota(jnp.int32, (rows_all, L), 1)
    sub = lax.broadcasted_iota(jnp.int32, (rows_all, L), 0) % 8
    fwd_row = sub == 0
    row_i = lax.broadcasted_iota(jnp.int32, (L, L), 0)
    col_i = lax.broadcasted_iota(jnp.int32, (L, L), 1)
    ones_col = jnp.ones((L, dh), BF16)

    lf = _log_sigmoid(g_ref[0])
    pre, suf = lf, lf
    d = 1
    while d < L:
        pre = pre + jnp.where(lane >= d, pltpu.roll(pre, d, axis=1), 0.0)
        suf = suf + jnp.where(lane < L - d, pltpu.roll(suf, L - d, axis=1), 0.0)
        d *= 2
    cum = jnp.where(fwd_row, pre, suf)
    tot = jnp.where(fwd_row, cum[:, L - 1:L], cum[:, 0:1])
    a = tot - cum + g_ref[1]
    ml = jnp.max(a, axis=1, keepdims=True)
    w_ref[...] = jnp.exp(a - ml)
    ml_ref[...] = jnp.broadcast_to(ml, (rows_all, L))
    tot_ref[...] = tot
    r = g_ref[1] - cum
    pmax, smax = r, r
    d = 1
    while d < L:
        pmax = jnp.maximum(pmax, jnp.where(lane >= d, pltpu.roll(pmax, d, axis=1), NEG_INF))
        smax = jnp.maximum(smax, jnp.where(lane < L - d, pltpu.roll(smax, L - d, axis=1), NEG_INF))
        d *= 2
    pm_ref[...] = jnp.where(fwd_row, pmax, smax)

    def tile_bcast(x, src):
        y = jnp.where(sub == src, x, 0.0)
        if src:
            y = pltpu.roll(y, rows_all - src, axis=0)
        for s in (1, 2, 4):
            y = y + pltpu.roll(y, s, axis=0)
        return y

    for dr in range(2):
        b_hi, b_mid, b_lo = _split3(tile_bcast(cum, dr))
        r_hi, r_mid, r_lo = _split3(tile_bcast(r, dr))
        tb_ref[dr] = jnp.where(sub == 1, b_hi, jnp.where(sub == 2, b_mid, jnp.where(
            sub == 3, b_lo, jnp.where((sub >= 4) & (sub <= 6), 1.0, 0.0))))
        tr_ref[dr] = jnp.where(sub == 0, 1.0, jnp.where(sub == 4, r_hi, jnp.where(
            sub == 5, r_mid, jnp.where(sub == 6, r_lo, 0.0))))

    def chunk_rows(ref, c):
        return ref[pl.ds(pl.multiple_of(c * 8, 8), 8), :]

    def v_aug(c):
        return jnp.concatenate([v_ref[pl.ds(c * L, L), :], ones_col], axis=1)

    def phase_a(c, carry):
        kt = kt_ref[c].astype(F32)
        w = chunk_rows(w_ref, c)
        kw = jnp.concatenate([kt * w[0:1], kt * w[1:2]], axis=0)
        dc_ref[c] = _dot(kw.astype(BF16), v_aug(c))
        return carry

    lax.fori_loop(0, nc, phase_a, 0, unroll=8)

    ms_ref[...] = jnp.zeros(ms_ref.shape, F32)
    def scan_dir(direction_row, reverse):
        off_rows = direction_row * dh
        off_cols = direction_row * 2 * dh

        def body(i, carry):
            c = (nc - 1 - i) if reverse else i
            state, m = carry
            cs_ref[c, :, off_cols:off_cols + 2 * dh] = state.astype(BF16)
            ms_ref[c, direction_row:direction_row + 1, :] = m
            ml_c = chunk_rows(ml_ref, c)[direction_row:direction_row + 1]
            g_c = chunk_rows(tot_ref, c)[direction_row:direction_row + 1]
            m_new = jnp.maximum(g_c + m, ml_c)
            alpha = jnp.exp(g_c + m - m_new)
            beta = jnp.exp(ml_c - m_new)
            alpha2 = jnp.concatenate([alpha, alpha], axis=1)
            beta2 = jnp.concatenate([beta, beta], axis=1)
            state = alpha2 * state + beta2 * dc_ref[c, off_rows:off_rows + dh, :]
            return state, m_new

        init = (jnp.zeros((dh, 2 * dh), F32), jnp.zeros((1, L), F32))
        lax.fori_loop(0, nc, body, init, unroll=16)

    scan_dir(0, False)
    scan_dir(1, True)

    ng = ng_ref[...]
    sk = sk_ref[...]

    sub8 = lax.broadcasted_iota(jnp.int32, (8, L), 0)
    floor_rows = jnp.where(sub8 == 0, 1.0, jnp.where(sub8 <= 3, -1.0, 0.0))
    no_rows = jnp.zeros((8, 3 * L), F32)

    def direction(s_qk, qc, vaug, ex, keep):
        w = jnp.exp(jnp.where(keep, ex[:, :L], NEG_INF)) * s_qk
        scale = jnp.exp(ex[:, L:2 * L])
        intra = _dot(w.astype(BF16), vaug)
        tot_c = intra + jnp.concatenate([scale, scale], axis=1) * qc
        return tot_c[:, :dh] / jnp.maximum(jnp.abs(tot_c[:, dh:]), jnp.exp(ex[:, 2 * L:]))

    def phase_c(c, carry):
        rows = pl.ds(c * L, L)
        q = q_ref[rows, :]
        s_qk = _dot(q, kt_ref[c])
        qc = _dot(q, cs_ref[c])
        vaug = v_aug(c)
        ms = ms_ref[c]
        e = -jnp.maximum(ms, chunk_rows(pm_ref, c))
        e = (e - jnp.abs(e) * (2.0 ** -7)).astype(BF16).astype(F32)
        lhs_tiles, rhs_tiles = [], []
        for dr in range(2):
            e_rows = jnp.broadcast_to(e[dr:dr + 1], (8, L))
            lhs_tiles.append(jnp.where(sub8 == 0, e_rows, chunk_rows(tb_ref.at[dr], c)))
            m_hi, m_mid, m_lo = _split3(jnp.broadcast_to(ms[dr:dr + 1], (8, L)))
            scale_rows = jnp.where(sub8 == 0, 1.0, jnp.where(sub8 == 4, m_hi, jnp.where(
                sub8 == 5, m_mid, jnp.where(sub8 == 6, m_lo, 0.0))))
            rhs_tiles.append(jnp.concatenate(
                [chunk_rows(tr_ref.at[dr], c), scale_rows, floor_rows], axis=1))
        lhs = jnp.concatenate(lhs_tiles, axis=0).astype(BF16)
        rhs = jnp.concatenate([jnp.concatenate([rhs_tiles[0], no_rows], axis=1),
                               jnp.concatenate([no_rows, rhs_tiles[1]], axis=1)],
                              axis=0).astype(BF16)
        ex = lax.dot_general(lhs, rhs, (((0,), (0,)), ((), ())),
                             preferred_element_type=F32)
        h = (direction(s_qk, qc[:, :2 * dh], vaug, ex[:, :3 * L], col_i <= row_i)
             + direction(s_qk, qc[:, 2 * dh:], vaug, ex[:, 3 * L:], col_i >= row_i))
        hn = _rms(h, ng)
        o_ref[rows, :] = (hn + sk * xc_ref[rows, :]) * _silu(z_ref[rows, :])
        return carry

    lax.fori_loop(0, nc, phase_c, 0, unroll=16)


def _mlstm(q, kt, v, gates, xc, z, norm_g, skip):
    b, seq, _ = q.shape
    nc = seq // MLSTM_CHUNK
    dh = MLSTM_HEAD_DIM
    head = lambda i, j: (i, 0, j)
    vec = lambda i, j: (0, j)
    blk = pl.BlockSpec((None, seq, dh), head)
    gate_rows = pltpu.VMEM((nc * 8, MLSTM_CHUNK), F32)
    gate_tiles = pltpu.VMEM((2, nc * 8, MLSTM_CHUNK), F32)
    return pl.pallas_call(
        _mlstm_kernel,
        grid=(b, MLSTM_HEADS),
        in_specs=[
            blk,
            pl.BlockSpec((None, nc, dh, MLSTM_CHUNK), lambda i, j: (i, 0, j, 0)),
            blk,
            pl.BlockSpec((None, None, 2, nc * 8, MLSTM_CHUNK), lambda i, j: (i, j, 0, 0, 0)),
            blk, blk,
            pl.BlockSpec((1, dh), vec),
            pl.BlockSpec((1, dh), vec),
        ],
        out_specs=blk,
        out_shape=jax.ShapeDtypeStruct((b, seq, D_MLSTM), F32),
        scratch_shapes=[
            gate_rows, gate_rows, gate_rows, gate_rows, gate_tiles, gate_tiles,
            pltpu.VMEM((nc, 2 * dh, 2 * dh), F32),
            pltpu.VMEM((nc, dh, 4 * dh), BF16),
            pltpu.VMEM((nc, 8, MLSTM_CHUNK), F32),
        ],
        compiler_params=_params(("parallel", "parallel")),
        name="mlstm",
    )(q, kt, v, gates, xc, z, norm_g, skip)


def _band_attn_kernel(q_ref, k_ref, v_ref, o_ref, lse_ref, *, half, dil):
    lsub = q_ref.shape[0]
    tq = ATTN_Q_TILE
    win = min(lsub, 2 * tq)
    first = lax.broadcasted_iota(jnp.int32, (1, LANES), 1) < ATTN_HEAD_DIM
    lane = lax.broadcasted_iota(jnp.int32, (tq, LANES), 1)
    rel = (lax.broadcasted_iota(jnp.int32, (2 * tq, win), 1)
           - lax.broadcasted_iota(jnp.int32, (2 * tq, win), 0) % tq)

    def tile(qs, ws, out_rows):
        keep = jnp.abs(rel + (ws - qs)) <= half
        for c in range(dil):
            lse_tile = jnp.zeros((tq, LANES), F32)
            for p in range(D_ATTN // LANES):
                lanes = slice(c * D_ATTN + p * LANES, c * D_ATTN + (p + 1) * LANES)
                q = q_ref[pl.ds(qs, tq), lanes]
                kw = k_ref[pl.ds(ws, win), lanes]
                vw = v_ref[pl.ds(ws, win), lanes]
                zero = jnp.zeros_like(q)
                q2 = jnp.concatenate([jnp.where(first, q, zero), jnp.where(first, zero, q)], axis=0)
                s = jnp.where(keep, _dot_nt(q2, kw), NEG_INF)
                m = jnp.max(s, axis=1, keepdims=True)
                e = jnp.exp(s - m)
                l = jnp.sum(e, axis=1, keepdims=True)
                o2 = _dot(e.astype(BF16), vw) / l
                o_ref[p, out_rows(c), :] = jnp.where(first, o2[:tq], o2[tq:])
                lse2 = m + jnp.log(l)
                lse_tile = jnp.where(lane == 2 * p, lse2[:tq],
                                     jnp.where(lane == 2 * p + 1, lse2[tq:], lse_tile))
            lse_ref[out_rows(c), :] = lse_tile

    if dil == 1:
        def body(t, carry):
            qs = pl.multiple_of(t * tq, tq)
            ws = pl.multiple_of(jnp.clip(qs - half, 0, lsub - win), half)
            tile(qs, ws, lambda c: pl.ds(qs, tq))
            return carry

        lax.fori_loop(0, lsub // tq, body, 0, unroll=8)
    else:
        for t in range(lsub // tq):
            qs = t * tq
            ws = min(max(qs - half, 0), lsub - win)
            tile(qs, ws, lambda c, qs=qs: pl.ds(qs * dil + c, tq, stride=dil))


def _band_attn(q, k, v, seq, win, dil):
    lsub = seq // dil
    b = q.shape[0] // lsub
    half = win // (2 * dil)
    pairs = D_ATTN // LANES
    assert lsub % ATTN_Q_TILE == 0 and half % 16 == 0 and ATTN_Q_TILE + 2 * half <= 2 * ATTN_Q_TILE
    blk = pl.BlockSpec((lsub, dil * D_ATTN), lambda i: (i, 0))
    return pl.pallas_call(
        functools.partial(_band_attn_kernel, half=half, dil=dil),
        grid=(b,),
        in_specs=[blk, blk, blk],
        out_specs=[pl.BlockSpec((None, pairs, seq, LANES), lambda i: (i, 0, 0, 0)),
                   pl.BlockSpec((seq, LANES), lambda i: (i, 0))],
        out_shape=[
            jax.ShapeDtypeStruct((b, pairs, seq, LANES), F32),
            jax.ShapeDtypeStruct((b * seq, LANES), F32),
        ],
        compiler_params=_params(("parallel",)),
        name=f"band_attn_d{dil}",
    )(q, k, v)


def _out_proj_kernel(ym_ref, o1_ref, o2_ref, o3_ref, l1_ref, l2_ref, l3_ref, sp_ref, x_ref, ag_ref,
                     w_ref, n2_ref, wr_ref, x2_ref, h2_ref, lg_ref):
    spread = sp_ref[...]
    wr = wr_ref[...]
    w_hi = wr.astype(BF16)
    w_lo = (wr - w_hi.astype(F32)).astype(BF16)
    wr3 = jnp.concatenate([w_hi, w_hi, w_lo], axis=0)

    def per_lane(w):
        hi = w.astype(BF16)
        lo = (w - hi.astype(F32)).astype(BF16)
        return _dot(jnp.concatenate([hi, lo], axis=1), spread)

    rows = x_ref.shape[0] // OUT_PROJ_SPLITS
    for part in range(OUT_PROJ_SPLITS):
        rs = slice(part * rows, (part + 1) * rows)
        lses = [r[rs, :] for r in (l1_ref, l2_ref, l3_ref)]
        top = jnp.maximum(jnp.maximum(lses[0], lses[1]), lses[2])
        wts = [jnp.exp(l - top) for l in lses]
        total = wts[0] + wts[1] + wts[2]

        def heads(o_ref):
            return jnp.concatenate([o_ref[p, rs, :] for p in range(o_ref.shape[0])], axis=1)

        ya = sum(per_lane(w / total) * heads(o) for w, o in zip(wts, (o1_ref, o2_ref, o3_ref)))
        ya = _rms(ya, ag_ref[...])
        mixed = jnp.concatenate([ym_ref[rs, :], ya], axis=1).astype(BF16)
        x2 = x_ref[rs, :] + _dot(mixed, w_ref[...])
        x2_ref[rs, :] = x2
        h2 = _rms(x2, n2_ref[...])
        hi = h2.astype(BF16)
        h2_ref[rs, :] = hi
        lo = (h2 - hi.astype(F32)).astype(BF16)
        lg_ref[rs, :] = _dot(jnp.concatenate([hi, lo, hi], axis=1), wr3)


def _out_proj(ym, branch_o, branch_lse, x2d, attn_g, w_bf, n2g, wr_pad):
    n = x2d.shape[0]
    spread = (jnp.arange(LANES)[:, None] == jnp.arange(D_ATTN)[None, :] // ATTN_HEAD_DIM)
    spread = jnp.tile(spread.astype(BF16), (2, 1))
    pairs, seq = branch_o[0].shape[1:3]
    tiles_per_seq = seq // ROW_TILE
    branch = pl.BlockSpec((None, pairs, ROW_TILE, LANES),
                          lambda i: (i // tiles_per_seq, 0, i % tiles_per_seq, 0))
    row = lambda i: (i, 0)
    fixed = lambda i: (0, 0)
    return pl.pallas_call(
        _out_proj_kernel,
        grid=(n // ROW_TILE,),
        in_specs=[
            pl.BlockSpec((ROW_TILE, D_MLSTM), row),
            branch, branch, branch,
            pl.BlockSpec((ROW_TILE, LANES), row),
            pl.BlockSpec((ROW_TILE, LANES), row),
            pl.BlockSpec((ROW_TILE, LANES), row),
            pl.BlockSpec((2 * LANES, D_ATTN), fixed),
            pl.BlockSpec((ROW_TILE, D_MODEL), row),
            pl.BlockSpec((1, D_ATTN), fixed),
            pl.BlockSpec((D_MODEL, D_MODEL), fixed),
            pl.BlockSpec((1, D_MODEL), fixed),
            pl.BlockSpec((D_MODEL, LANES), fixed),
        ],
        out_specs=[
            pl.BlockSpec((ROW_TILE, D_MODEL), row),
            pl.BlockSpec((ROW_TILE, D_MODEL), row),
            pl.BlockSpec((ROW_TILE, LANES), row),
        ],
        out_shape=[
            jax.ShapeDtypeStruct((n, D_MODEL), F32),
            jax.ShapeDtypeStruct((n, D_MODEL), BF16),
            jax.ShapeDtypeStruct((n, LANES), F32),
        ],
        compiler_params=_params(("parallel",)),
        name="out_proj",
    )(ym, *branch_o, *branch_lse, spread, x2d, attn_g, w_bf, n2g, wr_pad)


def _route_kernel(lg_ref, tri_ref, eye_ref, slot_ref, slot_t_ref, aff_ref, *, cap):
    for i in range(lg_ref.shape[0]):
        _route_one(lg_ref.at[i], tri_ref, eye_ref, slot_ref.at[i], slot_t_ref.at[i],
                   aff_ref.at[i], cap=cap)


def _route_one(lg_ref, tri_ref, eye_ref, slot_ref, slot_t_ref, aff_ref, *, cap):
    lg = lg_ref[...]
    valid = lax.broadcasted_iota(jnp.int32, (1, LANES), 1) < N_EXPERTS
    lg = jnp.where(valid, lg, NEG_INF)
    e = jnp.exp(lg - jnp.max(lg, axis=1, keepdims=True))
    aff = e / jnp.sum(e, axis=1, keepdims=True)
    aff_ref[...] = aff
    groups = LANES // N_EXPERTS
    rpg = aff.shape[0] // groups
    lane = lax.broadcasted_iota(jnp.int32, (1, LANES), 1)
    packed = aff[:rpg]
    for g in range(1, groups):
        packed = packed + pltpu.roll(aff[g * rpg:(g + 1) * rpg], g * N_EXPERTS, axis=1)

    def over_groups(x):
        shift = N_EXPERTS
        while shift < LANES:
            x = x + pltpu.roll(x, shift, axis=1)
            shift *= 2
        return x

    def enough(cand):
        part = jnp.sum(jnp.where(packed >= cand, 1.0, 0.0), axis=0, keepdims=True)
        return over_groups(part) >= cap

    def narrow(lo, hi, cands):
        new_lo, new_hi = lo, hi
        for cand in cands:
            ok = enough(cand)
            new_lo = jnp.maximum(new_lo, jnp.where(ok, cand, lo))
            new_hi = jnp.minimum(new_hi, jnp.where(ok, hi, cand))
        return new_lo, new_hi

    tiny = jnp.full((1, LANES), 2.0 ** -126, F32)
    normal = enough(tiny)
    p = tiny
    for span, count in ((16, 7), (1, 15)):
        p, _ = narrow(p, p, [p * (2.0 ** (span * j)) for j in range(1, count + 1)])
    lo = jnp.where(normal, p, 0.0)
    hi = jnp.where(normal, p * 2.0, tiny)
    width = jnp.where(normal, p, 0.0)
    for bits in THRESHOLD_RADIX_BITS:
        width = width * (0.5 ** bits)
        lo, hi = narrow(lo, hi, [lo + j * width for j in range(1, 2 ** bits)])
    gt = jnp.where(packed >= hi, 1.0, 0.0)
    eq = jnp.where(packed >= lo, 1.0, 0.0) - gt
    need = cap - over_groups(jnp.sum(gt, axis=0, keepdims=True))
    tri = tri_ref[...]

    def count_before(x):
        per_group = jnp.sum(x, axis=0, keepdims=True)
        upto = per_group
        shift = N_EXPERTS
        while shift < LANES:
            upto = upto + jnp.where(lane >= shift, pltpu.roll(upto, shift, axis=1), 0.0)
            shift *= 2
        return _dot(tri, x.astype(BF16)) + (upto - per_group)

    sel = gt + eq * jnp.where(count_before(eq) < need, 1.0, 0.0)
    pos = count_before(sel)
    slot_packed = jnp.where(sel > 0.0, pos, -1.0)
    slot = jnp.concatenate(
        [jnp.where(valid, slot_packed if g == 0 else
                   pltpu.roll(slot_packed, LANES - g * N_EXPERTS, axis=1), -1.0)
         for g in range(groups)], axis=0)
    slot_ref[...] = slot
    slot_t_ref[...] = _dot_nt(eye_ref[...], slot.astype(BF16))


def _route(logits, cap):
    b, seq, _ = logits.shape
    assert LANES % N_EXPERTS == 0 and seq % (LANES // N_EXPERTS) == 0
    rpg = seq // (LANES // N_EXPERTS)
    tri = (jnp.arange(rpg)[None, :] < jnp.arange(rpg)[:, None]).astype(BF16)
    eye = jnp.eye(LANES, dtype=BF16)
    per_b = lambda i: (i, 0, 0)
    fixed = lambda i: (0, 0)
    nb = ROUTE_BATCHES if b % ROUTE_BATCHES == 0 else 1
    return pl.pallas_call(
        functools.partial(_route_kernel, cap=cap),
        grid=(b // nb,),
        in_specs=[
            pl.BlockSpec((nb, seq, LANES), per_b),
            pl.BlockSpec((rpg, rpg), fixed),
            pl.BlockSpec((LANES, LANES), fixed),
        ],
        out_specs=[
            pl.BlockSpec((nb, seq, LANES), per_b),
            pl.BlockSpec((nb, LANES, seq), per_b),
            pl.BlockSpec((nb, seq, LANES), per_b),
        ],
        out_shape=[
            jax.ShapeDtypeStruct((b, seq, LANES), F32),
            jax.ShapeDtypeStruct((b, LANES, seq), F32),
            jax.ShapeDtypeStruct((b, seq, LANES), F32),
        ],
        compiler_params=_params(("parallel",)),
        name="route",
    )(logits, tri, eye)


def _moe_gather_kernel(slot_ref, h_ref, xs_ref):
    srow = slot_ref[pl.ds(pl.program_id(1) % 8, 1), :]
    cap, seq = xs_ref.shape[0], srow.shape[1]
    ci = lax.broadcasted_iota(jnp.int32, (cap, seq), 0).astype(F32)
    onehot = jnp.where(srow == ci, 1.0, 0.0).astype(BF16)
    xs_ref[...] = _dot(onehot, h_ref[...]).astype(BF16)


def _moe_gather(slot_t, h2, cap):
    b, seq, _ = h2.shape
    return pl.pallas_call(
        _moe_gather_kernel,
        grid=(b, N_EXPERTS),
        in_specs=[
            pl.BlockSpec((None, 8, seq), lambda i, e: (i, e // 8, 0)),
            pl.BlockSpec((None, seq, D_MODEL), lambda i, e: (i, 0, 0)),
        ],
        out_specs=pl.BlockSpec((None, None, cap, D_MODEL), lambda i, e: (i, e, 0, 0)),
        out_shape=jax.ShapeDtypeStruct((b, N_EXPERTS, cap, D_MODEL), BF16),
        compiler_params=_params(("parallel", "parallel")),
        name="moe_gather",
    )(slot_t, h2)


def _moe_ffn_kernel(xs_ref, w1_ref, w3_ref, w2_ref, y_ref, act_ref, w1b_ref, w3b_ref, w2b_ref):
    s = pl.program_id(1)
    nb, cap, _ = xs_ref.shape
    nf = act_ref.shape[0]
    per = FFN_ROW_TILE // cap
    row_tiles = nb // per

    @pl.when(s < nf)
    def _():
        w1b_ref[...] = w1_ref[...].astype(BF16)
        w3b_ref[...] = w3_ref[...].astype(BF16)
        for r in range(row_tiles):
            x = xs_ref[r * per:(r + 1) * per].reshape(FFN_ROW_TILE, D_MODEL)
            up = _dot(x, w1b_ref[...])
            gt = _dot(x, w3b_ref[...])
            act_ref[s, r * FFN_ROW_TILE:(r + 1) * FFN_ROW_TILE, :] = (_silu(up) * gt).astype(BF16)

    @pl.when(s >= nf)
    def _():
        w2b_ref[...] = w2_ref[...].astype(BF16)
        for r in range(row_tiles):
            rows = slice(r * FFN_ROW_TILE, (r + 1) * FFN_ROW_TILE)
            act = jnp.concatenate([act_ref[f, rows, :] for f in range(nf)], axis=1)
            y = _dot(act, w2b_ref[...])
            y_ref[r * per:(r + 1) * per] = y.astype(BF16).reshape(per, cap, y.shape[1])


def _moe_ffn(xs, w1, w3, w2):
    b, ne, cap, _ = xs.shape
    nf = D_EXPERT // FFN_F_TILE
    nn = D_MODEL // FFN_N_TILE
    hidden = lambda e, s: (e, 0, jnp.minimum(s, nf - 1))
    out_col = lambda e, s: jnp.maximum(s - nf, 0)
    return pl.pallas_call(
        _moe_ffn_kernel,
        grid=(ne, nf + nn),
        in_specs=[
            pl.BlockSpec((b, None, cap, D_MODEL), lambda e, s: (0, e, 0, 0)),
            pl.BlockSpec((None, D_MODEL, FFN_F_TILE), hidden),
            pl.BlockSpec((None, D_MODEL, FFN_F_TILE), hidden),
            pl.BlockSpec((None, D_EXPERT, FFN_N_TILE), lambda e, s: (e, 0, out_col(e, s))),
        ],
        out_specs=pl.BlockSpec((b, None, cap, FFN_N_TILE), lambda e, s: (0, e, 0, out_col(e, s))),
        out_shape=jax.ShapeDtypeStruct(xs.shape, BF16),
        scratch_shapes=[
            pltpu.VMEM((nf, b * cap, FFN_F_TILE), BF16),
            pltpu.VMEM((D_MODEL, FFN_F_TILE), BF16),
            pltpu.VMEM((D_MODEL, FFN_F_TILE), BF16),
            pltpu.VMEM((D_EXPERT, FFN_N_TILE), BF16),
        ],
        compiler_params=_params(("parallel", "arbitrary")),
        name="moe_ffn",
    )(xs, w1, w3, w2)


def _moe_scatter_kernel(slot_ref, aff_ref, y_ref, x2_ref, g_ref, o_ref):
    slot = slot_ref[...]
    aff = aff_ref[...]
    rows, cap = slot.shape[0], y_ref.shape[1]
    ci = lax.broadcasted_iota(jnp.int32, (rows, cap), 1).astype(F32)
    acc = x2_ref[...]
    for e in range(N_EXPERTS):
        onehot = jnp.where(slot[:, e:e + 1] == ci, 1.0, 0.0).astype(BF16)
        acc = acc + aff[:, e:e + 1] * _dot(onehot, y_ref[e])
    o_ref[...] = _rms(acc, g_ref[...])


def _moe_scatter(slot, aff, y, x2, norm_g):
    b, seq, _ = x2.shape
    cap = y.shape[2]
    tile = lambda i, r: (i, r, 0)
    return pl.pallas_call(
        _moe_scatter_kernel,
        grid=(b, seq // ROW_TILE),
        in_specs=[
            pl.BlockSpec((None, ROW_TILE, LANES), tile),
            pl.BlockSpec((None, ROW_TILE, LANES), tile),
            pl.BlockSpec((None, N_EXPERTS, cap, D_MODEL), lambda i, r: (i, 0, 0, 0)),
            pl.BlockSpec((None, ROW_TILE, D_MODEL), tile),
            pl.BlockSpec((1, D_MODEL), lambda i, r: (0, 0)),
        ],
        out_specs=pl.BlockSpec((None, ROW_TILE, D_MODEL), tile),
        out_shape=jax.ShapeDtypeStruct((b, seq, D_MODEL), F32),
        compiler_params=_params(("parallel", "parallel")),
        name="moe_scatter",
    )(slot, aff, y, x2, norm_g)


def kernel(x, norm1_g, w_in, conv_w, conv_b, wq_m, wk_m, wv_m, w_if_fwd, b_if_fwd,
           w_if_bwd, b_if_bwd, mlstm_norm_g, mlstm_skip, attn_norm_g, w_out, norm2_g,
           w_router, w1, w3, w2, norm_f_g):
    b, seq, _ = x.shape
    assert w_in.shape[0] == 1, "single-layer problem"
    assert seq % ROW_TILE == 0 and seq % MLSTM_CHUNK == 0 and seq % ATTN_Q_TILE == 0
    cap = EC_CAPACITY * seq // N_EXPERTS
    assert FFN_ROW_TILE % cap == 0 and (b * cap) % FFN_ROW_TILE == 0
    l = 0
    x2d = x.reshape(b * seq, D_MODEL)
    xm, z, *qkv_views = _in_proj(x2d, norm1_g[l][None, :], w_in[l].astype(BF16), seq)
    shp = lambda t: t.reshape(b, seq, t.shape[-1])
    wif_rows, bif_rows = _gate_rows(w_if_fwd[l], b_if_fwd[l], w_if_bwd[l], b_if_bwd[l])
    xc, qm, ktm, vm, gates = _mlstm_pre(
        shp(xm), conv_w[l], conv_b[l][None, :],
        _block_diag(wq_m[l]).astype(BF16), _block_diag(wk_m[l]).T.astype(BF16),
        _block_diag(wv_m[l]).astype(BF16), wif_rows.astype(BF16), bif_rows)
    ym = _mlstm(qm, ktm, vm, gates, xc, shp(z), mlstm_norm_g[l][None, :],
                mlstm_skip[l][None, :])
    branches = [_band_attn(*qkv_views[3 * d:3 * d + 3], seq, win, dil)
                for d, (win, dil) in enumerate(DILATED_PATTERNS)]
    wr_pad = jnp.pad(w_router[l], ((0, 0), (0, LANES - N_EXPERTS)))
    x2, h2, logits = _out_proj(
        ym.reshape(b * seq, D_MLSTM), [o for o, _ in branches], [s for _, s in branches], x2d,
        attn_norm_g[l][None, :], w_out[l].astype(BF16), norm2_g[l][None, :], wr_pad)
    slot, slot_t, aff = _route(logits.reshape(b, seq, LANES), cap)
    xs = _moe_gather(slot_t, h2.reshape(b, seq, D_MODEL), cap)
    y = _moe_ffn(xs, w1[l], w3[l], w2[l])
    return _moe_scatter(slot, aff, y, x2.reshape(b, seq, D_MODEL), norm_f_g[None, :])
```
